```python
import jax, jax.numpy as jnp
from jax import lax
import numpy as np

D_MODEL = 2048
BATCH = 16
SEQ = 2048
DEPTH = 4

GRID_W = 64
CTX_LEN = 256
EPS = 1e-6
NEG = -1e30

MIX_WIDTH = D_MODEL
CONV_WIDTH = MIX_WIDTH // 4
FOURIER_WIDTH = MIX_WIDTH // 4
FOURIER_GROUPS = 4
HEAD_DIM = 64
ATTN_WIDTH = MIX_WIDTH // 4
ATTN_HEADS = ATTN_WIDTH // HEAD_DIM
ATTN_KV_HEADS = ATTN_HEADS // 4
KV_WIDTH = ATTN_KV_HEADS * HEAD_DIM
WINDOW = 128
BLOCK = 128
SCALE = HEAD_DIM ** -0.5
ROPE_THETA = 10000.0
MLP_WIDTH = MIX_WIDTH // 4
MLP_GROUPS = 4
CHUNK = 128
D_FF = -(-8 * D_MODEL // (3 * 256)) * 256

OFF_AX = 0
OFF_AB = OFF_AX + CONV_WIDTH
OFF_AC = OFF_AB + CONV_WIDTH
OFF_F = OFF_AC + CONV_WIDTH
OFF_Q = OFF_F + FOURIER_WIDTH
OFF_K = OFF_Q + ATTN_WIDTH
OFF_V = OFF_K + KV_WIDTH
OFF_U = OFF_V + KV_WIDTH
OFF_MV = OFF_U + MLP_WIDTH
N_PROJ = OFF_MV + MLP_WIDTH

kernel_name = "hybrid_parallel_group_diffusion_trunk"


def rmsnorm(x, g):
    xf = x.astype(jnp.float32)
    y = xf * lax.rsqrt(jnp.mean(xf * xf, axis=-1, keepdims=True) + EPS)
    return (y * g.astype(jnp.float32)).astype(x.dtype)


def short_conv(xt, gb, gc, w_conv):
    z = gc * xt
    zp = jnp.pad(z, ((0, 0), (1, 1), (0, 0)))
    y = zp[:, :-2] * w_conv[0] + zp[:, 1:-1] * w_conv[1] + zp[:, 2:] * w_conv[2]
    return gb * y


def fourier_mix(z):
    b, n_pos, _ = z.shape
    zg = z.reshape(b, n_pos, FOURIER_GROUPS, FOURIER_WIDTH // FOURIER_GROUPS).astype(jnp.float32)
    f = jnp.fft.fftn(zg, axes=(1, 3), norm='ortho')
    return jnp.real(f).astype(z.dtype).reshape(b, n_pos, FOURIER_WIDTH)


def axial_rope(t, row, col):
    half = HEAD_DIM // 2
    quarter = half // 2
    inv = ROPE_THETA ** (-jnp.arange(quarter, dtype=jnp.float32) / quarter)

    def rot(ta, pos):
        ang = pos.astype(jnp.float32)[:, None] * inv[None, :]
        cs = jnp.cos(ang)[None, :, None, :].astype(t.dtype)
        sn = jnp.sin(ang)[None, :, None, :].astype(t.dtype)
        t1, t2 = ta[..., :quarter], ta[..., quarter:]
        return jnp.concatenate([t1 * cs - t2 * sn, t2 * cs + t1 * sn], axis=-1)

    return jnp.concatenate([rot(t[..., :half], row), rot(t[..., half:], col)], axis=-1)


def latent_attention(q, k, v, kc, vc, sink):
    b, n_pos = q.shape[0], q.shape[1]
    n_blk = n_pos // BLOCK
    lc = kc.shape[1]
    grp = ATTN_HEADS // ATTN_KV_HEADS
    qb = q.reshape(b, n_blk, BLOCK, ATTN_KV_HEADS, grp, HEAD_DIM)

    def band(t):
        tp = jnp.pad(t, ((0, 0), (BLOCK, BLOCK), (0, 0), (0, 0)))
        tp = tp.reshape(b, n_blk + 2, BLOCK, ATTN_KV_HEADS, HEAD_DIM)
        return jnp.concatenate([tp[:, :-2], tp[:, 1:-1], tp[:, 2:]], axis=2)

    kb, vb = band(k), band(v)
    s_loc = jnp.einsum('bnqkgd,bnjkd->bnkgqj', qb, kb).astype(jnp.float32) * SCALE
    s_ctx = jnp.einsum('bnqkgd,bckd->bnkgqc', qb, kc).astype(jnp.float32) * SCALE
    blk = jnp.arange(n_blk)[:, None, None]
    qpos = blk * BLOCK + jnp.arange(BLOCK)[None, :, None]
    kpos = (blk - 1) * BLOCK + jnp.arange(3 * BLOCK)[None, None, :]
    valid = (jnp.abs(qpos - kpos) <= WINDOW) & (kpos >= 0) & (kpos < n_pos)
    s_loc = jnp.where(valid[None, :, None, None], s_loc, NEG)
    sk = jnp.broadcast_to(sink.astype(jnp.float32).reshape(ATTN_KV_HEADS, grp)[None, None, :, :, None, None],
                          s_loc.shape[:-1] + (1,))
    probs = jax.nn.softmax(jnp.concatenate([s_loc, s_ctx, sk], axis=-1), axis=-1).astype(v.dtype)
    o = (jnp.einsum('bnkgqj,bnjkd->bnqkgd', probs[..., :3 * BLOCK], vb)
         + jnp.einsum('bnkgqc,bckd->bnqkgd', probs[..., 3 * BLOCK:3 * BLOCK + lc], vc))
    return o.reshape(b, n_pos, ATTN_WIDTH)


def context_attention(qc, kc, vc, sink):
    b, lc = qc.shape[0], qc.shape[1]
    grp = ATTN_HEADS // ATTN_KV_HEADS
    qg = qc.reshape(b, lc, ATTN_KV_HEADS, grp, HEAD_DIM)
    s = jnp.einsum('bqkgd,bckd->bkgqc', qg, kc).astype(jnp.float32) * SCALE
    sk = jnp.broadcast_to(sink.astype(jnp.float32).reshape(ATTN_KV_HEADS, grp)[None, :, :, None, None],
                          s.shape[:-1] + (1,))
    probs = jax.nn.softmax(jnp.concatenate([s, sk], axis=-1), axis=-1).astype(vc.dtype)
    o = jnp.einsum('bkgqc,bckd->bqkgd', probs[..., :lc], vc)
    return o.reshape(b, lc, ATTN_WIDTH)


def chunk_mlp(u, v, w_s, b_s):
    b, n_pos, _ = u.shape
    n_chunk = n_pos // CHUNK
    cg = MLP_WIDTH // MLP_GROUPS
    vf = v.reshape(b, n_chunk, CHUNK, MLP_GROUPS, cg).astype(jnp.float32)
    mu = jnp.mean(vf, axis=-1, keepdims=True)
    var = jnp.mean(jnp.square(vf - mu), axis=-1, keepdims=True)
    vn = ((vf - mu) * lax.rsqrt(var + EPS)).astype(v.dtype)
    mixed = jnp.einsum('gpq,bnqgc->bnpgc', w_s, vn) + b_s.T[None, None, :, :, None]
    return (u.reshape(b, n_chunk, CHUNK, MLP_GROUPS, cg) * mixed).reshape(b, n_pos, MLP_WIDTH)


def split_kv(pkv):
    b, n_pos = pkv.shape[0], pkv.shape[1]
    k = pkv[..., :KV_WIDTH].reshape(b, n_pos, ATTN_KV_HEADS, HEAD_DIM)
    v = pkv[..., KV_WIDTH:].reshape(b, n_pos, ATTN_KV_HEADS, HEAD_DIM)
    return k, v


def token_mixers(p, kc, vc, w_conv, w_s, b_s, sink, pos):
    b, n_pos = p.shape[0], p.shape[1]
    ya = short_conv(p[..., OFF_AX:OFF_AB], p[..., OFF_AB:OFF_AC], p[..., OFF_AC:OFF_F], w_conv)
    yb = fourier_mix(p[..., OFF_F:OFF_Q])
    q = p[..., OFF_Q:OFF_K].reshape(b, n_pos, ATTN_HEADS, HEAD_DIM)
    if pos is None:
        yc = context_attention(q, kc, vc, sink)
    else:
        row, col = pos
        k, v = split_kv(p[..., OFF_K:OFF_U])
        yc = latent_attention(axial_rope(q, row, col), axial_rope(k, row, col), v, kc, vc, sink)
    yd = chunk_mlp(jax.nn.gelu(p[..., OFF_U:OFF_MV]), jax.nn.gelu(p[..., OFF_MV:N_PROJ]), w_s, b_s)
    return jnp.concatenate([ya, yb, yc, yd], axis=-1)


def swiglu(h, w_gate, w_up, w_down):
    return (jax.nn.silu(h @ w_gate) * (h @ w_up)) @ w_down


def setup_inputs(seed: int = 0) -> dict:
    key = jax.random.key(seed)
    ks = jax.random.split(key, 18)
    f32 = jnp.float32
    nrm = lambda k, shape, s: jax.random.normal(k, shape, f32) * s
    return {
        'x': nrm(ks[0], (BATCH, SEQ, D_MODEL), 1.0),
        'c': nrm(ks[1], (BATCH, D_MODEL), 1.0),
        'ctx': nrm(ks[2], (BATCH, CTX_LEN, D_MODEL), 1.0),
        'c_ctx': nrm(ks[3], (D_MODEL,), 1.0),
        'w_ada': nrm(ks[4], (DEPTH, D_MODEL, 6 * D_MODEL), 0.5 * D_MODEL ** -0.5),
        'b_ada': nrm(ks[5], (DEPTH, 6 * D_MODEL), 0.02),
        'g_norm1': 1.0 + nrm(ks[6], (DEPTH, D_MODEL), 0.02),
        'w_in': nrm(ks[7], (DEPTH, D_MODEL, N_PROJ), D_MODEL ** -0.5),
        'w_conv': nrm(ks[8], (DEPTH, 3, CONV_WIDTH), 3 ** -0.5),
        'sink': nrm(ks[9], (DEPTH, ATTN_HEADS), 0.5),
        'w_s': nrm(ks[10], (DEPTH, MLP_GROUPS, CHUNK, CHUNK), CHUNK ** -0.5),
        'b_s': 1.0 + nrm(ks[11], (DEPTH, MLP_GROUPS, CHUNK), 0.1),
        'w_out': nrm(ks[12], (DEPTH, MIX_WIDTH, D_MODEL), MIX_WIDTH ** -0.5),
        'g_norm2': 1.0 + nrm(ks[13], (DEPTH, D_MODEL), 0.02),
        'w_gate': nrm(ks[14], (DEPTH, D_MODEL, D_FF), D_MODEL ** -0.5),
        'w_up': nrm(ks[15], (DEPTH, D_MODEL, D_FF), D_MODEL ** -0.5),
        'w_down': nrm(ks[16], (DEPTH, D_FF, D_MODEL), D_FF ** -0.5),
        'g_final': 1.0 + nrm(ks[17], (D_MODEL,), 0.02),
    }


def reference(x, c, ctx, c_ctx, w_ada, b_ada, g_norm1, w_in, w_conv, sink, w_s, b_s, w_out, g_norm2,
              w_gate, w_up, w_down, g_final):
    n_pos = x.shape[1]
    ROWS = n_pos // GRID_W
    row = jnp.repeat(jnp.arange(ROWS, dtype=jnp.int32), GRID_W)
    col = jnp.tile(jnp.arange(GRID_W, dtype=jnp.int32), ROWS)
    silu_c = jax.nn.silu(c)
    silu_cc = jax.nn.silu(c_ctx)
    xc = ctx
    for l in range(DEPTH):
        last = l == DEPTH - 1
        mod = (silu_c @ w_ada[l] + b_ada[l])[:, None, :]
        mod_c = silu_cc @ w_ada[l] + b_ada[l]
        sh1, sc1, gt1, sh2, sc2, gt2 = jnp.split(mod, 6, axis=-1)
        csh1, csc1, cgt1, csh2, csc2, cgt2 = jnp.split(mod_c, 6, axis=-1)
        h = rmsnorm(x, g_norm1[l]) * (1 + sc1) + sh1
        hc = rmsnorm(xc, g_norm1[l]) * (1 + csc1) + csh1
        if last:
            kc, vc = split_kv(hc @ w_in[l][:, OFF_K:OFF_U])
        else:
            pc = hc @ w_in[l]
            kc, vc = split_kv(pc[..., OFF_K:OFF_U])
            yc_ctx = token_mixers(pc, kc, vc, w_conv[l], w_s[l], b_s[l], sink[l], None)
            xc_new = xc + cgt1 * (yc_ctx @ w_out[l])
            hc2 = rmsnorm(xc_new, g_norm2[l]) * (1 + csc2) + csh2
            xc_next = xc_new + cgt2 * swiglu(hc2, w_gate[l], w_up[l], w_down[l])
        p = h @ w_in[l]
        y = token_mixers(p, kc, vc, w_conv[l], w_s[l], b_s[l], sink[l], (row, col))
        x = x + gt1 * (y @ w_out[l])
        h2 = rmsnorm(x, g_norm2[l]) * (1 + sc2) + sh2
        x = x + gt2 * swiglu(h2, w_gate[l], w_up[l], w_down[l])
        if not last:
            xc = xc_next
    return rmsnorm(x, g_final)
```

```python
import functools
import math

import jax
import jax.numpy as jnp
from jax import lax
from jax.experimental import pallas as pl
from jax.experimental.pallas import tpu as pltpu

F32 = jnp.float32
BF16 = jnp.bfloat16

EPS = 1e-6
NEG = -1e30
GRID_W = 64
HEAD_DIM = 64
N_HEADS = 8
N_KV_HEADS = 2
HEADS_PER_KV = N_HEADS // N_KV_HEADS
BLOCK = 128
CHUNK = 128
N_GROUPS = 4
ROPE_THETA = 10000.0
SCALE = HEAD_DIM ** -0.5

LANES = 128
VMEM_LIMIT_BYTES = 60 * 1024 * 1024
N_MOD_ROWS = 24

SH1, SC1, GT1, SH2, SC2, GT2 = range(6)


def _cparams(sem):
    return pltpu.CompilerParams(dimension_semantics=sem, vmem_limit_bytes=VMEM_LIMIT_BYTES)


def _resident(block_shape, index_map):
    return pl.BlockSpec(block_shape, index_map, pipeline_mode=pl.Buffered(1))


def _ada_kernel(c_ref, w_ref, b_ref, o_ref):
    c = c_ref[...]
    s = (c * jax.nn.sigmoid(c)).astype(BF16)
    o_ref[...] = jnp.dot(s, w_ref[...].astype(BF16), preferred_element_type=F32) + b_ref[...]


def _ada_call(cs, w_ada, b_ada):
    depth, d, n = w_ada.shape
    tn = 1024
    return pl.pallas_call(
        _ada_kernel,
        grid=(depth, n // tn),
        in_specs=[
            pl.BlockSpec((N_MOD_ROWS, d), lambda l, j: (0, 0)),
            pl.BlockSpec((None, d, tn), lambda l, j: (l, 0, j)),
            pl.BlockSpec((None, 1, tn), lambda l, j: (l, 0, j)),
        ],
        out_specs=pl.BlockSpec((None, N_MOD_ROWS, tn), lambda l, j: (l, 0, j)),
        out_shape=jax.ShapeDtypeStruct((depth, N_MOD_ROWS, n), F32),
        compiler_params=_cparams(("parallel", "parallel")),
        name="ada_mod",
    )(cs, w_ada, b_ada.reshape(depth, 1, n))


def _mod_spec(layer, row, chunk, d):
    if row is None:
        return pl.BlockSpec((None, None, 1, d), lambda b, *_: (layer, b, 0, chunk))
    return pl.BlockSpec((None, None, 1, d), lambda b, *_: (layer, row, 0, chunk))


def _norm_modulate_rows(x_ref, g_ref, sc_ref, sh_ref, h_ref, tm, rb):
    def body(r, carry):
        rows = pl.ds(pl.multiple_of(r * rb, rb), rb)
        xr = x_ref[0, rows, :]
        ms = jnp.mean(xr * xr, axis=-1, keepdims=True)
        y = xr * lax.rsqrt(ms + EPS)
        h = (y * g_ref[...]) * (1.0 + sc_ref[...]) + sh_ref[...]
        h_ref[rows, :] = h.astype(h_ref.dtype)
        return carry

    lax.fori_loop(0, tm // rb, body, 0)


def _norm_in_kernel(x_ref, g_ref, sc_ref, sh_ref, w_ref, o_ref, h_ref, *, tm, tn):
    _norm_modulate_rows(x_ref, g_ref, sc_ref, sh_ref, h_ref, tm, min(tm, 64))
    n = o_ref.shape[-1]
    for j in range(n // tn):
        cols = slice(j * tn, (j + 1) * tn)
        o_ref[0, :, cols] = jnp.dot(h_ref[...], w_ref[:, cols], preferred_element_type=F32).astype(o_ref.dtype)


def _norm_in_call(x, mods, g, w, layer, w_layer, mod_row, tn):
    b, l, d = x.shape
    n = w.shape[-1]
    tm = min(512, l)
    kern = functools.partial(_norm_in_kernel, tm=tm, tn=tn)
    return pl.pallas_call(
        kern,
        grid=(b, l // tm),
        in_specs=[
            pl.BlockSpec((1, tm, d), lambda bi, i: (bi, i, 0)),
            pl.BlockSpec((None, 1, d), lambda bi, i: (layer, 0, 0)),
            _mod_spec(layer, mod_row, SC1, d),
            _mod_spec(layer, mod_row, SH1, d),
            _resident((None, d, n), lambda bi, i: (w_layer, 0, 0)),
        ],
        out_specs=pl.BlockSpec((1, tm, n), lambda bi, i: (bi, i, 0)),
        out_shape=jax.ShapeDtypeStruct((b, l, n), BF16),
        scratch_shapes=[pltpu.VMEM((tm, d), BF16)],
        compiler_params=_cparams(("parallel", "parallel")),
        name="norm_in",
    )(x, g, mods, mods, w)


def _conv_kernel(x_ref, gb_ref, gc_ref, w_ref, o_ref):
    n_pos = x_ref.shape[1]
    z = gc_ref[0].astype(F32) * x_ref[0].astype(F32)
    row = lax.broadcasted_iota(jnp.int32, z.shape, 0)
    z_prev = jnp.where(row == 0, 0.0, pltpu.roll(z, 1, 0))
    z_next = jnp.where(row == n_pos - 1, 0.0, pltpu.roll(z, n_pos - 1, 0))
    y = z_prev * w_ref[0:1, :] + z * w_ref[1:2, :] + z_next * w_ref[2:3, :]
    o_ref[0] = (gb_ref[0].astype(F32) * y).astype(o_ref.dtype)


def _conv_call(p, w_conv, layer):
    b, l, _ = p.shape
    width = w_conv.shape[-1]
    nb = width // LANES
    blk = lambda off: pl.BlockSpec((1, l, LANES), lambda bi, j: (bi, 0, off + j))
    return pl.pallas_call(
        _conv_kernel,
        grid=(b, nb),
        in_specs=[blk(0), blk(nb), blk(2 * nb),
                  pl.BlockSpec((None, 3, LANES), lambda bi, j: (layer, 0, j))],
        out_specs=pl.BlockSpec((1, l, LANES), lambda bi, j: (bi, 0, j)),
        out_shape=jax.ShapeDtypeStruct((b, l, width), BF16),
        compiler_params=_cparams(("parallel", "parallel")),
        name="short_conv",
    )(p, p, p, w_conv)


def _fourier_kernel(z_ref, ccsc_ref, cpsp_ref, o_ref, rhs_ref, *, out_scale):
    n_pos = z_ref.shape[1]
    cw = ccsc_ref.shape[0]
    for g in range(N_GROUPS):
        cols = slice(g * cw, (g + 1) * cw)
        ab = jnp.dot(z_ref[0, :, cols], ccsc_ref[...], preferred_element_type=F32)
        rhs_ref[0:n_pos, cols] = ab[:, :cw].astype(BF16)
        rhs_ref[n_pos:2 * n_pos, cols] = ab[:, cw:].astype(BF16)
    out = jnp.dot(cpsp_ref[...], rhs_ref[...], preferred_element_type=F32)
    o_ref[0] = (out * out_scale).astype(o_ref.dtype)


def _fourier_call(p, ccsc, cpsp, col_block):
    b, l, _ = p.shape
    cw = ccsc.shape[0]
    width = N_GROUPS * cw
    kern = functools.partial(_fourier_kernel, out_scale=1.0 / math.sqrt(l * cw))
    return pl.pallas_call(
        kern,
        grid=(b,),
        in_specs=[
            pl.BlockSpec((1, l, width), lambda bi: (bi, 0, col_block)),
            _resident((cw, 2 * cw), lambda bi: (0, 0)),
            _resident((l, 2 * l), lambda bi: (0, 0)),
        ],
        out_specs=pl.BlockSpec((1, l, width), lambda bi: (bi, 0, 0)),
        out_shape=jax.ShapeDtypeStruct((b, l, width), BF16),
        scratch_shapes=[pltpu.VMEM((2 * l, width), BF16)],
        compiler_params=_cparams(("parallel",)),
        name="fourier_mix",
    )(p, ccsc, cpsp)


def _dft_tables(n):
    k = jnp.arange(n, dtype=jnp.int32)
    ang = ((k[:, None] * k[None, :]) % n).astype(F32) * (2.0 * math.pi / n)
    return jnp.cos(ang), jnp.sin(ang)


def _nt_dot(a, b):
    return lax.dot_general(a, b, (((1,), (1,)), ((), ())), preferred_element_type=F32)


def _latent_attn_kernel(sink_ref, q_ref, k_ref, v_ref, kc_ref, vc_ref, cos_ref, sin_ref, o_ref,
                        qs_ref, kpad_ref, vpad_ref, *, layer):
    n_pos = q_ref.shape[1]
    n_blk = n_pos // BLOCK
    kvw = k_ref.shape[-1]

    lane = lax.broadcasted_iota(jnp.int32, (BLOCK, LANES), 1)
    low = (lane % 32) < 16

    def rope_block(r, carry):
        rows = pl.ds(pl.multiple_of(r * BLOCK, BLOCK), BLOCK)
        cos = cos_ref[rows, :]
        sin = sin_ref[rows, :]

        def rope(t):
            swapped = jnp.where(low, pltpu.roll(t, LANES - 16, 1), pltpu.roll(t, 16, 1))
            return t * cos + swapped * sin

        for s in range(q_ref.shape[-1] // LANES):
            cols = slice(s * LANES, (s + 1) * LANES)
            qs_ref[rows, cols] = (rope(q_ref[0, rows, cols].astype(F32)) * SCALE).astype(BF16)
        prow = pl.ds(pl.multiple_of(r * BLOCK + BLOCK, BLOCK), BLOCK)
        kpad_ref[prow, :] = rope(k_ref[0, rows, :].astype(F32)).astype(BF16)
        vpad_ref[prow, :] = v_ref[0, rows, :]
        return carry

    zeros = jnp.zeros((BLOCK, kvw), BF16)
    for ref in (kpad_ref, vpad_ref):
        ref[0:BLOCK, :] = zeros
        ref[n_pos + BLOCK:n_pos + 2 * BLOCK, :] = zeros
    lax.fori_loop(0, n_blk, rope_block, 0)

    qi = lax.broadcasted_iota(jnp.int32, (BLOCK, 3 * BLOCK), 0)
    kj = lax.broadcasted_iota(jnp.int32, (BLOCK, 3 * BLOCK), 1)
    in_window = (kj >= qi) & (kj <= qi + 2 * BLOCK)

    def attn_block(n, carry):
        rows = pl.ds(pl.multiple_of(n * BLOCK, BLOCK), BLOCK)
        band = pl.ds(pl.multiple_of(n * BLOCK, BLOCK), 3 * BLOCK)
        kpos = (n - 1) * BLOCK + kj
        valid = in_window & (kpos >= 0) & (kpos < n_pos)
        kb = kpad_ref[band, :]
        vb = vpad_ref[band, :]
        qb = qs_ref[rows, :]
        outs = []
        for h in range(N_HEADS):
            kv = h // HEADS_PER_KV
            hc = slice(kv * HEAD_DIM, (kv + 1) * HEAD_DIM)
            qh = qb[:, h * HEAD_DIM:(h + 1) * HEAD_DIM]
            s_loc = jnp.where(valid, _nt_dot(qh, kb[:, hc]), NEG)
            s_ctx = _nt_dot(qh, kc_ref[0, :, hc])
            sk = sink_ref[layer, h]
            m = jnp.maximum(jnp.maximum(jnp.max(s_loc, axis=-1, keepdims=True),
                                        jnp.max(s_ctx, axis=-1, keepdims=True)), sk)
            e_loc = jnp.exp(s_loc - m)
            e_ctx = jnp.exp(s_ctx - m)
            den = (jnp.sum(e_loc, axis=-1, keepdims=True) + jnp.sum(e_ctx, axis=-1, keepdims=True)
                   + jnp.exp(sk - m))
            o = (jnp.dot(e_loc.astype(BF16), vb[:, hc], preferred_element_type=F32)
                 + jnp.dot(e_ctx.astype(BF16), vc_ref[0, :, hc], preferred_element_type=F32))
            outs.append(o / den)
        o_ref[0, rows, :] = jnp.concatenate(outs, axis=-1).astype(o_ref.dtype)
        return carry

    lax.fori_loop(0, n_blk, attn_block, 0)


def _latent_attn_call(p, pc, kc_block, sink, cos_t, sin_t, layer):
    b, l, _ = p.shape
    lc = pc.shape[1]
    qw = N_HEADS * HEAD_DIM
    kvw = N_KV_HEADS * HEAD_DIM
    q_block = 2048 // qw
    k_block = 2560 // kvw
    kern = functools.partial(_latent_attn_kernel, layer=layer)
    return pl.pallas_call(
        kern,
        grid=(b,),
        in_specs=[
            pl.BlockSpec(memory_space=pltpu.SMEM),
            pl.BlockSpec((1, l, qw), lambda bi: (bi, 0, q_block)),
            pl.BlockSpec((1, l, kvw), lambda bi: (bi, 0, k_block)),
            pl.BlockSpec((1, l, kvw), lambda bi: (bi, 0, k_block + 1)),
            pl.BlockSpec((1, lc, kvw), lambda bi: (bi, 0, kc_block)),
            pl.BlockSpec((1, lc, kvw), lambda bi: (bi, 0, kc_block + 1)),
            _resident((l, LANES), lambda bi: (0, 0)),
            _resident((l, LANES), lambda bi: (0, 0)),
        ],
        out_specs=pl.BlockSpec((1, l, qw), lambda bi: (bi, 0, 0)),
        out_shape=jax.ShapeDtypeStruct((b, l, qw), BF16),
        scratch_shapes=[pltpu.VMEM((l, qw), BF16),
                        pltpu.VMEM((l + 2 * BLOCK, kvw), BF16),
                        pltpu.VMEM((l + 2 * BLOCK, kvw), BF16)],
        compiler_params=_cparams(("parallel",)),
        name="latent_attention",
    )(sink, p, p, p, pc, pc, cos_t, sin_t)


def _ctx_attn_kernel(sink_ref, q_ref, kc_ref, vc_ref, o_ref, *, layer):
    outs = []
    for h in range(N_HEADS):
        kv = h // HEADS_PER_KV
        hc = slice(kv * HEAD_DIM, (kv + 1) * HEAD_DIM)
        qh = q_ref[0, :, h * HEAD_DIM:(h + 1) * HEAD_DIM] * SCALE
        s = _nt_dot(qh, kc_ref[0, :, hc])
        sk = sink_ref[layer, h]
        m = jnp.maximum(jnp.max(s, axis=-1, keepdims=True), sk)
        e = jnp.exp(s - m)
        den = jnp.sum(e, axis=-1, keepdims=True) + jnp.exp(sk - m)
        o = jnp.dot(e.astype(BF16), vc_ref[0, :, hc], preferred_element_type=F32)
        outs.append(o / den)
    o_ref[0] = jnp.concatenate(outs, axis=-1).astype(o_ref.dtype)


def _ctx_attn_call(pc, sink, layer):
    b, lc, _ = pc.shape
    qw = N_HEADS * HEAD_DIM
    kvw = N_KV_HEADS * HEAD_DIM
    kern = functools.partial(_ctx_attn_kernel, layer=layer)
    return pl.pallas_call(
        kern,
        grid=(b,),
        in_specs=[
            pl.BlockSpec(memory_space=pltpu.SMEM),
            pl.BlockSpec((1, lc, qw), lambda bi: (bi, 0, 2048 // qw)),
            pl.BlockSpec((1, lc, kvw), lambda bi: (bi, 0, 2560 // kvw)),
            pl.BlockSpec((1, lc, kvw), lambda bi: (bi, 0, 2560 // kvw + 1)),
        ],
        out_specs=pl.BlockSpec((1, lc, qw), lambda bi: (bi, 0, 0)),
        out_shape=jax.ShapeDtypeStruct((b, lc, qw), BF16),
        compiler_params=_cparams(("parallel",)),
        name="context_attention",
    )(sink, pc, pc, pc)


def _rope_tables(n_pos):
    quarter = HEAD_DIM // 4
    inv = ROPE_THETA ** (-jnp.arange(quarter, dtype=F32) / quarter)
    pos = jnp.arange(n_pos, dtype=jnp.int32)
    row = (pos // GRID_W).astype(F32)
    col = (pos % GRID_W).astype(F32)
    a_row = row[:, None] * inv[None, :]
    a_col = col[:, None] * inv[None, :]
    ang = jnp.concatenate([a_row, a_row, a_col, a_col], axis=-1)
    sign = jnp.tile(jnp.concatenate([-jnp.ones((quarter,), F32), jnp.ones((quarter,), F32)]), 2)
    reps = LANES // HEAD_DIM
    return jnp.tile(jnp.cos(ang), (1, reps)), jnp.tile(jnp.sin(ang) * sign, (1, reps))


def _chunk_mlp_kernel(u_ref, v_ref, w_ref, bias_ref, o_ref):
    n_pos = u_ref.shape[1]
    u = jax.nn.gelu(u_ref[0].astype(F32))
    v = jax.nn.gelu(v_ref[0].astype(F32))
    mu = jnp.mean(v, axis=-1, keepdims=True)
    dlt = v - mu
    var = jnp.mean(dlt * dlt, axis=-1, keepdims=True)
    vn = (dlt * lax.rsqrt(var + EPS)).astype(BF16)
    w = w_ref[...].astype(BF16)
    for n in range(n_pos // CHUNK):
        rows = slice(n * CHUNK, (n + 1) * CHUNK)
        mixed = jnp.dot(w, vn[rows], preferred_element_type=F32) + bias_ref[...]
        o_ref[0, rows, :] = (u[rows] * mixed).astype(o_ref.dtype)


def _chunk_mlp_call(p, w_s, bias, layer):
    b, l, _ = p.shape
    cg = LANES
    u_block = 2816 // cg
    v_block = 3328 // cg
    return pl.pallas_call(
        _chunk_mlp_kernel,
        grid=(b, N_GROUPS),
        in_specs=[
            pl.BlockSpec((1, l, cg), lambda bi, g: (bi, 0, u_block + g)),
            pl.BlockSpec((1, l, cg), lambda bi, g: (bi, 0, v_block + g)),
            pl.BlockSpec((None, None, CHUNK, CHUNK), lambda bi, g: (layer, g, 0, 0)),
            pl.BlockSpec((None, None, CHUNK, cg), lambda bi, g: (layer, g, 0, 0)),
        ],
        out_specs=pl.BlockSpec((1, l, cg), lambda bi, g: (bi, 0, g)),
        out_shape=jax.ShapeDtypeStruct((b, l, N_GROUPS * cg), BF16),
        compiler_params=_cparams(("parallel", "parallel")),
        name="chunk_mlp",
    )(p, p, w_s, bias)


def _out_proj_kernel(ya_ref, yb_ref, yc_ref, yd_ref, x_ref, gt_ref, w_ref, o_ref, *, tn):
    d = o_ref.shape[-1]
    kw = ya_ref.shape[-1]
    ys = (ya_ref, yb_ref, yc_ref, yd_ref)
    for j in range(d // tn):
        cols = slice(j * tn, (j + 1) * tn)
        acc = jnp.dot(ys[0][0], w_ref[0:kw, cols], preferred_element_type=F32)
        for k in range(1, 4):
            acc = acc + jnp.dot(ys[k][0], w_ref[k * kw:(k + 1) * kw, cols], preferred_element_type=F32)
        o_ref[0, :, cols] = x_ref[0, :, cols] + gt_ref[:, cols] * acc


def _out_proj_call(ys, x, mods, w_out, layer, mod_row):
    b, l, d = x.shape
    kw = ys[0].shape[-1]
    tm = min(512, l)
    kern = functools.partial(_out_proj_kernel, tn=512)
    yspec = pl.BlockSpec((1, tm, kw), lambda bi, i: (bi, i, 0))
    return pl.pallas_call(
        kern,
        grid=(b, l // tm),
        in_specs=[yspec, yspec, yspec, yspec,
                  pl.BlockSpec((1, tm, d), lambda bi, i: (bi, i, 0)),
                  _mod_spec(layer, mod_row, GT1, d),
                  _resident((None, d, d), lambda bi, i: (layer, 0, 0))],
        out_specs=pl.BlockSpec((1, tm, d), lambda bi, i: (bi, i, 0)),
        out_shape=jax.ShapeDtypeStruct((b, l, d), F32),
        compiler_params=_cparams(("parallel", "parallel")),
        name="out_proj",
    )(*ys, x, mods, w_out)


def _swiglu_kernel(x_ref, g_ref, sc_ref, sh_ref, gt_ref, wg_ref, wu_ref, wd_ref, gf_ref, o_ref, h_ref,
                   *, tm, final_norm):
    f = pl.program_id(2)

    @pl.when(f == 0)
    def _():
        _norm_modulate_rows(x_ref, g_ref, sc_ref, sh_ref, h_ref, tm, 64)
        o_ref[...] = x_ref[...]

    h = h_ref[...]
    gate = jnp.dot(h, wg_ref[...], preferred_element_type=F32)
    up = jnp.dot(h, wu_ref[...], preferred_element_type=F32)
    a = (gate * jax.nn.sigmoid(gate) * up).astype(BF16)
    o_ref[0] += gt_ref[...] * jnp.dot(a, wd_ref[...], preferred_element_type=F32)

    if final_norm:
        @pl.when(f == pl.num_programs(2) - 1)
        def _():
            def body(r, carry):
                rows = pl.ds(pl.multiple_of(r * 64, 64), 64)
                xr = o_ref[0, rows, :]
                ms = jnp.mean(xr * xr, axis=-1, keepdims=True)
                o_ref[0, rows, :] = (xr * lax.rsqrt(ms + EPS)) * gf_ref[...]
                return carry

            lax.fori_loop(0, tm // 64, body, 0)


def _swiglu_call(x, mods, g, w_gate, w_up, w_down, g_final, layer, mod_row, final_norm):
    b, l, d = x.shape
    ff = w_gate.shape[-1]
    tm = min(512, l)
    tf = 512
    kern = functools.partial(_swiglu_kernel, tm=tm, final_norm=final_norm)
    return pl.pallas_call(
        kern,
        grid=(b, l // tm, ff // tf),
        in_specs=[
            pl.BlockSpec((1, tm, d), lambda bi, i, f: (bi, i, 0)),
            pl.BlockSpec((None, 1, d), lambda bi, i, f: (layer, 0, 0)),
            _mod_spec(layer, mod_row, SC2, d),
            _mod_spec(layer, mod_row, SH2, d),
            _mod_spec(layer, mod_row, GT2, d),
            pl.BlockSpec((None, d, tf), lambda bi, i, f: (layer, 0, f)),
            pl.BlockSpec((None, d, tf), lambda bi, i, f: (layer, 0, f)),
            pl.BlockSpec((None, tf, d), lambda bi, i, f: (layer, f, 0)),
            pl.BlockSpec((1, d), lambda bi, i, f: (0, 0)),
        ],
        out_specs=pl.BlockSpec((1, tm, d), lambda bi, i, f: (bi, i, 0)),
        out_shape=jax.ShapeDtypeStruct((b, l, d), F32),
        scratch_shapes=[pltpu.VMEM((tm, d), BF16)],
        compiler_params=_cparams(("parallel", "parallel", "arbitrary")),
        name="swiglu",
    )(x, g, mods, mods, mods, w_gate, w_up, w_down, g_final)


def kernel(x, c, ctx, c_ctx, w_ada, b_ada, g_norm1, w_in, w_conv, sink, w_s, b_s, w_out, g_norm2,
           w_gate, w_up, w_down, g_final):
    batch, n_pos, d = x.shape
    n_ctx = ctx.shape[1]
    depth = w_ada.shape[0]
    cw = (d // 4) // N_GROUPS
    ctx_row = batch

    w_in_b = w_in.astype(BF16)
    w_out_b = w_out.astype(BF16)
    w_gate_b = w_gate.astype(BF16)
    w_up_b = w_up.astype(BF16)
    w_down_b = w_down.astype(BF16)
    g1 = g_norm1.reshape(depth, 1, d)
    g2 = g_norm2.reshape(depth, 1, d)
    gf = g_final.reshape(1, d)

    cs = jnp.concatenate([c, c_ctx[None, :], jnp.zeros((N_MOD_ROWS - batch - 1, d), F32)], axis=0)
    mods = _ada_call(cs, w_ada, b_ada).reshape(depth, N_MOD_ROWS, 1, 6 * d)

    cos_c, sin_c = _dft_tables(cw)
    ccsc = jnp.concatenate([cos_c, sin_c], axis=1).astype(BF16)
    cos_p, sin_p = _dft_tables(n_pos)
    cpsp = jnp.concatenate([cos_p, -sin_p], axis=1).astype(BF16)
    cos_x, sin_x = _dft_tables(n_ctx)
    cpsp_ctx = jnp.concatenate([cos_x, -sin_x], axis=1).astype(BF16)
    rope_cos, rope_sin = _rope_tables(n_pos)
    mlp_bias = jnp.broadcast_to(b_s[:, :, :, None], b_s.shape + (cw,))
    f_block = 1536 // (N_GROUPS * cw)

    xc = ctx
    for layer in range(depth):
        last = layer == depth - 1
        if last:
            w_kv = w_in_b[layer:layer + 1, :, 2560:2816]
            pc = _norm_in_call(xc, mods, g1, w_kv, layer, 0, ctx_row, tn=256)
            kc_block = 0
        else:
            pc = _norm_in_call(xc, mods, g1, w_in_b, layer, layer, ctx_row, tn=768)
            kc_block = 2560 // (N_KV_HEADS * HEAD_DIM)
            ys_c = (_conv_call(pc, w_conv, layer),
                    _fourier_call(pc, ccsc, cpsp_ctx, f_block),
                    _ctx_attn_call(pc, sink, layer),
                    _chunk_mlp_call(pc, w_s, mlp_bias, layer))
            xc_new = _out_proj_call(ys_c, xc, mods, w_out_b, layer, ctx_row)
            xc_next = _swiglu_call(xc_new, mods, g2, w_gate_b, w_up_b, w_down_b, gf, layer, ctx_row, False)
        p = _norm_in_call(x, mods, g1, w_in_b, layer, layer, None, tn=768)
        ys = (_conv_call(p, w_conv, layer),
              _fourier_call(p, ccsc, cpsp, f_block),
              _latent_attn_call(p, pc, kc_block, sink, rope_cos, rope_sin, layer),
              _chunk_mlp_call(p, w_s, mlp_bias, layer))
        x = _out_proj_call(ys, x, mods, w_out_b, layer, None)
        x = _swiglu_call(x, mods, g2, w_gate_b, w_up_b, w_down_b, gf, layer, None, last)
        if not last:
            xc = xc_next
    return x
```

```python
import functools
import math

import jax
import jax.numpy as jnp
from jax import lax
from jax.experimental import pallas as pl
from jax.experimental.pallas import tpu as pltpu

F32 = jnp.float32
BF16 = jnp.bfloat16

EPS = 1e-6
NEG = -1e30
GRID_W = 64
HEAD_DIM = 64
N_HEADS = 8
N_KV_HEADS = 2
HEADS_PER_KV = N_HEADS // N_KV_HEADS
BLOCK = 128
CHUNK = 128
N_GROUPS = 4
ROPE_THETA = 10000.0
SCALE = HEAD_DIM ** -0.5

LANES = 128
VMEM_LIMIT_BYTES = 60 * 1024 * 1024
N_MOD_ROWS = 24

SH1, SC1, GT1, SH2, SC2, GT2 = range(6)


def _cparams(sem):
    return pltpu.CompilerParams(dimension_semantics=sem, vmem_limit_bytes=VMEM_LIMIT_BYTES)


def _resident(block_shape, index_map):
    return pl.BlockSpec(block_shape, index_map, pipeline_mode=pl.Buffered(1))


def _ada_kernel(c_ref, w_ref, b_ref, o_ref):
    c = c_ref[...]
    s = (c * jax.nn.sigmoid(c)).astype(BF16)
    o_ref[...] = jnp.dot(s, w_ref[...].astype(BF16), preferred_element_type=F32) + b_ref[...]


def _ada_call(cs, w_ada, b_ada):
    depth, d, n = w_ada.shape
    tn = 1024
    return pl.pallas_call(
        _ada_kernel,
        grid=(depth, n // tn),
        in_specs=[
            pl.BlockSpec((N_MOD_ROWS, d), lambda l, j: (0, 0)),
            pl.BlockSpec((None, d, tn), lambda l, j: (l, 0, j)),
            pl.BlockSpec((None, 1, tn), lambda l, j: (l, 0, j)),
        ],
        out_specs=pl.BlockSpec((None, N_MOD_ROWS, tn), lambda l, j: (l, 0, j)),
        out_shape=jax.ShapeDtypeStruct((depth, N_MOD_ROWS, n), F32),
        compiler_params=_cparams(("parallel", "parallel")),
        name="ada_mod",
    )(cs, w_ada, b_ada.reshape(depth, 1, n))


def _mod_spec(layer, row, chunk, d):
    if row is None:
        return pl.BlockSpec((None, None, 1, d), lambda b, *_: (layer, b, 0, chunk))
    return pl.BlockSpec((None, None, 1, d), lambda b, *_: (layer, row, 0, chunk))


def _norm_modulate_rows(x_ref, g_ref, sc_ref, sh_ref, h_ref, tm, rb):
    def body(r, carry):
        rows = pl.ds(pl.multiple_of(r * rb, rb), rb)
        xr = x_ref[0, rows, :]
        ms = jnp.mean(xr * xr, axis=-1, keepdims=True)
        y = xr * lax.rsqrt(ms + EPS)
        h = (y * g_ref[...]) * (1.0 + sc_ref[...]) + sh_ref[...]
        h_ref[rows, :] = h.astype(h_ref.dtype)
        return carry

    lax.fori_loop(0, tm // rb, body, 0)


def _norm_in_kernel(x_ref, g_ref, sc_ref, sh_ref, w_ref, o_ref, h_ref, *, tm, tn):
    _norm_modulate_rows(x_ref, g_ref, sc_ref, sh_ref, h_ref, tm, min(tm, 64))
    n = o_ref.shape[-1]
    for j in range(n // tn):
        cols = slice(j * tn, (j + 1) * tn)
        o_ref[0, :, cols] = jnp.dot(h_ref[...], w_ref[:, cols], preferred_element_type=F32).astype(o_ref.dtype)


def _norm_in_call(x, mods, g, w, layer, w_layer, mod_row, tn):
    b, l, d = x.shape
    n = w.shape[-1]
    tm = min(512, l)
    kern = functools.partial(_norm_in_kernel, tm=tm, tn=tn)
    return pl.pallas_call(
        kern,
        grid=(b, l // tm),
        in_specs=[
            pl.BlockSpec((1, tm, d), lambda bi, i: (bi, i, 0)),
            pl.BlockSpec((None, 1, d), lambda bi, i: (layer, 0, 0)),
            _mod_spec(layer, mod_row, SC1, d),
            _mod_spec(layer, mod_row, SH1, d),
            _resident((None, d, n), lambda bi, i: (w_layer, 0, 0)),
        ],
        out_specs=pl.BlockSpec((1, tm, n), lambda bi, i: (bi, i, 0)),
        out_shape=jax.ShapeDtypeStruct((b, l, n), BF16),
        scratch_shapes=[pltpu.VMEM((tm, d), BF16)],
        compiler_params=_cparams(("parallel", "parallel")),
        name="norm_in",
    )(x, g, mods, mods, w)


def _conv_kernel(x_ref, gb_ref, gc_ref, w_ref, o_ref):
    n_pos = x_ref.shape[1]
    z = gc_ref[0].astype(F32) * x_ref[0].astype(F32)
    row = lax.broadcasted_iota(jnp.int32, z.shape, 0)
    z_prev = jnp.where(row == 0, 0.0, pltpu.roll(z, 1, 0))
    z_next = jnp.where(row == n_pos - 1, 0.0, pltpu.roll(z, n_pos - 1, 0))
    y = z_prev * w_ref[0:1, :] + z * w_ref[1:2, :] + z_next * w_ref[2:3, :]
    o_ref[0] = (gb_ref[0].astype(F32) * y).astype(o_ref.dtype)


def _conv_call(p, w_conv, layer):
    b, l, _ = p.shape
    width = w_conv.shape[-1]
    nb = width // LANES
    blk = lambda off: pl.BlockSpec((1, l, LANES), lambda bi, j: (bi, 0, off + j))
    return pl.pallas_call(
        _conv_kernel,
        grid=(b, nb),
        in_specs=[blk(0), blk(nb), blk(2 * nb),
                  pl.BlockSpec((None, 3, LANES), lambda bi, j: (layer, 0, j))],
        out_specs=pl.BlockSpec((1, l, LANES), lambda bi, j: (bi, 0, j)),
        out_shape=jax.ShapeDtypeStruct((b, l, width), BF16),
        compiler_params=_cparams(("parallel", "parallel")),
        name="short_conv",
    )(p, p, p, w_conv)


def _fourier_kernel(z_ref, ccsc_ref, cpsp_ref, o_ref, rhs_ref, *, out_scale):
    n_pos = z_ref.shape[1]
    cw = ccsc_ref.shape[0]
    for g in range(N_GROUPS):
        cols = slice(g * cw, (g + 1) * cw)
        ab = jnp.dot(z_ref[0, :, cols], ccsc_ref[...], preferred_element_type=F32)
        rhs_ref[0:n_pos, cols] = ab[:, :cw].astype(BF16)
        rhs_ref[n_pos:2 * n_pos, cols] = ab[:, cw:].astype(BF16)
    out = jnp.dot(cpsp_ref[...], rhs_ref[...], preferred_element_type=F32)
    o_ref[0] = (out * out_scale).astype(o_ref.dtype)


def _fourier_call(p, ccsc, cpsp, col_block):
    b, l, _ = p.shape
    cw = ccsc.shape[0]
    width = N_GROUPS * cw
    kern = functools.partial(_fourier_kernel, out_scale=1.0 / math.sqrt(l * cw))
    return pl.pallas_call(
        kern,
        grid=(b,),
        in_specs=[
            pl.BlockSpec((1, l, width), lambda bi: (bi, 0, col_block)),
            _resident((cw, 2 * cw), lambda bi: (0, 0)),
            _resident((l, 2 * l), lambda bi: (0, 0)),
        ],
        out_specs=pl.BlockSpec((1, l, width), lambda bi: (bi, 0, 0)),
        out_shape=jax.ShapeDtypeStruct((b, l, width), BF16),
        scratch_shapes=[pltpu.VMEM((2 * l, width), BF16)],
        compiler_params=_cparams(("parallel",)),
        name="fourier_mix",
    )(p, ccsc, cpsp)


def _dft_tables(n):
    k = jnp.arange(n, dtype=jnp.int32)
    ang = ((k[:, None] * k[None, :]) % n).astype(F32) * (2.0 * math.pi / n)
    return jnp.cos(ang), jnp.sin(ang)


def _nt_dot(a, b):
    return lax.dot_general(a, b, (((1,), (1,)), ((), ())), preferred_element_type=F32)


def _latent_attn_kernel(sink_ref, q_ref, k_ref, v_ref, kc_ref, vc_ref, cos_ref, sin_ref, o_ref,
                        qs_ref, kpad_ref, vpad_ref, *, layer):
    n_pos = q_ref.shape[1]
    n_blk = n_pos // BLOCK
    kvw = k_ref.shape[-1]

    lane = lax.broadcasted_iota(jnp.int32, (BLOCK, LANES), 1)
    low = (lane % 32) < 16

    def rope_block(r, carry):
        rows = pl.ds(pl.multiple_of(r * BLOCK, BLOCK), BLOCK)
        cos = cos_ref[rows, :]
        sin = sin_ref[rows, :]

        def rope(t):
            swapped = jnp.where(low, pltpu.roll(t, LANES - 16, 1), pltpu.roll(t, 16, 1))
            return t * cos + swapped * sin

        for s in range(q_ref.shape[-1] // LANES):
            cols = slice(s * LANES, (s + 1) * LANES)
            qs_ref[rows, cols] = (rope(q_ref[0, rows, cols].astype(F32)) * SCALE).astype(BF16)
        prow = pl.ds(pl.multiple_of(r * BLOCK + BLOCK, BLOCK), BLOCK)
        kpad_ref[prow, :] = rope(k_ref[0, rows, :].astype(F32)).astype(BF16)
        vpad_ref[prow, :] = v_ref[0, rows, :]
        return carry

    zeros = jnp.zeros((BLOCK, kvw), BF16)
    for ref in (kpad_ref, vpad_ref):
        ref[0:BLOCK, :] = zeros
        ref[n_pos + BLOCK:n_pos + 2 * BLOCK, :] = zeros
    lax.fori_loop(0, n_blk, rope_block, 0)

    qi = lax.broadcasted_iota(jnp.int32, (BLOCK, 3 * BLOCK), 0)
    kj = lax.broadcasted_iota(jnp.int32, (BLOCK, 3 * BLOCK), 1)
    in_window = (kj >= qi) & (kj <= qi + 2 * BLOCK)

    def attn_block(n, carry):
        rows = pl.ds(pl.multiple_of(n * BLOCK, BLOCK), BLOCK)
        band = pl.ds(pl.multiple_of(n * BLOCK, BLOCK), 3 * BLOCK)
        kpos = (n - 1) * BLOCK + kj
        valid = in_window & (kpos >= 0) & (kpos < n_pos)
        kb = kpad_ref[band, :]
        vb = vpad_ref[band, :]
        qb = qs_ref[rows, :]
        outs = []
        for h in range(N_HEADS):
            kv = h // HEADS_PER_KV
            hc = slice(kv * HEAD_DIM, (kv + 1) * HEAD_DIM)
            qh = qb[:, h * HEAD_DIM:(h + 1) * HEAD_DIM]
            s_loc = jnp.where(valid, _nt_dot(qh, kb[:, hc]), NEG)
            s_ctx = _nt_dot(qh, kc_ref[0, :, hc])
            sk = sink_ref[layer, h]
            m = jnp.maximum(jnp.maximum(jnp.max(s_loc, axis=-1, keepdims=True),
                                        jnp.max(s_ctx, axis=-1, keepdims=True)), sk)
            e_loc = jnp.exp(s_loc - m)
            e_ctx = jnp.exp(s_ctx - m)
            den = (jnp.sum(e_loc, axis=-1, keepdims=True) + jnp.sum(e_ctx, axis=-1, keepdims=True)
                   + jnp.exp(sk - m))
            o = (jnp.dot(e_loc.astype(BF16), vb[:, hc], preferred_element_type=F32)
                 + jnp.dot(e_ctx.astype(BF16), vc_ref[0, :, hc], preferred_element_type=F32))
            outs.append(o / den)
        o_ref[0, rows, :] = jnp.concatenate(outs, axis=-1).astype(o_ref.dtype)
        return carry

    lax.fori_loop(0, n_blk, attn_block, 0)


def _latent_attn_call(p, pc, kc_block, sink, cos_t, sin_t, layer):
    b, l, _ = p.shape
    lc = pc.shape[1]
    qw = N_HEADS * HEAD_DIM
    kvw = N_KV_HEADS * HEAD_DIM
    q_block = 2048 // qw
    k_block = 2560 // kvw
    kern = functools.partial(_latent_attn_kernel, layer=layer)
    return pl.pallas_call(
        kern,
        grid=(b,),
        in_specs=[
            pl.BlockSpec(memory_space=pltpu.SMEM),
            pl.BlockSpec((1, l, qw), lambda bi: (bi, 0, q_block)),
            pl.BlockSpec((1, l, kvw), lambda bi: (bi, 0, k_block)),
            pl.BlockSpec((1, l, kvw), lambda bi: (bi, 0, k_block + 1)),
            pl.BlockSpec((1, lc, kvw), lambda bi: (bi, 0, kc_block)),
            pl.BlockSpec((1, lc, kvw), lambda bi: (bi, 0, kc_block + 1)),
            _resident((l, LANES), lambda bi: (0, 0)),
            _resident((l, LANES), lambda bi: (0, 0)),
        ],
        out_specs=pl.BlockSpec((1, l, qw), lambda bi: (bi, 0, 0)),
        out_shape=jax.ShapeDtypeStruct((b, l, qw), BF16),
        scratch_shapes=[pltpu.VMEM((l, qw), BF16),
                        pltpu.VMEM((l + 2 * BLOCK, kvw), BF16),
                        pltpu.VMEM((l + 2 * BLOCK, kvw), BF16)],
        compiler_params=_cparams(("parallel",)),
        name="latent_attention",
    )(sink, p, p, p, pc, pc, cos_t, sin_t)


def _ctx_attn_kernel(sink_ref, q_ref, kc_ref, vc_ref, o_ref, *, layer):
    outs = []
    for h in range(N_HEADS):
        kv = h // HEADS_PER_KV
        hc = slice(kv * HEAD_DIM, (kv + 1) * HEAD_DIM)
        qh = q_ref[0, :, h * HEAD_DIM:(h + 1) * HEAD_DIM] * SCALE
        s = _nt_dot(qh, kc_ref[0, :, hc])
        sk = sink_ref[layer, h]
        m = jnp.maximum(jnp.max(s, axis=-1, keepdims=True), sk)
        e = jnp.exp(s - m)
        den = jnp.sum(e, axis=-1, keepdims=True) + jnp.exp(sk - m)
        o = jnp.dot(e.astype(BF16), vc_ref[0, :, hc], preferred_element_type=F32)
        outs.append(o / den)
    o_ref[0] = jnp.concatenate(outs, axis=-1).astype(o_ref.dtype)


def _ctx_attn_call(pc, sink, layer):
    b, lc, _ = pc.shape
    qw = N_HEADS * HEAD_DIM
    kvw = N_KV_HEADS * HEAD_DIM
    kern = functools.partial(_ctx_attn_kernel, layer=layer)
    return pl.pallas_call(
        kern,
        grid=(b,),
        in_specs=[
            pl.BlockSpec(memory_space=pltpu.SMEM),
            pl.BlockSpec((1, lc, qw), lambda bi: (bi, 0, 2048 // qw)),
            pl.BlockSpec((1, lc, kvw), lambda bi: (bi, 0, 2560 // kvw)),
            pl.BlockSpec((1, lc, kvw), lambda bi: (bi, 0, 2560 // kvw + 1)),
        ],
        out_specs=pl.BlockSpec((1, lc, qw), lambda bi: (bi, 0, 0)),
        out_shape=jax.ShapeDtypeStruct((b, lc, qw), BF16),
        compiler_params=_cparams(("parallel",)),
        name="context_attention",
    )(sink, pc, pc, pc)


def _rope_tables(n_pos):
    quarter = HEAD_DIM // 4
    inv = ROPE_THETA ** (-jnp.arange(quarter, dtype=F32) / quarter)
    pos = jnp.arange(n_pos, dtype=jnp.int32)
    row = (pos // GRID_W).astype(F32)
    col = (pos % GRID_W).astype(F32)
    a_row = row[:, None] * inv[None, :]
    a_col = col[:, None] * inv[None, :]
    ang = jnp.concatenate([a_row, a_row, a_col, a_col], axis=-1)
    sign = jnp.tile(jnp.concatenate([-jnp.ones((quarter,), F32), jnp.ones((quarter,), F32)]), 2)
    reps = LANES // HEAD_DIM
    return jnp.tile(jnp.cos(ang), (1, reps)), jnp.tile(jnp.sin(ang) * sign, (1, reps))


def _chunk_mlp_kernel(u_ref, v_ref, w_ref, bias_ref, o_ref):
    n_pos = u_ref.shape[1]
    u = jax.nn.gelu(u_ref[0].astype(F32))
    v = jax.nn.gelu(v_ref[0].astype(F32))
    mu = jnp.mean(v, axis=-1, keepdims=True)
    dlt = v - mu
    var = jnp.mean(dlt * dlt, axis=-1, keepdims=True)
    vn = (dlt * lax.rsqrt(var + EPS)).astype(BF16)
    w = w_ref[...].astype(BF16)
    for n in range(n_pos // CHUNK):
        rows = slice(n * CHUNK, (n + 1) * CHUNK)
        mixed = jnp.dot(w, vn[rows], preferred_element_type=F32) + bias_ref[...]
        o_ref[0, rows, :] = (u[rows] * mixed).astype(o_ref.dtype)


def _chunk_mlp_call(p, w_s, bias, layer):
    b, l, _ = p.shape
    cg = LANES
    u_block = 2816 // cg
    v_block = 3328 // cg
    return pl.pallas_call(
        _chunk_mlp_kernel,
        grid=(b, N_GROUPS),
        in_specs=[
            pl.BlockSpec((1, l, cg), lambda bi, g: (bi, 0, u_block + g)),
            pl.BlockSpec((1, l, cg), lambda bi, g: (bi, 0, v_block + g)),
            pl.BlockSpec((None, None, CHUNK, CHUNK), lambda bi, g: (layer, g, 0, 0)),
            pl.BlockSpec((None, None, CHUNK, cg), lambda bi, g: (layer, g, 0, 0)),
        ],
        out_specs=pl.BlockSpec((1, l, cg), lambda bi, g: (bi, 0, g)),
        out_shape=jax.ShapeDtypeStruct((b, l, N_GROUPS * cg), BF16),
        compiler_params=_cparams(("parallel", "parallel")),
        name="chunk_mlp",
    )(p, p, w_s, bias)


def _out_proj_kernel(ya_ref, yb_ref, yc_ref, yd_ref, x_ref, gt_ref, w_ref, o_ref, *, tn):
    d = o_ref.shape[-1]
    kw = ya_ref.shape[-1]
    ys = (ya_ref, yb_ref, yc_ref, yd_ref)
    for j in range(d // tn):
        cols = slice(j * tn, (j + 1) * tn)
        acc = jnp.dot(ys[0][0], w_ref[0:kw, cols], preferred_element_type=F32)
        for k in range(1, 4):
            acc = acc + jnp.dot(ys[k][0], w_ref[k * kw:(k + 1) * kw, cols], preferred_element_type=F32)
        o_ref[0, :, cols] = x_ref[0, :, cols] + gt_ref[:, cols] * acc


def _out_proj_call(ys, x, mods, w_out, layer, mod_row):
    b, l, d = x.shape
    kw = ys[0].shape[-1]
    tm = min(1024, l)
    kern = functools.partial(_out_proj_kernel, tn=512)
    yspec = pl.BlockSpec((1, tm, kw), lambda bi, i: (bi, i, 0))
    return pl.pallas_call(
        kern,
        grid=(b, l // tm),
        in_specs=[yspec, yspec, yspec, yspec,
                  pl.BlockSpec((1, tm, d), lambda bi, i: (bi, i, 0)),
                  _mod_spec(layer, mod_row, GT1, d),
                  _resident((None, d, d), lambda bi, i: (layer, 0, 0))],
        out_specs=pl.BlockSpec((1, tm, d), lambda bi, i: (bi, i, 0)),
        out_shape=jax.ShapeDtypeStruct((b, l, d), F32),
        compiler_params=_cparams(("parallel", "parallel")),
        name="out_proj",
    )(*ys, x, mods, w_out)


def _swiglu_kernel(x_ref, g_ref, sc_ref, sh_ref, gt_ref, wg_ref, wu_ref, wd_ref, gf_ref, o_ref, h_ref,
                   *, tm, final_norm):
    f = pl.program_id(2)

    @pl.when(f == 0)
    def _():
        _norm_modulate_rows(x_ref, g_ref, sc_ref, sh_ref, h_ref, tm, 64)
        o_ref[...] = x_ref[...]

    h = h_ref[...]
    gate = jnp.dot(h, wg_ref[...], preferred_element_type=F32)
    up = jnp.dot(h, wu_ref[...], preferred_element_type=F32)
    a = (gate * jax.nn.sigmoid(gate) * up).astype(BF16)
    o_ref[0] += gt_ref[...] * jnp.dot(a, wd_ref[...], preferred_element_type=F32)

    if final_norm:
        @pl.when(f == pl.num_programs(2) - 1)
        def _():
            def body(r, carry):
                rows = pl.ds(pl.multiple_of(r * 64, 64), 64)
                xr = o_ref[0, rows, :]
                ms = jnp.mean(xr * xr, axis=-1, keepdims=True)
                o_ref[0, rows, :] = (xr * lax.rsqrt(ms + EPS)) * gf_ref[...]
                return carry

            lax.fori_loop(0, tm // 64, body, 0)


def _swiglu_call(x, mods, g, w_gate, w_up, w_down, g_final, layer, mod_row, final_norm):
    b, l, d = x.shape
    ff = w_gate.shape[-1]
    tm = min(1024, l)
    tf = 512
    kern = functools.partial(_swiglu_kernel, tm=tm, final_norm=final_norm)
    return pl.pallas_call(
        kern,
        grid=(b, l // tm, ff // tf),
        in_specs=[
            pl.BlockSpec((1, tm, d), lambda bi, i, f: (bi, i, 0)),
            pl.BlockSpec((None, 1, d), lambda bi, i, f: (layer, 0, 0)),
            _mod_spec(layer, mod_row, SC2, d),
            _mod_spec(layer, mod_row, SH2, d),
            _mod_spec(layer, mod_row, GT2, d),
            pl.BlockSpec((None, d, tf), lambda bi, i, f: (layer, 0, f)),
            pl.BlockSpec((None, d, tf), lambda bi, i, f: (layer, 0, f)),
            pl.BlockSpec((None, tf, d), lambda bi, i, f: (layer, f, 0)),
            pl.BlockSpec((1, d), lambda bi, i, f: (0, 0)),
        ],
        out_specs=pl.BlockSpec((1, tm, d), lambda bi, i, f: (bi, i, 0)),
        out_shape=jax.ShapeDtypeStruct((b, l, d), F32),
        scratch_shapes=[pltpu.VMEM((tm, d), BF16)],
        compiler_params=_cparams(("parallel", "parallel", "arbitrary")),
        name="swiglu",
    )(x, g, mods, mods, mods, w_gate, w_up, w_down, g_final)


def kernel(x, c, ctx, c_ctx, w_ada, b_ada, g_norm1, w_in, w_conv, sink, w_s, b_s, w_out, g_norm2,
           w_gate, w_up, w_down, g_final):
    batch, n_pos, d = x.shape
    n_ctx = ctx.shape[1]
    depth = w_ada.shape[0]
    cw = (d // 4) // N_GROUPS
    ctx_row = batch

    w_in_b = w_in.astype(BF16)
    w_out_b = w_out.astype(BF16)
    w_gate_b = w_gate.astype(BF16)
    w_up_b = w_up.astype(BF16)
    w_down_b = w_down.astype(BF16)
    g1 = g_norm1.reshape(depth, 1, d)
    g2 = g_norm2.reshape(depth, 1, d)
    gf = g_final.reshape(1, d)

    cs = jnp.concatenate([c, c_ctx[None, :], jnp.zeros((N_MOD_ROWS - batch - 1, d), F32)], axis=0)
    mods = _ada_call(cs, w_ada, b_ada).reshape(depth, N_MOD_ROWS, 1, 6 * d)

    cos_c, sin_c = _dft_tables(cw)
    ccsc = jnp.concatenate([cos_c, sin_c], axis=1).astype(BF16)
    cos_p, sin_p = _dft_tables(n_pos)
    cpsp = jnp.concatenate([cos_p, -sin_p], axis=1).astype(BF16)
    cos_x, sin_x = _dft_tables(n_ctx)
    cpsp_ctx = jnp.concatenate([cos_x, -sin_x], axis=1).astype(BF16)
    rope_cos, rope_sin = _rope_tables(n_pos)
    mlp_bias = jnp.broadcast_to(b_s[:, :, :, None], b_s.shape + (cw,))
    f_block = 1536 // (N_GROUPS * cw)

    def flat(a):
        return a.reshape(1, -1, a.shape[-1])

    def unflat(a):
        return a.reshape(batch, n_ctx, a.shape[-1])

    xc = flat(ctx)
    for layer in range(depth):
        last = layer == depth - 1
        if last:
            w_kv = w_in_b[layer:layer + 1, :, 2560:2816]
            pc = unflat(_norm_in_call(xc, mods, g1, w_kv, layer, 0, ctx_row, tn=256))
            kc_block = 0
        else:
            pc = unflat(_norm_in_call(xc, mods, g1, w_in_b, layer, layer, ctx_row, tn=768))
            kc_block = 2560 // (N_KV_HEADS * HEAD_DIM)
            ys_c = (flat(_conv_call(pc, w_conv, layer)),
                    flat(_fourier_call(pc, ccsc, cpsp_ctx, f_block)),
                    flat(_ctx_attn_call(pc, sink, layer)),
                    flat(_chunk_mlp_call(pc, w_s, mlp_bias, layer)))
            xc_new = _out_proj_call(ys_c, xc, mods, w_out_b, layer, ctx_row)
            xc_next = _swiglu_call(xc_new, mods, g2, w_gate_b, w_up_b, w_down_b, gf, layer, ctx_row, False)
        p = _norm_in_call(x, mods, g1, w_in_b, layer, layer, None, tn=768)
        ys = (_conv_call(p, w_conv, layer),
              _fourier_call(p, ccsc, cpsp, f_block),
              _latent_attn_call(p, pc, kc_block, sink, rope_cos, rope_sin, layer),
              _chunk_mlp_call(p, w_s, mlp_bias, layer))
        x = _out_proj_call(ys, x, mods, w_out_b, layer, None)
        x = _swiglu_call(x, mods, g2, w_gate_b, w_up_b, w_down_b, gf, layer, None, last)
        if not last:
            xc = xc_next
    return x
```

```python
import functools
import math

import jax
import jax.numpy as jnp
from jax import lax
from jax.experimental import pallas as pl
from jax.experimental.pallas import tpu as pltpu

F32 = jnp.float32
BF16 = jnp.bfloat16

EPS = 1e-6
NEG = -1e30
GRID_W = 64
HEAD_DIM = 64
N_HEADS = 8
N_KV_HEADS = 2
HEADS_PER_KV = N_HEADS // N_KV_HEADS
BLOCK = 128
CHUNK = 128
N_GROUPS = 4
ROPE_THETA = 10000.0
SCALE = HEAD_DIM ** -0.5
LOG2E = math.log2(math.e)

LANES = 128
VMEM_LIMIT_BYTES = 60 * 1024 * 1024
N_MOD_ROWS = 24

SH1, SC1, GT1, SH2, SC2, GT2 = range(6)


def _cparams(sem):
    return pltpu.CompilerParams(dimension_semantics=sem, vmem_limit_bytes=VMEM_LIMIT_BYTES)


def _resident(block_shape, index_map):
    return pl.BlockSpec(block_shape, index_map, pipeline_mode=pl.Buffered(1))


def _ada_kernel(c_ref, w_ref, b_ref, o_ref):
    c = c_ref[...]
    s = (c * jax.nn.sigmoid(c)).astype(BF16)
    o_ref[...] = jnp.dot(s, w_ref[...].astype(BF16), preferred_element_type=F32) + b_ref[...]


def _ada_call(cs, w_ada, b_ada):
    depth, d, n = w_ada.shape
    tn = 1024
    return pl.pallas_call(
        _ada_kernel,
        grid=(depth, n // tn),
        in_specs=[
            pl.BlockSpec((N_MOD_ROWS, d), lambda l, j: (0, 0)),
            pl.BlockSpec((None, d, tn), lambda l, j: (l, 0, j)),
            pl.BlockSpec((None, 1, tn), lambda l, j: (l, 0, j)),
        ],
        out_specs=pl.BlockSpec((None, N_MOD_ROWS, tn), lambda l, j: (l, 0, j)),
        out_shape=jax.ShapeDtypeStruct((depth, N_MOD_ROWS, n), F32),
        compiler_params=_cparams(("parallel", "parallel")),
        name="ada_mod",
    )(cs, w_ada, b_ada.reshape(depth, 1, n))


def _mod_spec(layer, row, chunk, d):
    if row is None:
        return pl.BlockSpec((None, None, 1, d), lambda b, *_: (layer, b, 0, chunk))
    return pl.BlockSpec((None, None, 1, d), lambda b, *_: (layer, row, 0, chunk))


def _norm_modulate_rows(x_ref, g_ref, sc_ref, sh_ref, h_ref, tm, rb):
    def body(r, carry):
        rows = pl.ds(pl.multiple_of(r * rb, rb), rb)
        xr = x_ref[0, rows, :]
        ms = jnp.mean(xr * xr, axis=-1, keepdims=True)
        y = xr * lax.rsqrt(ms + EPS)
        h = (y * g_ref[...]) * (1.0 + sc_ref[...]) + sh_ref[...]
        h_ref[rows, :] = h.astype(h_ref.dtype)
        return carry

    lax.fori_loop(0, tm // rb, body, 0)


def _norm_in_kernel(x_ref, g_ref, sc_ref, sh_ref, w_ref, o_ref, h_ref, *, tm, tn):
    _norm_modulate_rows(x_ref, g_ref, sc_ref, sh_ref, h_ref, tm, min(tm, 64))
    n = o_ref.shape[-1]
    for j in range(n // tn):
        cols = slice(j * tn, (j + 1) * tn)
        o_ref[0, :, cols] = jnp.dot(h_ref[...], w_ref[:, cols], preferred_element_type=F32).astype(o_ref.dtype)


def _norm_in_call(x, mods, g, w, layer, w_layer, mod_row, tn):
    b, l, d = x.shape
    n = w.shape[-1]
    tm = min(512, l)
    kern = functools.partial(_norm_in_kernel, tm=tm, tn=tn)
    return pl.pallas_call(
        kern,
        grid=(b, l // tm),
        in_specs=[
            pl.BlockSpec((1, tm, d), lambda bi, i: (bi, i, 0)),
            pl.BlockSpec((None, 1, d), lambda bi, i: (layer, 0, 0)),
            _mod_spec(layer, mod_row, SC1, d),
            _mod_spec(layer, mod_row, SH1, d),
            _resident((None, d, n), lambda bi, i: (w_layer, 0, 0)),
        ],
        out_specs=pl.BlockSpec((1, tm, n), lambda bi, i: (bi, i, 0)),
        out_shape=jax.ShapeDtypeStruct((b, l, n), BF16),
        scratch_shapes=[pltpu.VMEM((tm, d), BF16)],
        compiler_params=_cparams(("parallel", "parallel")),
        name="norm_in",
    )(x, g, mods, mods, w)


def _conv_kernel(x_ref, gb_ref, gc_ref, w_ref, o_ref):
    n_pos = x_ref.shape[1]
    z = gc_ref[0].astype(F32) * x_ref[0].astype(F32)
    row = lax.broadcasted_iota(jnp.int32, z.shape, 0)
    z_prev = jnp.where(row == 0, 0.0, pltpu.roll(z, 1, 0))
    z_next = jnp.where(row == n_pos - 1, 0.0, pltpu.roll(z, n_pos - 1, 0))
    y = z_prev * w_ref[0:1, :] + z * w_ref[1:2, :] + z_next * w_ref[2:3, :]
    o_ref[0] = (gb_ref[0].astype(F32) * y).astype(o_ref.dtype)


def _conv_call(p, w_conv, layer):
    b, l, _ = p.shape
    width = w_conv.shape[-1]
    nb = width // LANES
    blk = lambda off: pl.BlockSpec((1, l, LANES), lambda bi, j: (bi, 0, off + j))
    return pl.pallas_call(
        _conv_kernel,
        grid=(b, nb),
        in_specs=[blk(0), blk(nb), blk(2 * nb),
                  pl.BlockSpec((None, 3, LANES), lambda bi, j: (layer, 0, j))],
        out_specs=pl.BlockSpec((1, l, LANES), lambda bi, j: (bi, 0, j)),
        out_shape=jax.ShapeDtypeStruct((b, l, width), BF16),
        compiler_params=_cparams(("parallel", "parallel")),
        name="short_conv",
    )(p, p, p, w_conv)


def _fourier_kernel(z_ref, ccsc_ref, cpsp_ref, o_ref, rhs_ref, *, out_scale):
    n_pos = z_ref.shape[1]
    cw = ccsc_ref.shape[0]
    for g in range(N_GROUPS):
        cols = slice(g * cw, (g + 1) * cw)
        ab = jnp.dot(z_ref[0, :, cols], ccsc_ref[...], preferred_element_type=F32)
        rhs_ref[0:n_pos, cols] = ab[:, :cw].astype(BF16)
        rhs_ref[n_pos:2 * n_pos, cols] = ab[:, cw:].astype(BF16)
    out = jnp.dot(cpsp_ref[...], rhs_ref[...], preferred_element_type=F32)
    o_ref[0] = (out * out_scale).astype(o_ref.dtype)


def _fourier_call(p, ccsc, cpsp, col_block):
    b, l, _ = p.shape
    cw = ccsc.shape[0]
    width = N_GROUPS * cw
    kern = functools.partial(_fourier_kernel, out_scale=1.0 / math.sqrt(l * cw))
    return pl.pallas_call(
        kern,
        grid=(b,),
        in_specs=[
            pl.BlockSpec((1, l, width), lambda bi: (bi, 0, col_block)),
            _resident((cw, 2 * cw), lambda bi: (0, 0)),
            _resident((l, 2 * l), lambda bi: (0, 0)),
        ],
        out_specs=pl.BlockSpec((1, l, width), lambda bi: (bi, 0, 0)),
        out_shape=jax.ShapeDtypeStruct((b, l, width), BF16),
        scratch_shapes=[pltpu.VMEM((2 * l, width), BF16)],
        compiler_params=_cparams(("parallel",)),
        name="fourier_mix",
    )(p, ccsc, cpsp)


def _dft_tables(n):
    k = jnp.arange(n, dtype=jnp.int32)
    ang = ((k[:, None] * k[None, :]) % n).astype(F32) * (2.0 * math.pi / n)
    return jnp.cos(ang), jnp.sin(ang)


def _nt_dot(a, b):
    return lax.dot_general(a, b, (((1,), (1,)), ((), ())), preferred_element_type=F32)


def _latent_attn_kernel(sink_ref, q_ref, k_ref, v_ref, kc_ref, vc_ref, cos_ref, sin_ref, o_ref,
                        qlo_ref, qhi_ref, kpad_ref, kswp_ref, vt_ref, kcs_ref, vct_ref, *, layer):
    n_pos = q_ref.shape[1]
    n_blk = n_pos // BLOCK
    n_ctx = kc_ref.shape[1]
    half = LANES // 2

    lane = lax.broadcasted_iota(jnp.int32, (BLOCK, LANES), 1)
    low16 = (lane % 32) < 16
    lo_half = lane < half

    def prep_block(r, carry):
        rows = pl.ds(pl.multiple_of(r * BLOCK, BLOCK), BLOCK)
        cos = cos_ref[rows, :]
        sin = sin_ref[rows, :]

        def rope(t):
            swapped = jnp.where(low16, pltpu.roll(t, LANES - 16, 1), pltpu.roll(t, 16, 1))
            return t * cos + swapped * sin

        for s in range(q_ref.shape[-1] // LANES):
            cols = slice(s * LANES, (s + 1) * LANES)
            t = rope(q_ref[0, rows, cols].astype(F32)) * (SCALE * LOG2E)
            qlo_ref[rows, cols] = jnp.where(lo_half, t, 0.0).astype(BF16)
            qhi_ref[rows, cols] = jnp.where(lo_half, 0.0, t).astype(BF16)
        prow = pl.ds(pl.multiple_of(r * BLOCK + BLOCK, BLOCK), BLOCK)
        tk = rope(k_ref[0, rows, :].astype(F32))
        kpad_ref[prow, :] = tk.astype(BF16)
        kswp_ref[prow, :] = pltpu.roll(tk, half, 1).astype(BF16)
        vt_ref[r + 1] = v_ref[0, rows, :].astype(F32).T.astype(BF16)
        return carry

    zeros = jnp.zeros((BLOCK, LANES), BF16)
    for ref in (kpad_ref, kswp_ref):
        ref[0:BLOCK, :] = zeros
        ref[n_pos + BLOCK:n_pos + 2 * BLOCK, :] = zeros
    vt_ref[0] = zeros
    vt_ref[n_blk + 1] = zeros
    kcs_ref[...] = pltpu.roll(kc_ref[0].astype(F32), half, 1).astype(BF16)
    for cb in range(n_ctx // BLOCK):
        crow = slice(cb * BLOCK, (cb + 1) * BLOCK)
        vct_ref[:, crow] = vc_ref[0, crow, :].astype(F32).T.astype(BF16)
    lax.fori_loop(0, n_blk, prep_block, 0)

    head_order = (0, 2, 5, 7, 1, 3, 4, 6)
    col_of = {h: i for i, h in enumerate(head_order)}
    n_col = N_HEADS * BLOCK
    kj = lax.broadcasted_iota(jnp.int32, (3 * BLOCK, BLOCK), 0)
    qi = lax.broadcasted_iota(jnp.int32, (3 * BLOCK, BLOCK), 1)
    in_window = (kj >= qi) & (kj <= qi + 2 * BLOCK)
    sk = jnp.concatenate([jnp.full((1, BLOCK), sink_ref[layer, h] * LOG2E, F32) for h in head_order], axis=1)
    neg = NEG * LOG2E

    def slab(ref, rows, s):
        return ref[rows, s * LANES:(s + 1) * LANES]

    def attn_block(n, carry):
        rows = pl.ds(pl.multiple_of(n * BLOCK, BLOCK), BLOCK)
        band = pl.ds(pl.multiple_of(n * BLOCK, BLOCK), 3 * BLOCK)
        valid = in_window & (kj >= (1 - n) * BLOCK) & (kj < n_pos - (n - 1) * BLOCK)

        def masked(s):
            return jnp.concatenate([jnp.where(valid, s[:, i * BLOCK:(i + 1) * BLOCK], neg)
                                    for i in range(s.shape[1] // BLOCK)], axis=1)

        rhs_nat = jnp.concatenate([slab(qlo_ref, rows, 0), slab(qlo_ref, rows, 1),
                                   slab(qhi_ref, rows, 2), slab(qhi_ref, rows, 3)], axis=0)
        rhs_swp = jnp.concatenate([slab(qhi_ref, rows, 0), slab(qhi_ref, rows, 1),
                                   slab(qlo_ref, rows, 2), slab(qlo_ref, rows, 3)], axis=0)
        s_loc = jnp.concatenate([masked(_nt_dot(kpad_ref[band, :], rhs_nat)),
                                 masked(_nt_dot(kswp_ref[band, :], rhs_swp))], axis=1)
        s_ctx = jnp.concatenate([_nt_dot(kc_ref[0], rhs_nat),
                                 _nt_dot(kcs_ref[...], rhs_swp)], axis=1)
        m = jnp.maximum(jnp.maximum(jnp.max(s_loc, axis=0, keepdims=True),
                                    jnp.max(s_ctx, axis=0, keepdims=True)), sk)
        e_loc = jnp.exp2(s_loc - m)
        e_ctx = jnp.exp2(s_ctx - m)
        inv_den = 1.0 / (jnp.sum(e_loc, axis=0, keepdims=True) + jnp.sum(e_ctx, axis=0, keepdims=True)
                         + jnp.exp2(sk - m))
        e_loc = e_loc.astype(BF16)
        e_ctx = e_ctx.astype(BF16)
        vt_band = jnp.concatenate([vt_ref[n], vt_ref[n + 1], vt_ref[n + 2]], axis=1)
        o_t = {}
        for g in range(N_KV_HEADS):
            heads = [4 * g + j for j in range(HEADS_PER_KV)]
            pick = lambda arr: jnp.concatenate(
                [arr[:, col_of[h] * BLOCK:(col_of[h] + 1) * BLOCK] for h in heads], axis=1)
            vrows = slice(g * HEAD_DIM, (g + 1) * HEAD_DIM)
            o = (jnp.dot(vt_band[vrows], pick(e_loc), preferred_element_type=F32)
                 + jnp.dot(vct_ref[vrows, :], pick(e_ctx), preferred_element_type=F32)) * pick(inv_den)
            for j, h in enumerate(heads):
                o_t[h] = o[:, j * BLOCK:(j + 1) * BLOCK]
        slabs = [jnp.concatenate([o_t[2 * s], o_t[2 * s + 1]], axis=0).T for s in range(N_HEADS // 2)]
        o_ref[0, rows, :] = jnp.concatenate(slabs, axis=1).astype(o_ref.dtype)
        return carry

    lax.fori_loop(0, n_blk, attn_block, 0, unroll=2)


def _latent_attn_call(p, pc, kc_block, sink, cos_t, sin_t, layer):
    b, l, _ = p.shape
    lc = pc.shape[1]
    qw = N_HEADS * HEAD_DIM
    kvw = N_KV_HEADS * HEAD_DIM
    q_block = 2048 // qw
    k_block = 2560 // kvw
    kern = functools.partial(_latent_attn_kernel, layer=layer)
    return pl.pallas_call(
        kern,
        grid=(b,),
        in_specs=[
            pl.BlockSpec(memory_space=pltpu.SMEM),
            pl.BlockSpec((1, l, qw), lambda bi: (bi, 0, q_block)),
            pl.BlockSpec((1, l, kvw), lambda bi: (bi, 0, k_block)),
            pl.BlockSpec((1, l, kvw), lambda bi: (bi, 0, k_block + 1)),
            pl.BlockSpec((1, lc, kvw), lambda bi: (bi, 0, kc_block)),
            pl.BlockSpec((1, lc, kvw), lambda bi: (bi, 0, kc_block + 1)),
            _resident((l, LANES), lambda bi: (0, 0)),
            _resident((l, LANES), lambda bi: (0, 0)),
        ],
        out_specs=pl.BlockSpec((1, l, qw), lambda bi: (bi, 0, 0)),
        out_shape=jax.ShapeDtypeStruct((b, l, qw), BF16),
        scratch_shapes=[pltpu.VMEM((l, qw), BF16),
                        pltpu.VMEM((l, qw), BF16),
                        pltpu.VMEM((l + 2 * BLOCK, kvw), BF16),
                        pltpu.VMEM((l + 2 * BLOCK, kvw), BF16),
                        pltpu.VMEM((l // BLOCK + 2, kvw, BLOCK), BF16),
                        pltpu.VMEM((lc, kvw), BF16),
                        pltpu.VMEM((kvw, lc), BF16)],
        compiler_params=_cparams(("parallel",)),
        name="latent_attention",
    )(sink, p, p, p, pc, pc, cos_t, sin_t)


def _ctx_attn_kernel(sink_ref, q_ref, kc_ref, vc_ref, o_ref, *, layer):
    outs = []
    for h in range(N_HEADS):
        kv = h // HEADS_PER_KV
        hc = slice(kv * HEAD_DIM, (kv + 1) * HEAD_DIM)
        qh = q_ref[0, :, h * HEAD_DIM:(h + 1) * HEAD_DIM] * SCALE
        s = _nt_dot(qh, kc_ref[0, :, hc])
        sk = sink_ref[layer, h]
        m = jnp.maximum(jnp.max(s, axis=-1, keepdims=True), sk)
        e = jnp.exp(s - m)
        den = jnp.sum(e, axis=-1, keepdims=True) + jnp.exp(sk - m)
        o = jnp.dot(e.astype(BF16), vc_ref[0, :, hc], preferred_element_type=F32)
        outs.append(o / den)
    o_ref[0] = jnp.concatenate(outs, axis=-1).astype(o_ref.dtype)


def _ctx_attn_call(pc, sink, layer):
    b, lc, _ = pc.shape
    qw = N_HEADS * HEAD_DIM
    kvw = N_KV_HEADS * HEAD_DIM
    kern = functools.partial(_ctx_attn_kernel, layer=layer)
    return pl.pallas_call(
        kern,
        grid=(b,),
        in_specs=[
            pl.BlockSpec(memory_space=pltpu.SMEM),
            pl.BlockSpec((1, lc, qw), lambda bi: (bi, 0, 2048 // qw)),
            pl.BlockSpec((1, lc, kvw), lambda bi: (bi, 0, 2560 // kvw)),
            pl.BlockSpec((1, lc, kvw), lambda bi: (bi, 0, 2560 // kvw + 1)),
        ],
        out_specs=pl.BlockSpec((1, lc, qw), lambda bi: (bi, 0, 0)),
        out_shape=jax.ShapeDtypeStruct((b, lc, qw), BF16),
        compiler_params=_cparams(("parallel",)),
        name="context_attention",
    )(sink, pc, pc, pc)


def _rope_tables(n_pos):
    quarter = HEAD_DIM // 4
    inv = ROPE_THETA ** (-jnp.arange(quarter, dtype=F32) / quarter)
    pos = jnp.arange(n_pos, dtype=jnp.int32)
    row = (pos // GRID_W).astype(F32)
    col = (pos % GRID_W).astype(F32)
    a_row = row[:, None] * inv[None, :]
    a_col = col[:, None] * inv[None, :]
    ang = jnp.concatenate([a_row, a_row, a_col, a_col], axis=-1)
    sign = jnp.tile(jnp.concatenate([-jnp.ones((quarter,), F32), jnp.ones((quarter,), F32)]), 2)
    reps = LANES // HEAD_DIM
    return jnp.tile(jnp.cos(ang), (1, reps)), jnp.tile(jnp.sin(ang) * sign, (1, reps))


def _chunk_mlp_kernel(u_ref, v_ref, w_ref, bias_ref, o_ref):
    n_pos = u_ref.shape[1]
    u = jax.nn.gelu(u_ref[0].astype(F32))
    v = jax.nn.gelu(v_ref[0].astype(F32))
    mu = jnp.mean(v, axis=-1, keepdims=True)
    dlt = v - mu
    var = jnp.mean(dlt * dlt, axis=-1, keepdims=True)
    vn = (dlt * lax.rsqrt(var + EPS)).astype(BF16)
    w = w_ref[...].astype(BF16)
    for n in range(n_pos // CHUNK):
        rows = slice(n * CHUNK, (n + 1) * CHUNK)
        mixed = jnp.dot(w, vn[rows], preferred_element_type=F32) + bias_ref[...]
        o_ref[0, rows, :] = (u[rows] * mixed).astype(o_ref.dtype)


def _chunk_mlp_call(p, w_s, bias, layer):
    b, l, _ = p.shape
    cg = LANES
    u_block = 2816 // cg
    v_block = 3328 // cg
    return pl.pallas_call(
        _chunk_mlp_kernel,
        grid=(b, N_GROUPS),
        in_specs=[
            pl.BlockSpec((1, l, cg), lambda bi, g: (bi, 0, u_block + g)),
            pl.BlockSpec((1, l, cg), lambda bi, g: (bi, 0, v_block + g)),
            pl.BlockSpec((None, None, CHUNK, CHUNK), lambda bi, g: (layer, g, 0, 0)),
            pl.BlockSpec((None, None, CHUNK, cg), lambda bi, g: (layer, g, 0, 0)),
        ],
        out_specs=pl.BlockSpec((1, l, cg), lambda bi, g: (bi, 0, g)),
        out_shape=jax.ShapeDtypeStruct((b, l, N_GROUPS * cg), BF16),
        compiler_params=_cparams(("parallel", "parallel")),
        name="chunk_mlp",
    )(p, p, w_s, bias)


def _out_proj_kernel(ya_ref, yb_ref, yc_ref, yd_ref, x_ref, gt_ref, w_ref, o_ref, *, tn):
    d = o_ref.shape[-1]
    kw = ya_ref.shape[-1]
    ys = (ya_ref, yb_ref, yc_ref, yd_ref)
    for j in range(d // tn):
        cols = slice(j * tn, (j + 1) * tn)
        acc = jnp.dot(ys[0][0], w_ref[0:kw, cols], preferred_element_type=F32)
        for k in range(1, 4):
            acc = acc + jnp.dot(ys[k][0], w_ref[k * kw:(k + 1) * kw, cols], preferred_element_type=F32)
        o_ref[0, :, cols] = x_ref[0, :, cols] + gt_ref[:, cols] * acc


def _out_proj_call(ys, x, mods, w_out, layer, mod_row):
    b, l, d = x.shape
    kw = ys[0].shape[-1]
    tm = min(1024, l)
    kern = functools.partial(_out_proj_kernel, tn=512)
    yspec = pl.BlockSpec((1, tm, kw), lambda bi, i: (bi, i, 0))
    return pl.pallas_call(
        kern,
        grid=(b, l // tm),
        in_specs=[yspec, yspec, yspec, yspec,
                  pl.BlockSpec((1, tm, d), lambda bi, i: (bi, i, 0)),
                  _mod_spec(layer, mod_row, GT1, d),
                  _resident((None, d, d), lambda bi, i: (layer, 0, 0))],
        out_specs=pl.BlockSpec((1, tm, d), lambda bi, i: (bi, i, 0)),
        out_shape=jax.ShapeDtypeStruct((b, l, d), F32),
        compiler_params=_cparams(("parallel", "parallel")),
        name="out_proj",
    )(*ys, x, mods, w_out)


def _swiglu_kernel(x_ref, g_ref, sc_ref, sh_ref, gt_ref, wg_ref, wu_ref, wd_ref, gf_ref, o_ref, h_ref,
                   *, tm, final_norm):
    f = pl.program_id(2)

    @pl.when(f == 0)
    def _():
        _norm_modulate_rows(x_ref, g_ref, sc_ref, sh_ref, h_ref, tm, 64)
        o_ref[...] = x_ref[...]

    h = h_ref[...]
    gate = jnp.dot(h, wg_ref[...], preferred_element_type=F32)
    up = jnp.dot(h, wu_ref[...], preferred_element_type=F32)
    a = (gate * jax.nn.sigmoid(gate) * up).astype(BF16)
    o_ref[0] += gt_ref[...] * jnp.dot(a, wd_ref[...], preferred_element_type=F32)

    if final_norm:
        @pl.when(f == pl.num_programs(2) - 1)
        def _():
            def body(r, carry):
                rows = pl.ds(pl.multiple_of(r * 64, 64), 64)
                xr = o_ref[0, rows, :]
                ms = jnp.mean(xr * xr, axis=-1, keepdims=True)
                o_ref[0, rows, :] = (xr * lax.rsqrt(ms + EPS)) * gf_ref[...]
                return carry

            lax.fori_loop(0, tm // 64, body, 0)


def _swiglu_call(x, mods, g, w_gate, w_up, w_down, g_final, layer, mod_row, final_norm):
    b, l, d = x.shape
    ff = w_gate.shape[-1]
    tm = min(1024, l)
    tf = 512
    kern = functools.partial(_swiglu_kernel, tm=tm, final_norm=final_norm)
    return pl.pallas_call(
        kern,
        grid=(b, l // tm, ff // tf),
        in_specs=[
            pl.BlockSpec((1, tm, d), lambda bi, i, f: (bi, i, 0)),
            pl.BlockSpec((None, 1, d), lambda bi, i, f: (layer, 0, 0)),
            _mod_spec(layer, mod_row, SC2, d),
            _mod_spec(layer, mod_row, SH2, d),
            _mod_spec(layer, mod_row, GT2, d),
            pl.BlockSpec((None, d, tf), lambda bi, i, f: (layer, 0, f)),
            pl.BlockSpec((None, d, tf), lambda bi, i, f: (layer, 0, f)),
            pl.BlockSpec((None, tf, d), lambda bi, i, f: (layer, f, 0)),
            pl.BlockSpec((1, d), lambda bi, i, f: (0, 0)),
        ],
        out_specs=pl.BlockSpec((1, tm, d), lambda bi, i, f: (bi, i, 0)),
        out_shape=jax.ShapeDtypeStruct((b, l, d), F32),
        scratch_shapes=[pltpu.VMEM((tm, d), BF16)],
        compiler_params=_cparams(("parallel", "parallel", "arbitrary")),
        name="swiglu",
    )(x, g, mods, mods, mods, w_gate, w_up, w_down, g_final)


def kernel(x, c, ctx, c_ctx, w_ada, b_ada, g_norm1, w_in, w_conv, sink, w_s, b_s, w_out, g_norm2,
           w_gate, w_up, w_down, g_final):
    batch, n_pos, d = x.shape
    n_ctx = ctx.shape[1]
    depth = w_ada.shape[0]
    cw = (d // 4) // N_GROUPS
    ctx_row = batch

    w_in_b = w_in.astype(BF16)
    w_out_b = w_out.astype(BF16)
    w_gate_b = w_gate.astype(BF16)
    w_up_b = w_up.astype(BF16)
    w_down_b = w_down.astype(BF16)
    g1 = g_norm1.reshape(depth, 1, d)
    g2 = g_norm2.reshape(depth, 1, d)
    gf = g_final.reshape(1, d)

    cs = jnp.concatenate([c, c_ctx[None, :], jnp.zeros((N_MOD_ROWS - batch - 1, d), F32)], axis=0)
    mods = _ada_call(cs, w_ada, b_ada).reshape(depth, N_MOD_ROWS, 1, 6 * d)

    cos_c, sin_c = _dft_tables(cw)
    ccsc = jnp.concatenate([cos_c, sin_c], axis=1).astype(BF16)
    cos_p, sin_p = _dft_tables(n_pos)
    cpsp = jnp.concatenate([cos_p, -sin_p], axis=1).astype(BF16)
    cos_x, sin_x = _dft_tables(n_ctx)
    cpsp_ctx = jnp.concatenate([cos_x, -sin_x], axis=1).astype(BF16)
    rope_cos, rope_sin = _rope_tables(n_pos)
    mlp_bias = jnp.broadcast_to(b_s[:, :, :, None], b_s.shape + (cw,))
    f_block = 1536 // (N_GROUPS * cw)

    def flat(a):
        return a.reshape(1, -1, a.shape[-1])

    def unflat(a):
        return a.reshape(batch, n_ctx, a.shape[-1])

    xc = flat(ctx)
    for layer in range(depth):
        last = layer == depth - 1
        if last:
            w_kv = w_in_b[layer:layer + 1, :, 2560:2816]
            pc = unflat(_norm_in_call(xc, mods, g1, w_kv, layer, 0, ctx_row, tn=256))
            kc_block = 0
        else:
            pc = unflat(_norm_in_call(xc, mods, g1, w_in_b, layer, layer, ctx_row, tn=768))
            kc_block = 2560 // (N_KV_HEADS * HEAD_DIM)
            ys_c = (flat(_conv_call(pc, w_conv, layer)),
                    flat(_fourier_call(pc, ccsc, cpsp_ctx, f_block)),
                    flat(_ctx_attn_call(pc, sink, layer)),
                    flat(_chunk_mlp_call(pc, w_s, mlp_bias, layer)))
            xc_new = _out_proj_call(ys_c, xc, mods, w_out_b, layer, ctx_row)
            xc_next = _swiglu_call(xc_new, mods, g2, w_gate_b, w_up_b, w_down_b, gf, layer, ctx_row, False)
        p = _norm_in_call(x, mods, g1, w_in_b, layer, layer, None, tn=768)
        ys = (_conv_call(p, w_conv, layer),
              _fourier_call(p, ccsc, cpsp, f_block),
              _latent_attn_call(p, pc, kc_block, sink, rope_cos, rope_sin, layer),
              _chunk_mlp_call(p, w_s, mlp_bias, layer))
        x = _out_proj_call(ys, x, mods, w_out_b, layer, None)
        x = _swiglu_call(x, mods, g2, w_gate_b, w_up_b, w_down_b, gf, layer, None, last)
        if not last:
            xc = xc_next
    return x
```

```python
import functools
import math

import jax
import jax.numpy as jnp
from jax import lax
from jax.experimental import pallas as pl
from jax.experimental.pallas import tpu as pltpu

F32 = jnp.float32
BF16 = jnp.bfloat16

EPS = 1e-6
NEG = -1e30
GRID_W = 64
HEAD_DIM = 64
N_HEADS = 8
N_KV_HEADS = 2
HEADS_PER_KV = N_HEADS // N_KV_HEADS
BLOCK = 128
CHUNK = 128
N_GROUPS = 4
ROPE_THETA = 10000.0
SCALE = HEAD_DIM ** -0.5
LOG2E = math.log2(math.e)

LANES = 128
VMEM_LIMIT_BYTES = 60 * 1024 * 1024
N_MOD_ROWS = 24

SH1, SC1, GT1, SH2, SC2, GT2 = range(6)


def _cparams(sem):
    return pltpu.CompilerParams(dimension_semantics=sem, vmem_limit_bytes=VMEM_LIMIT_BYTES)


def _resident(block_shape, index_map):
    return pl.BlockSpec(block_shape, index_map, pipeline_mode=pl.Buffered(1))


def _ada_kernel(c_ref, w_ref, b_ref, o_ref):
    c = c_ref[...]
    s = (c * jax.nn.sigmoid(c)).astype(BF16)
    o_ref[...] = jnp.dot(s, w_ref[...].astype(BF16), preferred_element_type=F32) + b_ref[...]


def _ada_call(cs, w_ada, b_ada):
    depth, d, n = w_ada.shape
    tn = 1024
    return pl.pallas_call(
        _ada_kernel,
        grid=(depth, n // tn),
        in_specs=[
            pl.BlockSpec((N_MOD_ROWS, d), lambda l, j: (0, 0)),
            pl.BlockSpec((None, d, tn), lambda l, j: (l, 0, j)),
            pl.BlockSpec((None, 1, tn), lambda l, j: (l, 0, j)),
        ],
        out_specs=pl.BlockSpec((None, N_MOD_ROWS, tn), lambda l, j: (l, 0, j)),
        out_shape=jax.ShapeDtypeStruct((depth, N_MOD_ROWS, n), F32),
        compiler_params=_cparams(("parallel", "parallel")),
        name="ada_mod",
    )(cs, w_ada, b_ada.reshape(depth, 1, n))


def _mod_spec(layer, row, chunk, d):
    if row is None:
        return pl.BlockSpec((None, None, 1, d), lambda b, *_: (layer, b, 0, chunk))
    return pl.BlockSpec((None, None, 1, d), lambda b, *_: (layer, row, 0, chunk))


def _norm_modulate_rows(x_ref, g_ref, sc_ref, sh_ref, h_ref, tm, rb):
    def body(r, carry):
        rows = pl.ds(pl.multiple_of(r * rb, rb), rb)
        xr = x_ref[0, rows, :]
        ms = jnp.mean(xr * xr, axis=-1, keepdims=True)
        y = xr * lax.rsqrt(ms + EPS)
        h = (y * g_ref[...]) * (1.0 + sc_ref[...]) + sh_ref[...]
        h_ref[rows, :] = h.astype(h_ref.dtype)
        return carry

    lax.fori_loop(0, tm // rb, body, 0)


def _norm_in_kernel(x_ref, xn_ref, g_ref, sc_ref, sh_ref, scn_ref, shn_ref, w_ref, o_ref, ha_ref, hb_ref,
                    *, tm, tn):
    t = pl.program_id(0) * pl.num_programs(1) + pl.program_id(1)
    n = o_ref.shape[-1]

    @pl.when(t == 0)
    def _():
        _norm_modulate_rows(x_ref, g_ref, sc_ref, sh_ref, ha_ref, tm, min(tm, 64))

    def step(h_cur, h_next):
        rb = min(tm, 64)
        for r in range(tm // rb):
            rows = slice(r * rb, (r + 1) * rb)
            xr = xn_ref[0, rows, :]
            ms = jnp.mean(xr * xr, axis=-1, keepdims=True)
            h = ((xr * lax.rsqrt(ms + EPS)) * g_ref[...]) * (1.0 + scn_ref[...]) + shn_ref[...]
            h_next[rows, :] = h.astype(BF16)
        for j in range(n // tn):
            cols = slice(j * tn, (j + 1) * tn)
            o_ref[0, :, cols] = jnp.dot(h_cur[...], w_ref[:, cols],
                                        preferred_element_type=F32).astype(o_ref.dtype)

    parity = lax.rem(t, 2)

    @pl.when(parity == 0)
    def _():
        step(ha_ref, hb_ref)

    @pl.when(parity == 1)
    def _():
        step(hb_ref, ha_ref)


def _norm_in_call(x, mods, g, w, layer, w_layer, mod_row, tn):
    b, l, d = x.shape
    n = w.shape[-1]
    tm = min(512, l)
    nt = l // tm
    kern = functools.partial(_norm_in_kernel, tm=tm, tn=tn)

    def nxt(bi, i):
        t = jnp.minimum(bi * nt + i + 1, b * nt - 1)
        return t // nt, t % nt

    def mod_next(chunk):
        if mod_row is None:
            return pl.BlockSpec((None, None, 1, d), lambda bi, i: (layer, nxt(bi, i)[0], 0, chunk))
        return _mod_spec(layer, mod_row, chunk, d)

    return pl.pallas_call(
        kern,
        grid=(b, nt),
        in_specs=[
            pl.BlockSpec((1, tm, d), lambda bi, i: (bi, i, 0)),
            pl.BlockSpec((1, tm, d), lambda bi, i: nxt(bi, i) + (0,)),
            pl.BlockSpec((None, 1, d), lambda bi, i: (layer, 0, 0)),
            _mod_spec(layer, mod_row, SC1, d),
            _mod_spec(layer, mod_row, SH1, d),
            mod_next(SC1),
            mod_next(SH1),
            _resident((None, d, n), lambda bi, i: (w_layer, 0, 0)),
        ],
        out_specs=pl.BlockSpec((1, tm, n), lambda bi, i: (bi, i, 0)),
        out_shape=jax.ShapeDtypeStruct((b, l, n), BF16),
        scratch_shapes=[pltpu.VMEM((tm, d), BF16), pltpu.VMEM((tm, d), BF16)],
        compiler_params=_cparams(("arbitrary", "arbitrary")),
        name="norm_in",
    )(x, x, g, mods, mods, mods, mods, w)


def _conv_kernel(x_ref, gb_ref, gc_ref, w_ref, o_ref):
    n_pos = x_ref.shape[1]
    z = gc_ref[0].astype(F32) * x_ref[0].astype(F32)
    row = lax.broadcasted_iota(jnp.int32, z.shape, 0)
    z_prev = jnp.where(row == 0, 0.0, pltpu.roll(z, 1, 0))
    z_next = jnp.where(row == n_pos - 1, 0.0, pltpu.roll(z, n_pos - 1, 0))
    y = z_prev * w_ref[0:1, :] + z * w_ref[1:2, :] + z_next * w_ref[2:3, :]
    o_ref[0] = (gb_ref[0].astype(F32) * y).astype(o_ref.dtype)


def _conv_call(p, w_conv, layer):
    b, l, _ = p.shape
    width = w_conv.shape[-1]
    nb = width // LANES
    blk = lambda off: pl.BlockSpec((1, l, LANES), lambda bi, j: (bi, 0, off + j))
    return pl.pallas_call(
        _conv_kernel,
        grid=(b, nb),
        in_specs=[blk(0), blk(nb), blk(2 * nb),
                  pl.BlockSpec((None, 3, LANES), lambda bi, j: (layer, 0, j))],
        out_specs=pl.BlockSpec((1, l, LANES), lambda bi, j: (bi, 0, j)),
        out_shape=jax.ShapeDtypeStruct((b, l, width), BF16),
        compiler_params=_cparams(("parallel", "parallel")),
        name="short_conv",
    )(p, p, p, w_conv)


def _fourier_kernel(z_ref, ccsc_ref, cpsp_ref, o_ref, rhs_ref, *, out_scale):
    n_pos = z_ref.shape[1]
    cw = ccsc_ref.shape[0]
    for g in range(N_GROUPS):
        cols = slice(g * cw, (g + 1) * cw)
        ab = jnp.dot(z_ref[0, :, cols], ccsc_ref[...], preferred_element_type=F32)
        rhs_ref[0:n_pos, cols] = ab[:, :cw].astype(BF16)
        rhs_ref[n_pos:2 * n_pos, cols] = ab[:, cw:].astype(BF16)
    out = jnp.dot(cpsp_ref[...], rhs_ref[...], preferred_element_type=F32)
    o_ref[0] = (out * out_scale).astype(o_ref.dtype)


def _fourier_call(p, ccsc, cpsp, col_block):
    b, l, _ = p.shape
    cw = ccsc.shape[0]
    width = N_GROUPS * cw
    kern = functools.partial(_fourier_kernel, out_scale=1.0 / math.sqrt(l * cw))
    return pl.pallas_call(
        kern,
        grid=(b,),
        in_specs=[
            pl.BlockSpec((1, l, width), lambda bi: (bi, 0, col_block)),
            _resident((cw, 2 * cw), lambda bi: (0, 0)),
            _resident((l, 2 * l), lambda bi: (0, 0)),
        ],
        out_specs=pl.BlockSpec((1, l, width), lambda bi: (bi, 0, 0)),
        out_shape=jax.ShapeDtypeStruct((b, l, width), BF16),
        scratch_shapes=[pltpu.VMEM((2 * l, width), BF16)],
        compiler_params=_cparams(("parallel",)),
        name="fourier_mix",
    )(p, ccsc, cpsp)


def _dft_tables(n):
    k = jnp.arange(n, dtype=jnp.int32)
    ang = ((k[:, None] * k[None, :]) % n).astype(F32) * (2.0 * math.pi / n)
    return jnp.cos(ang), jnp.sin(ang)


def _nt_dot(a, b):
    return lax.dot_general(a, b, (((1,), (1,)), ((), ())), preferred_element_type=F32)


def _latent_attn_kernel(sink_ref, q_ref, k_ref, v_ref, kc_ref, vc_ref, cos_ref, sin_ref, o_ref,
                        qlo_ref, qhi_ref, kpad_ref, kswp_ref, vt_ref, kcs_ref, vct_ref, *, layer):
    n_pos = q_ref.shape[1]
    n_blk = n_pos // BLOCK
    n_ctx = kc_ref.shape[1]
    half = LANES // 2

    lane = lax.broadcasted_iota(jnp.int32, (BLOCK, LANES), 1)
    low16 = (lane % 32) < 16
    lo_half = lane < half

    def prep_block(r, carry):
        rows = pl.ds(pl.multiple_of(r * BLOCK, BLOCK), BLOCK)
        cos = cos_ref[rows, :]
        sin = sin_ref[rows, :]

        def rope(t):
            swapped = jnp.where(low16, pltpu.roll(t, LANES - 16, 1), pltpu.roll(t, 16, 1))
            return t * cos + swapped * sin

        for s in range(q_ref.shape[-1] // LANES):
            cols = slice(s * LANES, (s + 1) * LANES)
            t = rope(q_ref[0, rows, cols].astype(F32)) * (SCALE * LOG2E)
            qlo_ref[rows, cols] = jnp.where(lo_half, t, 0.0).astype(BF16)
            qhi_ref[rows, cols] = jnp.where(lo_half, 0.0, t).astype(BF16)
        prow = pl.ds(pl.multiple_of(r * BLOCK + BLOCK, BLOCK), BLOCK)
        tk = rope(k_ref[0, rows, :].astype(F32))
        kpad_ref[prow, :] = tk.astype(BF16)
        kswp_ref[prow, :] = pltpu.roll(tk, half, 1).astype(BF16)
        vt_ref[r + 1] = v_ref[0, rows, :].astype(F32).T.astype(BF16)
        return carry

    zeros = jnp.zeros((BLOCK, LANES), BF16)
    for ref in (kpad_ref, kswp_ref):
        ref[0:BLOCK, :] = zeros
        ref[n_pos + BLOCK:n_pos + 2 * BLOCK, :] = zeros
    vt_ref[0] = zeros
    vt_ref[n_blk + 1] = zeros
    kcs_ref[...] = pltpu.roll(kc_ref[0].astype(F32), half, 1).astype(BF16)
    for cb in range(n_ctx // BLOCK):
        crow = slice(cb * BLOCK, (cb + 1) * BLOCK)
        vct_ref[:, crow] = vc_ref[0, crow, :].astype(F32).T.astype(BF16)
    lax.fori_loop(0, n_blk, prep_block, 0)

    head_order = (0, 2, 5, 7, 1, 3, 4, 6)
    col_of = {h: i for i, h in enumerate(head_order)}
    n_col = N_HEADS * BLOCK
    kj = lax.broadcasted_iota(jnp.int32, (3 * BLOCK, BLOCK), 0)
    qi = lax.broadcasted_iota(jnp.int32, (3 * BLOCK, BLOCK), 1)
    in_window = (kj >= qi) & (kj <= qi + 2 * BLOCK)
    sk = jnp.concatenate([jnp.full((1, BLOCK), sink_ref[layer, h] * LOG2E, F32) for h in head_order], axis=1)
    neg = NEG * LOG2E

    def slab(ref, rows, s):
        return ref[rows, s * LANES:(s + 1) * LANES]

    def attn_block(n, carry):
        rows = pl.ds(pl.multiple_of(n * BLOCK, BLOCK), BLOCK)
        band = pl.ds(pl.multiple_of(n * BLOCK, BLOCK), 3 * BLOCK)
        valid = in_window & (kj >= (1 - n) * BLOCK) & (kj < n_pos - (n - 1) * BLOCK)

        def masked(s):
            return jnp.concatenate([jnp.where(valid, s[:, i * BLOCK:(i + 1) * BLOCK], neg)
                                    for i in range(s.shape[1] // BLOCK)], axis=1)

        rhs_nat = jnp.concatenate([slab(qlo_ref, rows, 0), slab(qlo_ref, rows, 1),
                                   slab(qhi_ref, rows, 2), slab(qhi_ref, rows, 3)], axis=0)
        rhs_swp = jnp.concatenate([slab(qhi_ref, rows, 0), slab(qhi_ref, rows, 1),
                                   slab(qlo_ref, rows, 2), slab(qlo_ref, rows, 3)], axis=0)
        s_loc = jnp.concatenate([masked(_nt_dot(kpad_ref[band, :], rhs_nat)),
                                 masked(_nt_dot(kswp_ref[band, :], rhs_swp))], axis=1)
        s_ctx = jnp.concatenate([_nt_dot(kc_ref[0], rhs_nat),
                                 _nt_dot(kcs_ref[...], rhs_swp)], axis=1)
        m = jnp.maximum(jnp.maximum(jnp.max(s_loc, axis=0, keepdims=True),
                                    jnp.max(s_ctx, axis=0, keepdims=True)), sk)
        e_loc = jnp.exp2(s_loc - m)
        e_ctx = jnp.exp2(s_ctx - m)
        inv_den = 1.0 / (jnp.sum(e_loc, axis=0, keepdims=True) + jnp.sum(e_ctx, axis=0, keepdims=True)
                         + jnp.exp2(sk - m))
        e_loc = e_loc.astype(BF16)
        e_ctx = e_ctx.astype(BF16)
        vt_band = jnp.concatenate([vt_ref[n], vt_ref[n + 1], vt_ref[n + 2]], axis=1)
        o_t = {}
        for g in range(N_KV_HEADS):
            heads = [4 * g + j for j in range(HEADS_PER_KV)]
            pick = lambda arr: jnp.concatenate(
                [arr[:, col_of[h] * BLOCK:(col_of[h] + 1) * BLOCK] for h in heads], axis=1)
            vrows = slice(g * HEAD_DIM, (g + 1) * HEAD_DIM)
            o = (jnp.dot(vt_band[vrows], pick(e_loc), preferred_element_type=F32)
                 + jnp.dot(vct_ref[vrows, :], pick(e_ctx), preferred_element_type=F32)) * pick(inv_den)
            for j, h in enumerate(heads):
                o_t[h] = o[:, j * BLOCK:(j + 1) * BLOCK]
        slabs = [jnp.concatenate([o_t[2 * s], o_t[2 * s + 1]], axis=0).T for s in range(N_HEADS // 2)]
        o_ref[0, rows, :] = jnp.concatenate(slabs, axis=1).astype(o_ref.dtype)
        return carry

    lax.fori_loop(0, n_blk, attn_block, 0, unroll=2)


def _latent_attn_call(p, pc, kc_block, sink, cos_t, sin_t, layer):
    b, l, _ = p.shape
    lc = pc.shape[1]
    qw = N_HEADS * HEAD_DIM
    kvw = N_KV_HEADS * HEAD_DIM
    q_block = 2048 // qw
    k_block = 2560 // kvw
    kern = functools.partial(_latent_attn_kernel, layer=layer)
    return pl.pallas_call(
        kern,
        grid=(b,),
        in_specs=[
            pl.BlockSpec(memory_space=pltpu.SMEM),
            pl.BlockSpec((1, l, qw), lambda bi: (bi, 0, q_block)),
            pl.BlockSpec((1, l, kvw), lambda bi: (bi, 0, k_block)),
            pl.BlockSpec((1, l, kvw), lambda bi: (bi, 0, k_block + 1)),
            pl.BlockSpec((1, lc, kvw), lambda bi: (bi, 0, kc_block)),
            pl.BlockSpec((1, lc, kvw), lambda bi: (bi, 0, kc_block + 1)),
            _resident((l, LANES), lambda bi: (0, 0)),
            _resident((l, LANES), lambda bi: (0, 0)),
        ],
        out_specs=pl.BlockSpec((1, l, qw), lambda bi: (bi, 0, 0)),
        out_shape=jax.ShapeDtypeStruct((b, l, qw), BF16),
        scratch_shapes=[pltpu.VMEM((l, qw), BF16),
                        pltpu.VMEM((l, qw), BF16),
                        pltpu.VMEM((l + 2 * BLOCK, kvw), BF16),
                        pltpu.VMEM((l + 2 * BLOCK, kvw), BF16),
                        pltpu.VMEM((l // BLOCK + 2, kvw, BLOCK), BF16),
                        pltpu.VMEM((lc, kvw), BF16),
                        pltpu.VMEM((kvw, lc), BF16)],
        compiler_params=_cparams(("parallel",)),
        name="latent_attention",
    )(sink, p, p, p, pc, pc, cos_t, sin_t)


def _ctx_attn_kernel(sink_ref, q_ref, kc_ref, vc_ref, o_ref, *, layer):
    outs = []
    for h in range(N_HEADS):
        kv = h // HEADS_PER_KV
        hc = slice(kv * HEAD_DIM, (kv + 1) * HEAD_DIM)
        qh = q_ref[0, :, h * HEAD_DIM:(h + 1) * HEAD_DIM] * SCALE
        s = _nt_dot(qh, kc_ref[0, :, hc])
        sk = sink_ref[layer, h]
        m = jnp.maximum(jnp.max(s, axis=-1, keepdims=True), sk)
        e = jnp.exp(s - m)
        den = jnp.sum(e, axis=-1, keepdims=True) + jnp.exp(sk - m)
        o = jnp.dot(e.astype(BF16), vc_ref[0, :, hc], preferred_element_type=F32)
        outs.append(o / den)
    o_ref[0] = jnp.concatenate(outs, axis=-1).astype(o_ref.dtype)


def _ctx_attn_call(pc, sink, layer):
    b, lc, _ = pc.shape
    qw = N_HEADS * HEAD_DIM
    kvw = N_KV_HEADS * HEAD_DIM
    kern = functools.partial(_ctx_attn_kernel, layer=layer)
    return pl.pallas_call(
        kern,
        grid=(b,),
        in_specs=[
            pl.BlockSpec(memory_space=pltpu.SMEM),
            pl.BlockSpec((1, lc, qw), lambda bi: (bi, 0, 2048 // qw)),
            pl.BlockSpec((1, lc, kvw), lambda bi: (bi, 0, 2560 // kvw)),
            pl.BlockSpec((1, lc, kvw), lambda bi: (bi, 0, 2560 // kvw + 1)),
        ],
        out_specs=pl.BlockSpec((1, lc, qw), lambda bi: (bi, 0, 0)),
        out_shape=jax.ShapeDtypeStruct((b, lc, qw), BF16),
        compiler_params=_cparams(("parallel",)),
        name="context_attention",
    )(sink, pc, pc, pc)


def _rope_tables(n_pos):
    quarter = HEAD_DIM // 4
    inv = ROPE_THETA ** (-jnp.arange(quarter, dtype=F32) / quarter)
    pos = jnp.arange(n_pos, dtype=jnp.int32)
    row = (pos // GRID_W).astype(F32)
    col = (pos % GRID_W).astype(F32)
    a_row = row[:, None] * inv[None, :]
    a_col = col[:, None] * inv[None, :]
    ang = jnp.concatenate([a_row, a_row, a_col, a_col], axis=-1)
    sign = jnp.tile(jnp.concatenate([-jnp.ones((quarter,), F32), jnp.ones((quarter,), F32)]), 2)
    reps = LANES // HEAD_DIM
    return jnp.tile(jnp.cos(ang), (1, reps)), jnp.tile(jnp.sin(ang) * sign, (1, reps))


def _chunk_mlp_kernel(u_ref, v_ref, w_ref, bias_ref, o_ref):
    n_pos = u_ref.shape[1]
    u = jax.nn.gelu(u_ref[0].astype(F32))
    v = jax.nn.gelu(v_ref[0].astype(F32))
    mu = jnp.mean(v, axis=-1, keepdims=True)
    dlt = v - mu
    var = jnp.mean(dlt * dlt, axis=-1, keepdims=True)
    vn = (dlt * lax.rsqrt(var + EPS)).astype(BF16)
    w = w_ref[...].astype(BF16)
    for n in range(n_pos // CHUNK):
        rows = slice(n * CHUNK, (n + 1) * CHUNK)
        mixed = jnp.dot(w, vn[rows], preferred_element_type=F32) + bias_ref[...]
        o_ref[0, rows, :] = (u[rows] * mixed).astype(o_ref.dtype)


def _chunk_mlp_call(p, w_s, bias, layer):
    b, l, _ = p.shape
    cg = LANES
    u_block = 2816 // cg
    v_block = 3328 // cg
    return pl.pallas_call(
        _chunk_mlp_kernel,
        grid=(b, N_GROUPS),
        in_specs=[
            pl.BlockSpec((1, l, cg), lambda bi, g: (bi, 0, u_block + g)),
            pl.BlockSpec((1, l, cg), lambda bi, g: (bi, 0, v_block + g)),
            pl.BlockSpec((None, None, CHUNK, CHUNK), lambda bi, g: (layer, g, 0, 0)),
            pl.BlockSpec((None, None, CHUNK, cg), lambda bi, g: (layer, g, 0, 0)),
        ],
        out_specs=pl.BlockSpec((1, l, cg), lambda bi, g: (bi, 0, g)),
        out_shape=jax.ShapeDtypeStruct((b, l, N_GROUPS * cg), BF16),
        compiler_params=_cparams(("parallel", "parallel")),
        name="chunk_mlp",
    )(p, p, w_s, bias)


def _out_proj_kernel(ya_ref, yb_ref, yc_ref, yd_ref, x_ref, gt_ref, w_ref, o_ref, *, tn):
    d = o_ref.shape[-1]
    kw = ya_ref.shape[-1]
    ys = (ya_ref, yb_ref, yc_ref, yd_ref)
    for j in range(d // tn):
        cols = slice(j * tn, (j + 1) * tn)
        acc = jnp.dot(ys[0][0], w_ref[0:kw, cols], preferred_element_type=F32)
        for k in range(1, 4):
            acc = acc + jnp.dot(ys[k][0], w_ref[k * kw:(k + 1) * kw, cols], preferred_element_type=F32)
        o_ref[0, :, cols] = x_ref[0, :, cols] + gt_ref[:, cols] * acc


def _out_proj_call(ys, x, mods, w_out, layer, mod_row):
    b, l, d = x.shape
    kw = ys[0].shape[-1]
    tm = min(1024, l)
    kern = functools.partial(_out_proj_kernel, tn=512)
    yspec = pl.BlockSpec((1, tm, kw), lambda bi, i: (bi, i, 0))
    return pl.pallas_call(
        kern,
        grid=(b, l // tm),
        in_specs=[yspec, yspec, yspec, yspec,
                  pl.BlockSpec((1, tm, d), lambda bi, i: (bi, i, 0)),
                  _mod_spec(layer, mod_row, GT1, d),
                  _resident((None, d, d), lambda bi, i: (layer, 0, 0))],
        out_specs=pl.BlockSpec((1, tm, d), lambda bi, i: (bi, i, 0)),
        out_shape=jax.ShapeDtypeStruct((b, l, d), F32),
        compiler_params=_cparams(("parallel", "parallel")),
        name="out_proj",
    )(*ys, x, mods, w_out)


def _swiglu_kernel(x_ref, g_ref, sc_ref, sh_ref, gt_ref, wg_ref, wu_ref, wd_ref, gf_ref, o_ref, h_ref,
                   *, tm, final_norm):
    f = pl.program_id(2)

    def gated_ffn(h):
        gate = jnp.dot(h, wg_ref[...], preferred_element_type=F32)
        up = jnp.dot(h, wu_ref[...], preferred_element_type=F32)
        a = (gate * jax.nn.sigmoid(gate) * up).astype(BF16)
        return gt_ref[...] * jnp.dot(a, wd_ref[...], preferred_element_type=F32)

    @pl.when(f == 0)
    def _():
        rb = min(tm, 256)
        for r in range(tm // rb):
            rows = slice(r * rb, (r + 1) * rb)
            xr = x_ref[0, rows, :]
            ms = jnp.mean(xr * xr, axis=-1, keepdims=True)
            h = (((xr * lax.rsqrt(ms + EPS)) * g_ref[...]) * (1.0 + sc_ref[...]) + sh_ref[...]).astype(BF16)
            h_ref[rows, :] = h
            o_ref[0, rows, :] = xr + gated_ffn(h)

    @pl.when(f != 0)
    def _():
        o_ref[0] += gated_ffn(h_ref[...])

    if final_norm:
        @pl.when(f == pl.num_programs(2) - 1)
        def _():
            def body(r, carry):
                rows = pl.ds(pl.multiple_of(r * 64, 64), 64)
                xr = o_ref[0, rows, :]
                ms = jnp.mean(xr * xr, axis=-1, keepdims=True)
                o_ref[0, rows, :] = (xr * lax.rsqrt(ms + EPS)) * gf_ref[...]
                return carry

            lax.fori_loop(0, tm // 64, body, 0)


def _swiglu_call(x, mods, g, w_gate, w_up, w_down, g_final, layer, mod_row, final_norm):
    b, l, d = x.shape
    ff = w_gate.shape[-1]
    tm = min(1024, l)
    tf = 512
    kern = functools.partial(_swiglu_kernel, tm=tm, final_norm=final_norm)
    return pl.pallas_call(
        kern,
        grid=(b, l // tm, ff // tf),
        in_specs=[
            pl.BlockSpec((1, tm, d), lambda bi, i, f: (bi, i, 0)),
            pl.BlockSpec((None, 1, d), lambda bi, i, f: (layer, 0, 0)),
            _mod_spec(layer, mod_row, SC2, d),
            _mod_spec(layer, mod_row, SH2, d),
            _mod_spec(layer, mod_row, GT2, d),
            pl.BlockSpec((None, d, tf), lambda bi, i, f: (layer, 0, f)),
            pl.BlockSpec((None, d, tf), lambda bi, i, f: (layer, 0, f)),
            pl.BlockSpec((None, tf, d), lambda bi, i, f: (layer, f, 0)),
            pl.BlockSpec((1, d), lambda bi, i, f: (0, 0)),
        ],
        out_specs=pl.BlockSpec((1, tm, d), lambda bi, i, f: (bi, i, 0)),
        out_shape=jax.ShapeDtypeStruct((b, l, d), F32),
        scratch_shapes=[pltpu.VMEM((tm, d), BF16)],
        compiler_params=_cparams(("parallel", "parallel", "arbitrary")),
        name="swiglu",
    )(x, g, mods, mods, mods, w_gate, w_up, w_down, g_final)


def kernel(x, c, ctx, c_ctx, w_ada, b_ada, g_norm1, w_in, w_conv, sink, w_s, b_s, w_out, g_norm2,
           w_gate, w_up, w_down, g_final):
    batch, n_pos, d = x.shape
    n_ctx = ctx.shape[1]
    depth = w_ada.shape[0]
    cw = (d // 4) // N_GROUPS
    ctx_row = batch

    w_in_b = w_in.astype(BF16)
    w_out_b = w_out.astype(BF16)
    w_gate_b = w_gate.astype(BF16)
    w_up_b = w_up.astype(BF16)
    w_down_b = w_down.astype(BF16)
    g1 = g_norm1.reshape(depth, 1, d)
    g2 = g_norm2.reshape(depth, 1, d)
    gf = g_final.reshape(1, d)

    cs = jnp.concatenate([c, c_ctx[None, :], jnp.zeros((N_MOD_ROWS - batch - 1, d), F32)], axis=0)
    mods = _ada_call(cs, w_ada, b_ada).reshape(depth, N_MOD_ROWS, 1, 6 * d)

    cos_c, sin_c = _dft_tables(cw)
    ccsc = jnp.concatenate([cos_c, sin_c], axis=1).astype(BF16)
    cos_p, sin_p = _dft_tables(n_pos)
    cpsp = jnp.concatenate([cos_p, -sin_p], axis=1).astype(BF16)
    cos_x, sin_x = _dft_tables(n_ctx)
    cpsp_ctx = jnp.concatenate([cos_x, -sin_x], axis=1).astype(BF16)
    rope_cos, rope_sin = _rope_tables(n_pos)
    mlp_bias = jnp.broadcast_to(b_s[:, :, :, None], b_s.shape + (cw,))
    f_block = 1536 // (N_GROUPS * cw)

    def flat(a):
        return a.reshape(1, -1, a.shape[-1])

    def unflat(a):
        return a.reshape(batch, n_ctx, a.shape[-1])

    xc = flat(ctx)
    for layer in range(depth):
        last = layer == depth - 1
        if last:
            w_kv = w_in_b[layer:layer + 1, :, 2560:2816]
            pc = unflat(_norm_in_call(xc, mods, g1, w_kv, layer, 0, ctx_row, tn=256))
            kc_block = 0
        else:
            pc = unflat(_norm_in_call(xc, mods, g1, w_in_b, layer, layer, ctx_row, tn=768))
            kc_block = 2560 // (N_KV_HEADS * HEAD_DIM)
            ys_c = (flat(_conv_call(pc, w_conv, layer)),
                    flat(_fourier_call(pc, ccsc, cpsp_ctx, f_block)),
                    flat(_ctx_attn_call(pc, sink, layer)),
                    flat(_chunk_mlp_call(pc, w_s, mlp_bias, layer)))
            xc_new = _out_proj_call(ys_c, xc, mods, w_out_b, layer, ctx_row)
            xc_next = _swiglu_call(xc_new, mods, g2, w_gate_b, w_up_b, w_down_b, gf, layer, ctx_row, False)
        p = _norm_in_call(x, mods, g1, w_in_b, layer, layer, None, tn=768)
        ys = (_conv_call(p, w_conv, layer),
              _fourier_call(p, ccsc, cpsp, f_block),
              _latent_attn_call(p, pc, kc_block, sink, rope_cos, rope_sin, layer),
              _chunk_mlp_call(p, w_s, mlp_bias, layer))
        x = _out_proj_call(ys, x, mods, w_out_b, layer, None)
        x = _swiglu_call(x, mods, g2, w_gate_b, w_up_b, w_down_b, gf, layer, None, last)
        if not last:
            xc = xc_next
    return x
```

```python
import functools
import math

import jax
import jax.numpy as jnp
from jax import lax
from jax.experimental import pallas as pl
from jax.experimental.pallas import tpu as pltpu

F32 = jnp.float32
BF16 = jnp.bfloat16

EPS = 1e-6
NEG = -1e30
GRID_W = 64
HEAD_DIM = 64
N_HEADS = 8
N_KV_HEADS = 2
HEADS_PER_KV = N_HEADS // N_KV_HEADS
BLOCK = 128
CHUNK = 128
N_GROUPS = 4
ROPE_THETA = 10000.0
SCALE = HEAD_DIM ** -0.5
LOG2E = math.log2(math.e)

LANES = 128
VMEM_LIMIT_BYTES = 60 * 1024 * 1024
N_MOD_ROWS = 24

SH1, SC1, GT1, SH2, SC2, GT2 = range(6)


def _cparams(sem):
    return pltpu.CompilerParams(dimension_semantics=sem, vmem_limit_bytes=VMEM_LIMIT_BYTES)


def _resident(block_shape, index_map):
    return pl.BlockSpec(block_shape, index_map, pipeline_mode=pl.Buffered(1))


def _ada_kernel(c_ref, w_ref, b_ref, o_ref):
    c = c_ref[...]
    s = (c * jax.nn.sigmoid(c)).astype(BF16)
    o_ref[...] = jnp.dot(s, w_ref[...].astype(BF16), preferred_element_type=F32) + b_ref[...]


def _ada_call(cs, w_ada, b_ada):
    depth, d, n = w_ada.shape
    tn = 1024
    return pl.pallas_call(
        _ada_kernel,
        grid=(depth, n // tn),
        in_specs=[
            pl.BlockSpec((N_MOD_ROWS, d), lambda l, j: (0, 0)),
            pl.BlockSpec((None, d, tn), lambda l, j: (l, 0, j)),
            pl.BlockSpec((None, 1, tn), lambda l, j: (l, 0, j)),
        ],
        out_specs=pl.BlockSpec((None, N_MOD_ROWS, tn), lambda l, j: (l, 0, j)),
        out_shape=jax.ShapeDtypeStruct((depth, N_MOD_ROWS, n), F32),
        compiler_params=_cparams(("parallel", "parallel")),
        name="ada_mod",
    )(cs, w_ada, b_ada.reshape(depth, 1, n))


def _mod_spec(layer, row, chunk, d):
    if row is None:
        return pl.BlockSpec((None, None, 1, d), lambda b, *_: (layer, b, 0, chunk))
    return pl.BlockSpec((None, None, 1, d), lambda b, *_: (layer, row, 0, chunk))


def _norm_modulate_rows(x_ref, g_ref, sc_ref, sh_ref, h_ref, tm, rb):
    def body(r, carry):
        rows = pl.ds(pl.multiple_of(r * rb, rb), rb)
        xr = x_ref[0, rows, :]
        ms = jnp.mean(xr * xr, axis=-1, keepdims=True)
        y = xr * lax.rsqrt(ms + EPS)
        h = (y * g_ref[...]) * (1.0 + sc_ref[...]) + sh_ref[...]
        h_ref[rows, :] = h.astype(h_ref.dtype)
        return carry

    lax.fori_loop(0, tm // rb, body, 0)


def _norm_in_kernel(x_ref, xn_ref, g_ref, sc_ref, sh_ref, scn_ref, shn_ref, w_ref, o_ref, ha_ref, hb_ref,
                    *, tm, tn):
    t = pl.program_id(0) * pl.num_programs(1) + pl.program_id(1)
    n = o_ref.shape[-1]

    @pl.when(t == 0)
    def _():
        _norm_modulate_rows(x_ref, g_ref, sc_ref, sh_ref, ha_ref, tm, min(tm, 64))

    def step(h_cur, h_next):
        rb = min(tm, 64)
        for r in range(tm // rb):
            rows = slice(r * rb, (r + 1) * rb)
            xr = xn_ref[0, rows, :]
            ms = jnp.mean(xr * xr, axis=-1, keepdims=True)
            h = ((xr * lax.rsqrt(ms + EPS)) * g_ref[...]) * (1.0 + scn_ref[...]) + shn_ref[...]
            h_next[rows, :] = h.astype(BF16)
        for j in range(n // tn):
            cols = slice(j * tn, (j + 1) * tn)
            o_ref[0, :, cols] = jnp.dot(h_cur[...], w_ref[:, cols],
                                        preferred_element_type=F32).astype(o_ref.dtype)

    parity = lax.rem(t, 2)

    @pl.when(parity == 0)
    def _():
        step(ha_ref, hb_ref)

    @pl.when(parity == 1)
    def _():
        step(hb_ref, ha_ref)


def _norm_in_call(x, mods, g, w, layer, w_layer, mod_row, tn):
    b, l, d = x.shape
    n = w.shape[-1]
    tm = min(512, l)
    nt = l // tm
    kern = functools.partial(_norm_in_kernel, tm=tm, tn=tn)

    def nxt(bi, i):
        t = jnp.minimum(bi * nt + i + 1, b * nt - 1)
        return t // nt, t % nt

    def mod_next(chunk):
        if mod_row is None:
            return pl.BlockSpec((None, None, 1, d), lambda bi, i: (layer, nxt(bi, i)[0], 0, chunk))
        return _mod_spec(layer, mod_row, chunk, d)

    return pl.pallas_call(
        kern,
        grid=(b, nt),
        in_specs=[
            pl.BlockSpec((1, tm, d), lambda bi, i: (bi, i, 0)),
            pl.BlockSpec((1, tm, d), lambda bi, i: nxt(bi, i) + (0,)),
            pl.BlockSpec((None, 1, d), lambda bi, i: (layer, 0, 0)),
            _mod_spec(layer, mod_row, SC1, d),
            _mod_spec(layer, mod_row, SH1, d),
            mod_next(SC1),
            mod_next(SH1),
            _resident((None, d, n), lambda bi, i: (w_layer, 0, 0)),
        ],
        out_specs=pl.BlockSpec((1, tm, n), lambda bi, i: (bi, i, 0)),
        out_shape=jax.ShapeDtypeStruct((b, l, n), BF16),
        scratch_shapes=[pltpu.VMEM((tm, d), BF16), pltpu.VMEM((tm, d), BF16)],
        compiler_params=_cparams(("arbitrary", "arbitrary")),
        name="norm_in",
    )(x, x, g, mods, mods, mods, mods, w)


def _conv_kernel(x_ref, gb_ref, gc_ref, w_ref, o_ref, *, seq_len):
    n_rows = x_ref.shape[1]
    pos = lax.broadcasted_iota(jnp.int32, (n_rows, LANES), 0) % seq_len
    first = pos == 0
    last = pos == seq_len - 1
    for s in range(x_ref.shape[2] // LANES):
        cols = slice(s * LANES, (s + 1) * LANES)
        z = gc_ref[0, :, cols].astype(F32) * x_ref[0, :, cols].astype(F32)
        z_prev = jnp.where(first, 0.0, pltpu.roll(z, 1, 0))
        z_next = jnp.where(last, 0.0, pltpu.roll(z, n_rows - 1, 0))
        y = z_prev * w_ref[0:1, cols] + z * w_ref[1:2, cols] + z_next * w_ref[2:3, cols]
        o_ref[0, :, cols] = (gb_ref[0, :, cols].astype(F32) * y).astype(o_ref.dtype)


def _conv_call(p, w_conv, layer, seq_len):
    b, l, _ = p.shape
    width = w_conv.shape[-1]
    blk = lambda j: pl.BlockSpec((1, l, width), lambda bi: (bi, 0, j))
    return pl.pallas_call(
        functools.partial(_conv_kernel, seq_len=seq_len),
        grid=(b,),
        in_specs=[blk(0), blk(1), blk(2),
                  pl.BlockSpec((None, 3, width), lambda bi: (layer, 0, 0))],
        out_specs=pl.BlockSpec((1, l, width), lambda bi: (bi, 0, 0)),
        out_shape=jax.ShapeDtypeStruct((b, l, width), BF16),
        compiler_params=_cparams(("parallel",)),
        name="short_conv",
    )(p, p, p, w_conv)


def _fourier_kernel(z_ref, ccsc_ref, cpsp_ref, o_ref, rhs_ref, *, out_scale):
    n_pos = z_ref.shape[1]
    cw = ccsc_ref.shape[0]
    for g in range(N_GROUPS):
        cols = slice(g * cw, (g + 1) * cw)
        ab = jnp.dot(z_ref[0, :, cols], ccsc_ref[...], preferred_element_type=F32)
        rhs_ref[0:n_pos, cols] = ab[:, :cw].astype(BF16)
        rhs_ref[n_pos:2 * n_pos, cols] = ab[:, cw:].astype(BF16)
    out = jnp.dot(cpsp_ref[...], rhs_ref[...], preferred_element_type=F32)
    o_ref[0] = (out * out_scale).astype(o_ref.dtype)


def _fourier_call(p, ccsc, cpsp, col_block):
    b, l, _ = p.shape
    cw = ccsc.shape[0]
    width = N_GROUPS * cw
    kern = functools.partial(_fourier_kernel, out_scale=1.0 / math.sqrt(l * cw))
    return pl.pallas_call(
        kern,
        grid=(b,),
        in_specs=[
            pl.BlockSpec((1, l, width), lambda bi: (bi, 0, col_block)),
            _resident((cw, 2 * cw), lambda bi: (0, 0)),
            _resident((l, 2 * l), lambda bi: (0, 0)),
        ],
        out_specs=pl.BlockSpec((1, l, width), lambda bi: (bi, 0, 0)),
        out_shape=jax.ShapeDtypeStruct((b, l, width), BF16),
        scratch_shapes=[pltpu.VMEM((2 * l, width), BF16)],
        compiler_params=_cparams(("parallel",)),
        name="fourier_mix",
    )(p, ccsc, cpsp)


def _dft_tables(n):
    k = jnp.arange(n, dtype=jnp.int32)
    ang = ((k[:, None] * k[None, :]) % n).astype(F32) * (2.0 * math.pi / n)
    return jnp.cos(ang), jnp.sin(ang)


def _nt_dot(a, b):
    return lax.dot_general(a, b, (((1,), (1,)), ((), ())), preferred_element_type=F32)


def _latent_attn_kernel(sink_ref, q_ref, k_ref, v_ref, kc_ref, vc_ref, cos_ref, sin_ref, o_ref,
                        qlo_ref, qhi_ref, kpad_ref, kswp_ref, vt_ref, kcs_ref, vct_ref, *, layer):
    n_pos = q_ref.shape[1]
    n_blk = n_pos // BLOCK
    n_ctx = kc_ref.shape[1]
    half = LANES // 2

    lane = lax.broadcasted_iota(jnp.int32, (BLOCK, LANES), 1)
    low16 = (lane % 32) < 16
    lo_half = lane < half

    def prep_block(r, carry):
        rows = pl.ds(pl.multiple_of(r * BLOCK, BLOCK), BLOCK)
        cos = cos_ref[rows, :]
        sin = sin_ref[rows, :]

        def rope(t):
            swapped = jnp.where(low16, pltpu.roll(t, LANES - 16, 1), pltpu.roll(t, 16, 1))
            return t * cos + swapped * sin

        for s in range(q_ref.shape[-1] // LANES):
            cols = slice(s * LANES, (s + 1) * LANES)
            t = rope(q_ref[0, rows, cols].astype(F32)) * (SCALE * LOG2E)
            qlo_ref[rows, cols] = jnp.where(lo_half, t, 0.0).astype(BF16)
            qhi_ref[rows, cols] = jnp.where(lo_half, 0.0, t).astype(BF16)
        prow = pl.ds(pl.multiple_of(r * BLOCK + BLOCK, BLOCK), BLOCK)
        tk = rope(k_ref[0, rows, :].astype(F32))
        kpad_ref[prow, :] = tk.astype(BF16)
        kswp_ref[prow, :] = pltpu.roll(tk, half, 1).astype(BF16)
        vt_ref[r + 1] = v_ref[0, rows, :].astype(F32).T.astype(BF16)
        return carry

    zeros = jnp.zeros((BLOCK, LANES), BF16)
    for ref in (kpad_ref, kswp_ref):
        ref[0:BLOCK, :] = zeros
        ref[n_pos + BLOCK:n_pos + 2 * BLOCK, :] = zeros
    vt_ref[0] = zeros
    vt_ref[n_blk + 1] = zeros
    kcs_ref[...] = pltpu.roll(kc_ref[0].astype(F32), half, 1).astype(BF16)
    for cb in range(n_ctx // BLOCK):
        crow = slice(cb * BLOCK, (cb + 1) * BLOCK)
        vct_ref[:, crow] = vc_ref[0, crow, :].astype(F32).T.astype(BF16)
    lax.fori_loop(0, n_blk, prep_block, 0)

    head_order = (0, 2, 5, 7, 1, 3, 4, 6)
    col_of = {h: i for i, h in enumerate(head_order)}
    n_col = N_HEADS * BLOCK
    kj = lax.broadcasted_iota(jnp.int32, (3 * BLOCK, BLOCK), 0)
    qi = lax.broadcasted_iota(jnp.int32, (3 * BLOCK, BLOCK), 1)
    in_window = (kj >= qi) & (kj <= qi + 2 * BLOCK)
    sk = jnp.concatenate([jnp.full((1, BLOCK), sink_ref[layer, h] * LOG2E, F32) for h in head_order], axis=1)
    neg = NEG * LOG2E

    def slab(ref, rows, s):
        return ref[rows, s * LANES:(s + 1) * LANES]

    def attn_block(n, carry):
        rows = pl.ds(pl.multiple_of(n * BLOCK, BLOCK), BLOCK)
        band = pl.ds(pl.multiple_of(n * BLOCK, BLOCK), 3 * BLOCK)
        valid = in_window & (kj >= (1 - n) * BLOCK) & (kj < n_pos - (n - 1) * BLOCK)

        def masked(s):
            return jnp.concatenate([jnp.where(valid, s[:, i * BLOCK:(i + 1) * BLOCK], neg)
                                    for i in range(s.shape[1] // BLOCK)], axis=1)

        rhs_nat = jnp.concatenate([slab(qlo_ref, rows, 0), slab(qlo_ref, rows, 1),
                                   slab(qhi_ref, rows, 2), slab(qhi_ref, rows, 3)], axis=0)
        rhs_swp = jnp.concatenate([slab(qhi_ref, rows, 0), slab(qhi_ref, rows, 1),
                                   slab(qlo_ref, rows, 2), slab(qlo_ref, rows, 3)], axis=0)
        s_loc = jnp.concatenate([masked(_nt_dot(kpad_ref[band, :], rhs_nat)),
                                 masked(_nt_dot(kswp_ref[band, :], rhs_swp))], axis=1)
        s_ctx = jnp.concatenate([_nt_dot(kc_ref[0], rhs_nat),
                                 _nt_dot(kcs_ref[...], rhs_swp)], axis=1)
        m = jnp.maximum(jnp.maximum(jnp.max(s_loc, axis=0, keepdims=True),
                                    jnp.max(s_ctx, axis=0, keepdims=True)), sk)
        e_loc = jnp.exp2(s_loc - m)
        e_ctx = jnp.exp2(s_ctx - m)
        inv_den = 1.0 / (jnp.sum(e_loc, axis=0, keepdims=True) + jnp.sum(e_ctx, axis=0, keepdims=True)
                         + jnp.exp2(sk - m))
        e_loc = e_loc.astype(BF16)
        e_ctx = e_ctx.astype(BF16)
        vt_band = jnp.concatenate([vt_ref[n], vt_ref[n + 1], vt_ref[n + 2]], axis=1)
        o_t = {}
        for g in range(N_KV_HEADS):
            heads = [4 * g + j for j in range(HEADS_PER_KV)]
            pick = lambda arr: jnp.concatenate(
                [arr[:, col_of[h] * BLOCK:(col_of[h] + 1) * BLOCK] for h in heads], axis=1)
            vrows = slice(g * HEAD_DIM, (g + 1) * HEAD_DIM)
            o = (jnp.dot(vt_band[vrows], pick(e_loc), preferred_element_type=F32)
                 + jnp.dot(vct_ref[vrows, :], pick(e_ctx), preferred_element_type=F32)) * pick(inv_den)
            for j, h in enumerate(heads):
                o_t[h] = o[:, j * BLOCK:(j + 1) * BLOCK]
        slabs = [jnp.concatenate([o_t[2 * s], o_t[2 * s + 1]], axis=0).T for s in range(N_HEADS // 2)]
        o_ref[0, rows, :] = jnp.concatenate(slabs, axis=1).astype(o_ref.dtype)
        return carry

    lax.fori_loop(0, n_blk, attn_block, 0, unroll=2)


def _latent_attn_call(p, pc, kc_block, sink, cos_t, sin_t, layer):
    b, l, _ = p.shape
    lc = pc.shape[1]
    qw = N_HEADS * HEAD_DIM
    kvw = N_KV_HEADS * HEAD_DIM
    q_block = 2048 // qw
    k_block = 2560 // kvw
    kern = functools.partial(_latent_attn_kernel, layer=layer)
    return pl.pallas_call(
        kern,
        grid=(b,),
        in_specs=[
            pl.BlockSpec(memory_space=pltpu.SMEM),
            pl.BlockSpec((1, l, qw), lambda bi: (bi, 0, q_block)),
            pl.BlockSpec((1, l, kvw), lambda bi: (bi, 0, k_block)),
            pl.BlockSpec((1, l, kvw), lambda bi: (bi, 0, k_block + 1)),
            pl.BlockSpec((1, lc, kvw), lambda bi: (bi, 0, kc_block)),
            pl.BlockSpec((1, lc, kvw), lambda bi: (bi, 0, kc_block + 1)),
            _resident((l, LANES), lambda bi: (0, 0)),
            _resident((l, LANES), lambda bi: (0, 0)),
        ],
        out_specs=pl.BlockSpec((1, l, qw), lambda bi: (bi, 0, 0)),
        out_shape=jax.ShapeDtypeStruct((b, l, qw), BF16),
        scratch_shapes=[pltpu.VMEM((l, qw), BF16),
                        pltpu.VMEM((l, qw), BF16),
                        pltpu.VMEM((l + 2 * BLOCK, kvw), BF16),
                        pltpu.VMEM((l + 2 * BLOCK, kvw), BF16),
                        pltpu.VMEM((l // BLOCK + 2, kvw, BLOCK), BF16),
                        pltpu.VMEM((lc, kvw), BF16),
                        pltpu.VMEM((kvw, lc), BF16)],
        compiler_params=_cparams(("parallel",)),
        name="latent_attention",
    )(sink, p, p, p, pc, pc, cos_t, sin_t)


def _ctx_attn_kernel(sink_ref, q_ref, kc_ref, vc_ref, o_ref, *, layer):
    outs = []
    for h in range(N_HEADS):
        kv = h // HEADS_PER_KV
        hc = slice(kv * HEAD_DIM, (kv + 1) * HEAD_DIM)
        qh = q_ref[0, :, h * HEAD_DIM:(h + 1) * HEAD_DIM] * SCALE
        s = _nt_dot(qh, kc_ref[0, :, hc])
        sk = sink_ref[layer, h]
        m = jnp.maximum(jnp.max(s, axis=-1, keepdims=True), sk)
        e = jnp.exp(s - m)
        den = jnp.sum(e, axis=-1, keepdims=True) + jnp.exp(sk - m)
        o = jnp.dot(e.astype(BF16), vc_ref[0, :, hc], preferred_element_type=F32)
        outs.append(o / den)
    o_ref[0] = jnp.concatenate(outs, axis=-1).astype(o_ref.dtype)


def _ctx_attn_call(pc, sink, layer):
    b, lc, _ = pc.shape
    qw = N_HEADS * HEAD_DIM
    kvw = N_KV_HEADS * HEAD_DIM
    kern = functools.partial(_ctx_attn_kernel, layer=layer)
    return pl.pallas_call(
        kern,
        grid=(b,),
        in_specs=[
            pl.BlockSpec(memory_space=pltpu.SMEM),
            pl.BlockSpec((1, lc, qw), lambda bi: (bi, 0, 2048 // qw)),
            pl.BlockSpec((1, lc, kvw), lambda bi: (bi, 0, 2560 // kvw)),
            pl.BlockSpec((1, lc, kvw), lambda bi: (bi, 0, 2560 // kvw + 1)),
        ],
        out_specs=pl.BlockSpec((1, lc, qw), lambda bi: (bi, 0, 0)),
        out_shape=jax.ShapeDtypeStruct((b, lc, qw), BF16),
        compiler_params=_cparams(("parallel",)),
        name="context_attention",
    )(sink, pc, pc, pc)


def _rope_tables(n_pos):
    quarter = HEAD_DIM // 4
    inv = ROPE_THETA ** (-jnp.arange(quarter, dtype=F32) / quarter)
    pos = jnp.arange(n_pos, dtype=jnp.int32)
    row = (pos // GRID_W).astype(F32)
    col = (pos % GRID_W).astype(F32)
    a_row = row[:, None] * inv[None, :]
    a_col = col[:, None] * inv[None, :]
    ang = jnp.concatenate([a_row, a_row, a_col, a_col], axis=-1)
    sign = jnp.tile(jnp.concatenate([-jnp.ones((quarter,), F32), jnp.ones((quarter,), F32)]), 2)
    reps = LANES // HEAD_DIM
    return jnp.tile(jnp.cos(ang), (1, reps)), jnp.tile(jnp.sin(ang) * sign, (1, reps))


def _gelu_tanh(x):
    k1 = -2.0 * math.sqrt(2.0 / math.pi) * LOG2E
    t = x * (k1 + (k1 * 0.044715) * (x * x))
    return x * (1.0 / (1.0 + jnp.exp2(t)))


def _chunk_mlp_kernel(u_ref, v_ref, w_ref, bias_ref, o_ref):
    n_rows = u_ref.shape[1]
    averager = jnp.full((LANES, LANES), 1.0 / LANES, BF16)
    for gi in range(u_ref.shape[2] // LANES):
        cols = slice(gi * LANES, (gi + 1) * LANES)
        u = _gelu_tanh(u_ref[0, :, cols].astype(F32))
        v = _gelu_tanh(v_ref[0, :, cols].astype(F32))
        mu = jnp.dot(v.astype(BF16), averager, preferred_element_type=F32)
        dlt = v - mu
        var = jnp.dot((dlt * dlt).astype(BF16), averager, preferred_element_type=F32)
        vn = (dlt * lax.rsqrt(var + EPS)).astype(BF16)
        w = w_ref[gi].astype(BF16)
        for n in range(n_rows // CHUNK):
            rows = slice(n * CHUNK, (n + 1) * CHUNK)
            mixed = jnp.dot(w, vn[rows], preferred_element_type=F32) + bias_ref[gi]
            o_ref[0, rows, cols] = (u[rows] * mixed).astype(o_ref.dtype)


def _chunk_mlp_call(p, w_s, bias, layer):
    b, l, _ = p.shape
    gpb = 2
    bw = gpb * LANES
    u_block = 2816 // bw
    v_block = 3328 // bw
    return pl.pallas_call(
        _chunk_mlp_kernel,
        grid=(b, N_GROUPS // gpb),
        in_specs=[
            pl.BlockSpec((1, l, bw), lambda bi, j: (bi, 0, u_block + j)),
            pl.BlockSpec((1, l, bw), lambda bi, j: (bi, 0, v_block + j)),
            pl.BlockSpec((None, gpb, CHUNK, CHUNK), lambda bi, j: (layer, j, 0, 0)),
            pl.BlockSpec((None, gpb, CHUNK, LANES), lambda bi, j: (layer, j, 0, 0)),
        ],
        out_specs=pl.BlockSpec((1, l, bw), lambda bi, j: (bi, 0, j)),
        out_shape=jax.ShapeDtypeStruct((b, l, N_GROUPS * LANES), BF16),
        compiler_params=_cparams(("parallel", "parallel")),
        name="chunk_mlp",
    )(p, p, w_s, bias)


def _out_proj_kernel(ya_ref, yb_ref, yc_ref, yd_ref, x_ref, gt_ref, w_ref, o_ref, *, tn):
    d = o_ref.shape[-1]
    kw = ya_ref.shape[-1]
    ys = (ya_ref, yb_ref, yc_ref, yd_ref)
    for j in range(d // tn):
        cols = slice(j * tn, (j + 1) * tn)
        acc = jnp.dot(ys[0][0], w_ref[0:kw, cols], preferred_element_type=F32)
        for k in range(1, 4):
            acc = acc + jnp.dot(ys[k][0], w_ref[k * kw:(k + 1) * kw, cols], preferred_element_type=F32)
        o_ref[0, :, cols] = x_ref[0, :, cols] + gt_ref[:, cols] * acc


def _out_proj_call(ys, x, mods, w_out, layer, mod_row):
    b, l, d = x.shape
    kw = ys[0].shape[-1]
    tm = min(1024, l)
    kern = functools.partial(_out_proj_kernel, tn=512)
    yspec = pl.BlockSpec((1, tm, kw), lambda bi, i: (bi, i, 0))
    return pl.pallas_call(
        kern,
        grid=(b, l // tm),
        in_specs=[yspec, yspec, yspec, yspec,
                  pl.BlockSpec((1, tm, d), lambda bi, i: (bi, i, 0)),
                  _mod_spec(layer, mod_row, GT1, d),
                  _resident((None, d, d), lambda bi, i: (layer, 0, 0))],
        out_specs=pl.BlockSpec((1, tm, d), lambda bi, i: (bi, i, 0)),
        out_shape=jax.ShapeDtypeStruct((b, l, d), F32),
        compiler_params=_cparams(("parallel", "parallel")),
        name="out_proj",
    )(*ys, x, mods, w_out)


def _swiglu_kernel(x_ref, g_ref, sc_ref, sh_ref, gt_ref, wg_ref, wu_ref, wd_ref, gf_ref, o_ref, h_ref,
                   *, tm, final_norm):
    f = pl.program_id(2)

    def gated_ffn(h):
        gate = jnp.dot(h, wg_ref[...], preferred_element_type=F32)
        up = jnp.dot(h, wu_ref[...], preferred_element_type=F32)
        a = (gate * jax.nn.sigmoid(gate) * up).astype(BF16)
        return gt_ref[...] * jnp.dot(a, wd_ref[...], preferred_element_type=F32)

    @pl.when(f == 0)
    def _():
        rb = min(tm, 256)
        for r in range(tm // rb):
            rows = slice(r * rb, (r + 1) * rb)
            xr = x_ref[0, rows, :]
            ms = jnp.mean(xr * xr, axis=-1, keepdims=True)
            h = (((xr * lax.rsqrt(ms + EPS)) * g_ref[...]) * (1.0 + sc_ref[...]) + sh_ref[...]).astype(BF16)
            h_ref[rows, :] = h
            o_ref[0, rows, :] = xr + gated_ffn(h)

    @pl.when(f != 0)
    def _():
        o_ref[0] += gated_ffn(h_ref[...])

    if final_norm:
        @pl.when(f == pl.num_programs(2) - 1)
        def _():
            def body(r, carry):
                rows = pl.ds(pl.multiple_of(r * 64, 64), 64)
                xr = o_ref[0, rows, :]
                ms = jnp.mean(xr * xr, axis=-1, keepdims=True)
                o_ref[0, rows, :] = (xr * lax.rsqrt(ms + EPS)) * gf_ref[...]
                return carry

            lax.fori_loop(0, tm // 64, body, 0)


def _swiglu_call(x, mods, g, w_gate, w_up, w_down, g_final, layer, mod_row, final_norm):
    b, l, d = x.shape
    ff = w_gate.shape[-1]
    tm = min(1024, l)
    tf = 512
    kern = functools.partial(_swiglu_kernel, tm=tm, final_norm=final_norm)
    return pl.pallas_call(
        kern,
        grid=(b, l // tm, ff // tf),
        in_specs=[
            pl.BlockSpec((1, tm, d), lambda bi, i, f: (bi, i, 0)),
            pl.BlockSpec((None, 1, d), lambda bi, i, f: (layer, 0, 0)),
            _mod_spec(layer, mod_row, SC2, d),
            _mod_spec(layer, mod_row, SH2, d),
            _mod_spec(layer, mod_row, GT2, d),
            pl.BlockSpec((None, d, tf), lambda bi, i, f: (layer, 0, f)),
            pl.BlockSpec((None, d, tf), lambda bi, i, f: (layer, 0, f)),
            pl.BlockSpec((None, tf, d), lambda bi, i, f: (layer, f, 0)),
            pl.BlockSpec((1, d), lambda bi, i, f: (0, 0)),
        ],
        out_specs=pl.BlockSpec((1, tm, d), lambda bi, i, f: (bi, i, 0)),
        out_shape=jax.ShapeDtypeStruct((b, l, d), F32),
        scratch_shapes=[pltpu.VMEM((tm, d), BF16)],
        compiler_params=_cparams(("parallel", "parallel", "arbitrary")),
        name="swiglu",
    )(x, g, mods, mods, mods, w_gate, w_up, w_down, g_final)


def kernel(x, c, ctx, c_ctx, w_ada, b_ada, g_norm1, w_in, w_conv, sink, w_s, b_s, w_out, g_norm2,
           w_gate, w_up, w_down, g_final):
    batch, n_pos, d = x.shape
    n_ctx = ctx.shape[1]
    depth = w_ada.shape[0]
    cw = (d // 4) // N_GROUPS
    ctx_row = batch

    w_in_b = w_in.astype(BF16)
    w_out_b = w_out.astype(BF16)
    w_gate_b = w_gate.astype(BF16)
    w_up_b = w_up.astype(BF16)
    w_down_b = w_down.astype(BF16)
    g1 = g_norm1.reshape(depth, 1, d)
    g2 = g_norm2.reshape(depth, 1, d)
    gf = g_final.reshape(1, d)

    cs = jnp.concatenate([c, c_ctx[None, :], jnp.zeros((N_MOD_ROWS - batch - 1, d), F32)], axis=0)
    mods = _ada_call(cs, w_ada, b_ada).reshape(depth, N_MOD_ROWS, 1, 6 * d)

    cos_c, sin_c = _dft_tables(cw)
    ccsc = jnp.concatenate([cos_c, sin_c], axis=1).astype(BF16)
    cos_p, sin_p = _dft_tables(n_pos)
    cpsp = jnp.concatenate([cos_p, -sin_p], axis=1).astype(BF16)
    cos_x, sin_x = _dft_tables(n_ctx)
    cpsp_ctx = jnp.concatenate([cos_x, -sin_x], axis=1).astype(BF16)
    rope_cos, rope_sin = _rope_tables(n_pos)
    mlp_bias = jnp.broadcast_to(b_s[:, :, :, None], b_s.shape + (cw,))
    f_block = 1536 // (N_GROUPS * cw)

    def flat(a):
        return a.reshape(1, -1, a.shape[-1])

    def unflat(a):
        return a.reshape(batch, n_ctx, a.shape[-1])

    xc = flat(ctx)
    for layer in range(depth):
        last = layer == depth - 1
        if last:
            w_kv = w_in_b[layer:layer + 1, :, 2560:2816]
            pc = unflat(_norm_in_call(xc, mods, g1, w_kv, layer, 0, ctx_row, tn=256))
            kc_block = 0
        else:
            pc_flat = _norm_in_call(xc, mods, g1, w_in_b, layer, layer, ctx_row, tn=768)
            pc = unflat(pc_flat)
            kc_block = 2560 // (N_KV_HEADS * HEAD_DIM)
            ys_c = (_conv_call(pc_flat, w_conv, layer, n_ctx),
                    flat(_fourier_call(pc, ccsc, cpsp_ctx, f_block)),
                    flat(_ctx_attn_call(pc, sink, layer)),
                    _chunk_mlp_call(pc_flat, w_s, mlp_bias, layer))
            xc_new = _out_proj_call(ys_c, xc, mods, w_out_b, layer, ctx_row)
            xc_next = _swiglu_call(xc_new, mods, g2, w_gate_b, w_up_b, w_down_b, gf, layer, ctx_row, False)
        p = _norm_in_call(x, mods, g1, w_in_b, layer, layer, None, tn=768)
        ys = (_conv_call(p, w_conv, layer, n_pos),
              _fourier_call(p, ccsc, cpsp, f_block),
              _latent_attn_call(p, pc, kc_block, sink, rope_cos, rope_sin, layer),
              _chunk_mlp_call(p, w_s, mlp_bias, layer))
        x = _out_proj_call(ys, x, mods, w_out_b, layer, None)
        x = _swiglu_call(x, mods, g2, w_gate_b, w_up_b, w_down_b, gf, layer, None, last)
        if not last:
            xc = xc_next
    return x
```

```python
import functools
import math

import jax
import jax.numpy as jnp
from jax import lax
from jax.experimental import pallas as pl
from jax.experimental.pallas import tpu as pltpu

F32 = jnp.float32
BF16 = jnp.bfloat16

EPS = 1e-6
NEG = -1e30
GRID_W = 64
HEAD_DIM = 64
N_HEADS = 8
N_KV_HEADS = 2
HEADS_PER_KV = N_HEADS // N_KV_HEADS
BLOCK = 128
CHUNK = 128
N_GROUPS = 4
ROPE_THETA = 10000.0
SCALE = HEAD_DIM ** -0.5
LOG2E = math.log2(math.e)

LANES = 128
VMEM_LIMIT_BYTES = 60 * 1024 * 1024
N_MOD_ROWS = 24

SH1, SC1, GT1, SH2, SC2, GT2 = range(6)


def _cparams(sem):
    return pltpu.CompilerParams(dimension_semantics=sem, vmem_limit_bytes=VMEM_LIMIT_BYTES)


def _resident(block_shape, index_map):
    return pl.BlockSpec(block_shape, index_map, pipeline_mode=pl.Buffered(1))


def _ada_kernel(c_ref, w_ref, b_ref, o_ref):
    c = c_ref[...]
    s = (c * jax.nn.sigmoid(c)).astype(BF16)
    o_ref[...] = jnp.dot(s, w_ref[...].astype(BF16), preferred_element_type=F32) + b_ref[...]


def _ada_call(cs, w_ada, b_ada):
    depth, d, n = w_ada.shape
    tn = 1024
    return pl.pallas_call(
        _ada_kernel,
        grid=(depth, n // tn),
        in_specs=[
            pl.BlockSpec((N_MOD_ROWS, d), lambda l, j: (0, 0)),
            pl.BlockSpec((None, d, tn), lambda l, j: (l, 0, j)),
            pl.BlockSpec((None, 1, tn), lambda l, j: (l, 0, j)),
        ],
        out_specs=pl.BlockSpec((None, N_MOD_ROWS, tn), lambda l, j: (l, 0, j)),
        out_shape=jax.ShapeDtypeStruct((depth, N_MOD_ROWS, n), F32),
        compiler_params=_cparams(("parallel", "parallel")),
        name="ada_mod",
    )(cs, w_ada, b_ada.reshape(depth, 1, n))


def _mod_spec(layer, row, chunk, d):
    if row is None:
        return pl.BlockSpec((None, None, 1, d), lambda b, *_: (layer, b, 0, chunk))
    return pl.BlockSpec((None, None, 1, d), lambda b, *_: (layer, row, 0, chunk))


def _norm_modulate_rows(x_ref, g_ref, sc_ref, sh_ref, h_ref, tm, rb):
    def body(r, carry):
        rows = pl.ds(pl.multiple_of(r * rb, rb), rb)
        xr = x_ref[0, rows, :]
        ms = jnp.mean(xr * xr, axis=-1, keepdims=True)
        y = xr * lax.rsqrt(ms + EPS)
        h = (y * g_ref[...]) * (1.0 + sc_ref[...]) + sh_ref[...]
        h_ref[rows, :] = h.astype(h_ref.dtype)
        return carry

    lax.fori_loop(0, tm // rb, body, 0)


def _norm_in_kernel(x_ref, xn_ref, g_ref, sc_ref, sh_ref, scn_ref, shn_ref, w_ref, o_ref, ha_ref, hb_ref,
                    *, tm, tn):
    t = pl.program_id(0) * pl.num_programs(1) + pl.program_id(1)
    n = o_ref.shape[-1]

    @pl.when(t == 0)
    def _():
        _norm_modulate_rows(x_ref, g_ref, sc_ref, sh_ref, ha_ref, tm, min(tm, 64))

    def step(h_cur, h_next):
        rb = min(tm, 64)
        for r in range(tm // rb):
            rows = slice(r * rb, (r + 1) * rb)
            xr = xn_ref[0, rows, :]
            ms = jnp.mean(xr * xr, axis=-1, keepdims=True)
            h = ((xr * lax.rsqrt(ms + EPS)) * g_ref[...]) * (1.0 + scn_ref[...]) + shn_ref[...]
            h_next[rows, :] = h.astype(BF16)
        for j in range(n // tn):
            cols = slice(j * tn, (j + 1) * tn)
            o_ref[0, :, cols] = jnp.dot(h_cur[...], w_ref[:, cols],
                                        preferred_element_type=F32).astype(o_ref.dtype)

    parity = lax.rem(t, 2)

    @pl.when(parity == 0)
    def _():
        step(ha_ref, hb_ref)

    @pl.when(parity == 1)
    def _():
        step(hb_ref, ha_ref)


def _norm_in_call(x, mods, g, w, layer, w_layer, mod_row, tn):
    b, l, d = x.shape
    n = w.shape[-1]
    tm = min(512, l)
    nt = l // tm
    kern = functools.partial(_norm_in_kernel, tm=tm, tn=tn)

    def nxt(bi, i):
        t = jnp.minimum(bi * nt + i + 1, b * nt - 1)
        return t // nt, t % nt

    def mod_next(chunk):
        if mod_row is None:
            return pl.BlockSpec((None, None, 1, d), lambda bi, i: (layer, nxt(bi, i)[0], 0, chunk))
        return _mod_spec(layer, mod_row, chunk, d)

    return pl.pallas_call(
        kern,
        grid=(b, nt),
        in_specs=[
            pl.BlockSpec((1, tm, d), lambda bi, i: (bi, i, 0)),
            pl.BlockSpec((1, tm, d), lambda bi, i: nxt(bi, i) + (0,)),
            pl.BlockSpec((None, 1, d), lambda bi, i: (layer, 0, 0)),
            _mod_spec(layer, mod_row, SC1, d),
            _mod_spec(layer, mod_row, SH1, d),
            mod_next(SC1),
            mod_next(SH1),
            _resident((None, d, n), lambda bi, i: (w_layer, 0, 0)),
        ],
        out_specs=pl.BlockSpec((1, tm, n), lambda bi, i: (bi, i, 0)),
        out_shape=jax.ShapeDtypeStruct((b, l, n), BF16),
        scratch_shapes=[pltpu.VMEM((tm, d), BF16), pltpu.VMEM((tm, d), BF16)],
        compiler_params=_cparams(("arbitrary", "arbitrary")),
        name="norm_in",
    )(x, x, g, mods, mods, mods, mods, w)


def _conv_kernel(x_ref, gb_ref, gc_ref, w_ref, o_ref, *, seq_len):
    n_rows = x_ref.shape[1]
    pos = lax.broadcasted_iota(jnp.int32, (n_rows, LANES), 0) % seq_len
    first = pos == 0
    last = pos == seq_len - 1
    for s in range(x_ref.shape[2] // LANES):
        cols = slice(s * LANES, (s + 1) * LANES)
        z = gc_ref[0, :, cols].astype(F32) * x_ref[0, :, cols].astype(F32)
        z_prev = jnp.where(first, 0.0, pltpu.roll(z, 1, 0))
        z_next = jnp.where(last, 0.0, pltpu.roll(z, n_rows - 1, 0))
        y = z_prev * w_ref[0:1, cols] + z * w_ref[1:2, cols] + z_next * w_ref[2:3, cols]
        o_ref[0, :, cols] = (gb_ref[0, :, cols].astype(F32) * y).astype(o_ref.dtype)


def _conv_call(p, w_conv, layer, seq_len):
    b, l, _ = p.shape
    width = w_conv.shape[-1]
    blk = lambda j: pl.BlockSpec((1, l, width), lambda bi: (bi, 0, j))
    return pl.pallas_call(
        functools.partial(_conv_kernel, seq_len=seq_len),
        grid=(b,),
        in_specs=[blk(0), blk(1), blk(2),
                  pl.BlockSpec((None, 3, width), lambda bi: (layer, 0, 0))],
        out_specs=pl.BlockSpec((1, l, width), lambda bi: (bi, 0, 0)),
        out_shape=jax.ShapeDtypeStruct((b, l, width), BF16),
        compiler_params=_cparams(("parallel",)),
        name="short_conv",
    )(p, p, p, w_conv)


def _fourier_kernel(z_ref, ccsc_ref, cpsp_ref, o_ref, rhs_ref, *, out_scale):
    n_pos = z_ref.shape[1]
    cw = ccsc_ref.shape[0]
    for g in range(N_GROUPS):
        cols = slice(g * cw, (g + 1) * cw)
        ab = jnp.dot(z_ref[0, :, cols], ccsc_ref[...], preferred_element_type=F32)
        rhs_ref[0:n_pos, cols] = ab[:, :cw].astype(BF16)
        rhs_ref[n_pos:2 * n_pos, cols] = ab[:, cw:].astype(BF16)
    out = jnp.dot(cpsp_ref[...], rhs_ref[...], preferred_element_type=F32)
    o_ref[0] = (out * out_scale).astype(o_ref.dtype)


def _fourier_call(p, ccsc, cpsp, col_block):
    b, l, _ = p.shape
    cw = ccsc.shape[0]
    width = N_GROUPS * cw
    kern = functools.partial(_fourier_kernel, out_scale=1.0 / math.sqrt(l * cw))
    return pl.pallas_call(
        kern,
        grid=(b,),
        in_specs=[
            pl.BlockSpec((1, l, width), lambda bi: (bi, 0, col_block)),
            _resident((cw, 2 * cw), lambda bi: (0, 0)),
            _resident((l, 2 * l), lambda bi: (0, 0)),
        ],
        out_specs=pl.BlockSpec((1, l, width), lambda bi: (bi, 0, 0)),
        out_shape=jax.ShapeDtypeStruct((b, l, width), BF16),
        scratch_shapes=[pltpu.VMEM((2 * l, width), BF16)],
        compiler_params=_cparams(("parallel",)),
        name="fourier_mix",
    )(p, ccsc, cpsp)


def _dft_tables(n):
    k = jnp.arange(n, dtype=jnp.int32)
    ang = ((k[:, None] * k[None, :]) % n).astype(F32) * (2.0 * math.pi / n)
    return jnp.cos(ang), jnp.sin(ang)


FFT_RADIX = 8


def _fourier_fft_kernel(z_ref, ccsc_ref, fsub_ref, twr_ref, twi_ref, o_ref, w_ref, g_ref, *, out_scale):
    n_pos = z_ref.shape[1]
    m = n_pos // FFT_RADIX
    cw = ccsc_ref.shape[0]
    c = math.sqrt(0.5)
    rb = 64
    gpp = 2
    for gp in range(N_GROUPS // gpp):
        groups = range(gp * gpp, (gp + 1) * gpp)
        pcols = slice(gp * gpp * cw, (gp + 1) * gpp * cw)
        for g in groups:
            ab = jnp.dot(z_ref[0, :, g * cw:(g + 1) * cw], ccsc_ref[...], preferred_element_type=F32)
            w_ref[g] = ab[:, :cw]
            w_ref[N_GROUPS + g] = ab[:, cw:]
        for n2 in range(FFT_RADIX):
            sub = pl.ds(n2, m, stride=FFT_RADIX)
            rhs = jnp.concatenate(
                [jnp.concatenate([w_ref[part * N_GROUPS + g, sub, :] for g in groups], axis=1)
                 for part in range(2)], axis=0).astype(BF16)
            g_ref[n2, :, pcols] = jnp.dot(fsub_ref[...], rhs, preferred_element_type=F32)
        for r in range(m // rb):
            rows = slice(r * rb, (r + 1) * rb)
            irows = slice(m + r * rb, m + (r + 1) * rb)
            for g in groups:
                cols = slice(g * cw, (g + 1) * cw)
                hr, hi = [g_ref[0, rows, cols]], [g_ref[0, irows, cols]]
                for n2 in range(1, FFT_RADIX):
                    gr, gi = g_ref[n2, rows, cols], g_ref[n2, irows, cols]
                    tr, ti = twr_ref[n2, rows, :], twi_ref[n2, rows, :]
                    hr.append(tr * gr - ti * gi)
                    hi.append(tr * gi + ti * gr)
                a0, a1 = hr[0] + hr[4], hr[0] - hr[4]
                a2 = hr[2] + hr[6]
                b0, b1 = hr[1] + hr[5], hr[1] - hr[5]
                b2, b3 = hr[3] + hr[7], hr[3] - hr[7]
                d1, d2, d3 = hi[1] - hi[5], hi[2] - hi[6], hi[3] - hi[7]
                e0, e1 = a0 + a2, b0 + b2
                odd_r = c * (b1 - b3)
                odd_i = c * (d1 + d3)
                p1, p3 = a1 + odd_r, a1 - odd_r
                p2 = a0 - a2
                q1, q3 = odd_i + d2, odd_i - d2
                q2 = (hi[1] + hi[5]) - (hi[3] + hi[7])
                ys = (e0 + e1, p1 + q1, p2 + q2, p3 + q3, e0 - e1, p3 - q3, p2 - q2, p1 - q1)
                for k2, y in enumerate(ys):
                    orow = slice(k2 * m + r * rb, k2 * m + (r + 1) * rb)
                    o_ref[0, orow, cols] = (y * out_scale).astype(o_ref.dtype)


def _fourier_fft_call(p, ccsc_neg, fsub, twr, twi, col_block):
    b, l, _ = p.shape
    cw = ccsc_neg.shape[0]
    width = N_GROUPS * cw
    m = l // FFT_RADIX
    kern = functools.partial(_fourier_fft_kernel, out_scale=1.0 / math.sqrt(l * cw))
    return pl.pallas_call(
        kern,
        grid=(b,),
        in_specs=[
            pl.BlockSpec((1, l, width), lambda bi: (bi, 0, col_block)),
            _resident((cw, 2 * cw), lambda bi: (0, 0)),
            _resident((2 * m, 2 * m), lambda bi: (0, 0)),
            _resident((FFT_RADIX, m, LANES), lambda bi: (0, 0, 0)),
            _resident((FFT_RADIX, m, LANES), lambda bi: (0, 0, 0)),
        ],
        out_specs=pl.BlockSpec((1, l, width), lambda bi: (bi, 0, 0)),
        out_shape=jax.ShapeDtypeStruct((b, l, width), BF16),
        scratch_shapes=[pltpu.VMEM((2 * N_GROUPS, l, cw), F32),
                        pltpu.VMEM((FFT_RADIX, 2 * m, width), F32)],
        compiler_params=_cparams(("parallel",)),
        name="fourier_fft",
    )(p, ccsc_neg, fsub, twr, twi)


def _fft_tables(n_pos):
    m = n_pos // FFT_RADIX
    cos_m, sin_m = _dft_tables(m)
    fsub = jnp.concatenate([jnp.concatenate([cos_m, sin_m], axis=1),
                            jnp.concatenate([-sin_m, cos_m], axis=1)], axis=0).astype(BF16)
    n2 = jnp.arange(FFT_RADIX, dtype=jnp.int32)[:, None]
    k1 = jnp.arange(m, dtype=jnp.int32)[None, :]
    ang = ((n2 * k1) % n_pos).astype(F32) * (2.0 * math.pi / n_pos)
    bcast = lambda t: jnp.broadcast_to(t[:, :, None], (FFT_RADIX, m, LANES))
    return fsub, bcast(jnp.cos(ang)), bcast(-jnp.sin(ang))


def _nt_dot(a, b):
    return lax.dot_general(a, b, (((1,), (1,)), ((), ())), preferred_element_type=F32)


def _latent_attn_kernel(sink_ref, q_ref, k_ref, v_ref, kc_ref, vc_ref, cos_ref, sin_ref, o_ref,
                        qlo_ref, qhi_ref, kpad_ref, kswp_ref, vt_ref, kcs_ref, vct_ref, *, layer):
    n_pos = q_ref.shape[1]
    n_blk = n_pos // BLOCK
    n_ctx = kc_ref.shape[1]
    half = LANES // 2

    lane = lax.broadcasted_iota(jnp.int32, (BLOCK, LANES), 1)
    low16 = (lane % 32) < 16
    lo_half = lane < half

    def prep_block(r, carry):
        rows = pl.ds(pl.multiple_of(r * BLOCK, BLOCK), BLOCK)
        cos = cos_ref[rows, :]
        sin = sin_ref[rows, :]

        def rope(t):
            swapped = jnp.where(low16, pltpu.roll(t, LANES - 16, 1), pltpu.roll(t, 16, 1))
            return t * cos + swapped * sin

        for s in range(q_ref.shape[-1] // LANES):
            cols = slice(s * LANES, (s + 1) * LANES)
            t = rope(q_ref[0, rows, cols].astype(F32)) * (SCALE * LOG2E)
            qlo_ref[rows, cols] = jnp.where(lo_half, t, 0.0).astype(BF16)
            qhi_ref[rows, cols] = jnp.where(lo_half, 0.0, t).astype(BF16)
        prow = pl.ds(pl.multiple_of(r * BLOCK + BLOCK, BLOCK), BLOCK)
        tk = rope(k_ref[0, rows, :].astype(F32))
        kpad_ref[prow, :] = tk.astype(BF16)
        kswp_ref[prow, :] = pltpu.roll(tk, half, 1).astype(BF16)
        vt_ref[r + 1] = v_ref[0, rows, :].astype(F32).T.astype(BF16)
        return carry

    zeros = jnp.zeros((BLOCK, LANES), BF16)
    for ref in (kpad_ref, kswp_ref):
        ref[0:BLOCK, :] = zeros
        ref[n_pos + BLOCK:n_pos + 2 * BLOCK, :] = zeros
    vt_ref[0] = zeros
    vt_ref[n_blk + 1] = zeros
    kcs_ref[...] = pltpu.roll(kc_ref[0].astype(F32), half, 1).astype(BF16)
    for cb in range(n_ctx // BLOCK):
        crow = slice(cb * BLOCK, (cb + 1) * BLOCK)
        vct_ref[:, crow] = vc_ref[0, crow, :].astype(F32).T.astype(BF16)
    lax.fori_loop(0, n_blk, prep_block, 0)

    head_order = (0, 2, 5, 7, 1, 3, 4, 6)
    col_of = {h: i for i, h in enumerate(head_order)}
    n_col = N_HEADS * BLOCK
    kj = lax.broadcasted_iota(jnp.int32, (3 * BLOCK, BLOCK), 0)
    qi = lax.broadcasted_iota(jnp.int32, (3 * BLOCK, BLOCK), 1)
    in_window = (kj >= qi) & (kj <= qi + 2 * BLOCK)
    sk = jnp.concatenate([jnp.full((1, BLOCK), sink_ref[layer, h] * LOG2E, F32) for h in head_order], axis=1)
    neg = NEG * LOG2E

    def slab(ref, rows, s):
        return ref[rows, s * LANES:(s + 1) * LANES]

    def attn_block(n, carry):
        rows = pl.ds(pl.multiple_of(n * BLOCK, BLOCK), BLOCK)
        band = pl.ds(pl.multiple_of(n * BLOCK, BLOCK), 3 * BLOCK)
        valid = in_window & (kj >= (1 - n) * BLOCK) & (kj < n_pos - (n - 1) * BLOCK)

        def masked(s):
            return jnp.concatenate([jnp.where(valid, s[:, i * BLOCK:(i + 1) * BLOCK], neg)
                                    for i in range(s.shape[1] // BLOCK)], axis=1)

        rhs_nat = jnp.concatenate([slab(qlo_ref, rows, 0), slab(qlo_ref, rows, 1),
                                   slab(qhi_ref, rows, 2), slab(qhi_ref, rows, 3)], axis=0)
        rhs_swp = jnp.concatenate([slab(qhi_ref, rows, 0), slab(qhi_ref, rows, 1),
                                   slab(qlo_ref, rows, 2), slab(qlo_ref, rows, 3)], axis=0)
        s_loc = jnp.concatenate([masked(_nt_dot(kpad_ref[band, :], rhs_nat)),
                                 masked(_nt_dot(kswp_ref[band, :], rhs_swp))], axis=1)
        s_ctx = jnp.concatenate([_nt_dot(kc_ref[0], rhs_nat),
                                 _nt_dot(kcs_ref[...], rhs_swp)], axis=1)
        m = jnp.maximum(jnp.maximum(jnp.max(s_loc, axis=0, keepdims=True),
                                    jnp.max(s_ctx, axis=0, keepdims=True)), sk)
        e_loc = jnp.exp2(s_loc - m)
        e_ctx = jnp.exp2(s_ctx - m)
        inv_den = 1.0 / (jnp.sum(e_loc, axis=0, keepdims=True) + jnp.sum(e_ctx, axis=0, keepdims=True)
                         + jnp.exp2(sk - m))
        e_loc = e_loc.astype(BF16)
        e_ctx = e_ctx.astype(BF16)
        vt_band = jnp.concatenate([vt_ref[n], vt_ref[n + 1], vt_ref[n + 2]], axis=1)
        o_t = {}
        for g in range(N_KV_HEADS):
            heads = [4 * g + j for j in range(HEADS_PER_KV)]
            pick = lambda arr: jnp.concatenate(
                [arr[:, col_of[h] * BLOCK:(col_of[h] + 1) * BLOCK] for h in heads], axis=1)
            vrows = slice(g * HEAD_DIM, (g + 1) * HEAD_DIM)
            o = (jnp.dot(vt_band[vrows], pick(e_loc), preferred_element_type=F32)
                 + jnp.dot(vct_ref[vrows, :], pick(e_ctx), preferred_element_type=F32)) * pick(inv_den)
            for j, h in enumerate(heads):
                o_t[h] = o[:, j * BLOCK:(j + 1) * BLOCK]
        slabs = [jnp.concatenate([o_t[2 * s], o_t[2 * s + 1]], axis=0).T for s in range(N_HEADS // 2)]
        o_ref[0, rows, :] = jnp.concatenate(slabs, axis=1).astype(o_ref.dtype)
        return carry

    lax.fori_loop(0, n_blk, attn_block, 0, unroll=4)


def _latent_attn_call(p, pc, kc_block, sink, cos_t, sin_t, layer):
    b, l, _ = p.shape
    lc = pc.shape[1]
    qw = N_HEADS * HEAD_DIM
    kvw = N_KV_HEADS * HEAD_DIM
    q_block = 2048 // qw
    k_block = 2560 // kvw
    kern = functools.partial(_latent_attn_kernel, layer=layer)
    return pl.pallas_call(
        kern,
        grid=(b,),
        in_specs=[
            pl.BlockSpec(memory_space=pltpu.SMEM),
            pl.BlockSpec((1, l, qw), lambda bi: (bi, 0, q_block)),
            pl.BlockSpec((1, l, kvw), lambda bi: (bi, 0, k_block)),
            pl.BlockSpec((1, l, kvw), lambda bi: (bi, 0, k_block + 1)),
            pl.BlockSpec((1, lc, kvw), lambda bi: (bi, 0, kc_block)),
            pl.BlockSpec((1, lc, kvw), lambda bi: (bi, 0, kc_block + 1)),
            _resident((l, LANES), lambda bi: (0, 0)),
            _resident((l, LANES), lambda bi: (0, 0)),
        ],
        out_specs=pl.BlockSpec((1, l, qw), lambda bi: (bi, 0, 0)),
        out_shape=jax.ShapeDtypeStruct((b, l, qw), BF16),
        scratch_shapes=[pltpu.VMEM((l, qw), BF16),
                        pltpu.VMEM((l, qw), BF16),
                        pltpu.VMEM((l + 2 * BLOCK, kvw), BF16),
                        pltpu.VMEM((l + 2 * BLOCK, kvw), BF16),
                        pltpu.VMEM((l // BLOCK + 2, kvw, BLOCK), BF16),
                        pltpu.VMEM((lc, kvw), BF16),
                        pltpu.VMEM((kvw, lc), BF16)],
        compiler_params=_cparams(("parallel",)),
        name="latent_attention",
    )(sink, p, p, p, pc, pc, cos_t, sin_t)


def _ctx_attn_kernel(sink_ref, q_ref, kc_ref, vc_ref, o_ref, *, layer):
    outs = []
    for h in range(N_HEADS):
        kv = h // HEADS_PER_KV
        hc = slice(kv * HEAD_DIM, (kv + 1) * HEAD_DIM)
        qh = q_ref[0, :, h * HEAD_DIM:(h + 1) * HEAD_DIM] * SCALE
        s = _nt_dot(qh, kc_ref[0, :, hc])
        sk = sink_ref[layer, h]
        m = jnp.maximum(jnp.max(s, axis=-1, keepdims=True), sk)
        e = jnp.exp(s - m)
        den = jnp.sum(e, axis=-1, keepdims=True) + jnp.exp(sk - m)
        o = jnp.dot(e.astype(BF16), vc_ref[0, :, hc], preferred_element_type=F32)
        outs.append(o / den)
    o_ref[0] = jnp.concatenate(outs, axis=-1).astype(o_ref.dtype)


def _ctx_attn_call(pc, sink, layer):
    b, lc, _ = pc.shape
    qw = N_HEADS * HEAD_DIM
    kvw = N_KV_HEADS * HEAD_DIM
    kern = functools.partial(_ctx_attn_kernel, layer=layer)
    return pl.pallas_call(
        kern,
        grid=(b,),
        in_specs=[
            pl.BlockSpec(memory_space=pltpu.SMEM),
            pl.BlockSpec((1, lc, qw), lambda bi: (bi, 0, 2048 // qw)),
            pl.BlockSpec((1, lc, kvw), lambda bi: (bi, 0, 2560 // kvw)),
            pl.BlockSpec((1, lc, kvw), lambda bi: (bi, 0, 2560 // kvw + 1)),
        ],
        out_specs=pl.BlockSpec((1, lc, qw), lambda bi: (bi, 0, 0)),
        out_shape=jax.ShapeDtypeStruct((b, lc, qw), BF16),
        compiler_params=_cparams(("parallel",)),
        name="context_attention",
    )(sink, pc, pc, pc)


def _rope_tables(n_pos):
    quarter = HEAD_DIM // 4
    inv = ROPE_THETA ** (-jnp.arange(quarter, dtype=F32) / quarter)
    pos = jnp.arange(n_pos, dtype=jnp.int32)
    row = (pos // GRID_W).astype(F32)
    col = (pos % GRID_W).astype(F32)
    a_row = row[:, None] * inv[None, :]
    a_col = col[:, None] * inv[None, :]
    ang = jnp.concatenate([a_row, a_row, a_col, a_col], axis=-1)
    sign = jnp.tile(jnp.concatenate([-jnp.ones((quarter,), F32), jnp.ones((quarter,), F32)]), 2)
    reps = LANES // HEAD_DIM
    return jnp.tile(jnp.cos(ang), (1, reps)), jnp.tile(jnp.sin(ang) * sign, (1, reps))


def _gelu_tanh(x):
    k1 = -2.0 * math.sqrt(2.0 / math.pi) * LOG2E
    t = x * (k1 + (k1 * 0.044715) * (x * x))
    return x * (1.0 / (1.0 + jnp.exp2(t)))


def _chunk_mlp_kernel(u_ref, v_ref, w_ref, bias_ref, o_ref):
    n_rows = u_ref.shape[1]
    averager = jnp.full((LANES, LANES), 1.0 / LANES, BF16)
    for gi in range(u_ref.shape[2] // LANES):
        cols = slice(gi * LANES, (gi + 1) * LANES)
        u = _gelu_tanh(u_ref[0, :, cols].astype(F32))
        v = _gelu_tanh(v_ref[0, :, cols].astype(F32))
        mu = jnp.dot(v.astype(BF16), averager, preferred_element_type=F32)
        dlt = v - mu
        var = jnp.dot((dlt * dlt).astype(BF16), averager, preferred_element_type=F32)
        vn = (dlt * lax.rsqrt(var + EPS)).astype(BF16)
        w = w_ref[gi].astype(BF16)
        for n in range(n_rows // CHUNK):
            rows = slice(n * CHUNK, (n + 1) * CHUNK)
            mixed = jnp.dot(w, vn[rows], preferred_element_type=F32) + bias_ref[gi]
            o_ref[0, rows, cols] = (u[rows] * mixed).astype(o_ref.dtype)


def _chunk_mlp_call(p, w_s, bias, layer):
    b, l, _ = p.shape
    gpb = 2
    bw = gpb * LANES
    u_block = 2816 // bw
    v_block = 3328 // bw
    return pl.pallas_call(
        _chunk_mlp_kernel,
        grid=(b, N_GROUPS // gpb),
        in_specs=[
            pl.BlockSpec((1, l, bw), lambda bi, j: (bi, 0, u_block + j)),
            pl.BlockSpec((1, l, bw), lambda bi, j: (bi, 0, v_block + j)),
            pl.BlockSpec((None, gpb, CHUNK, CHUNK), lambda bi, j: (layer, j, 0, 0)),
            pl.BlockSpec((None, gpb, CHUNK, LANES), lambda bi, j: (layer, j, 0, 0)),
        ],
        out_specs=pl.BlockSpec((1, l, bw), lambda bi, j: (bi, 0, j)),
        out_shape=jax.ShapeDtypeStruct((b, l, N_GROUPS * LANES), BF16),
        compiler_params=_cparams(("parallel", "parallel")),
        name="chunk_mlp",
    )(p, p, w_s, bias)


def _out_proj_kernel(ya_ref, yb_ref, yc_ref, yd_ref, x_ref, gt_ref, w_ref, o_ref, *, tn):
    d = o_ref.shape[-1]
    kw = ya_ref.shape[-1]
    ys = (ya_ref, yb_ref, yc_ref, yd_ref)
    for j in range(d // tn):
        cols = slice(j * tn, (j + 1) * tn)
        acc = jnp.dot(ys[0][0], w_ref[0:kw, cols], preferred_element_type=F32)
        for k in range(1, 4):
            acc = acc + jnp.dot(ys[k][0], w_ref[k * kw:(k + 1) * kw, cols], preferred_element_type=F32)
        o_ref[0, :, cols] = x_ref[0, :, cols] + gt_ref[:, cols] * acc


def _out_proj_call(ys, x, mods, w_out, layer, mod_row):
    b, l, d = x.shape
    kw = ys[0].shape[-1]
    tm = min(1024, l)
    kern = functools.partial(_out_proj_kernel, tn=512)
    yspec = pl.BlockSpec((1, tm, kw), lambda bi, i: (bi, i, 0))
    return pl.pallas_call(
        kern,
        grid=(b, l // tm),
        in_specs=[yspec, yspec, yspec, yspec,
                  pl.BlockSpec((1, tm, d), lambda bi, i: (bi, i, 0)),
                  _mod_spec(layer, mod_row, GT1, d),
                  _resident((None, d, d), lambda bi, i: (layer, 0, 0))],
        out_specs=pl.BlockSpec((1, tm, d), lambda bi, i: (bi, i, 0)),
        out_shape=jax.ShapeDtypeStruct((b, l, d), F32),
        compiler_params=_cparams(("parallel", "parallel")),
        name="out_proj",
    )(*ys, x, mods, w_out)


def _swiglu_kernel(x_ref, g_ref, sc_ref, sh_ref, gt_ref, wg_ref, wu_ref, wd_ref, gf_ref, o_ref, h_ref,
                   *, tm, final_norm):
    f = pl.program_id(2)

    def gated_ffn(h):
        gate = jnp.dot(h, wg_ref[...], preferred_element_type=F32)
        up = jnp.dot(h, wu_ref[...], preferred_element_type=F32)
        a = (gate * jax.nn.sigmoid(gate) * up).astype(BF16)
        return gt_ref[...] * jnp.dot(a, wd_ref[...], preferred_element_type=F32)

    @pl.when(f == 0)
    def _():
        rb = min(tm, 256)
        for r in range(tm // rb):
            rows = slice(r * rb, (r + 1) * rb)
            xr = x_ref[0, rows, :]
            ms = jnp.mean(xr * xr, axis=-1, keepdims=True)
            h = (((xr * lax.rsqrt(ms + EPS)) * g_ref[...]) * (1.0 + sc_ref[...]) + sh_ref[...]).astype(BF16)
            h_ref[rows, :] = h
            o_ref[0, rows, :] = xr + gated_ffn(h)

    @pl.when(f != 0)
    def _():
        o_ref[0] += gated_ffn(h_ref[...])

    if final_norm:
        @pl.when(f == pl.num_programs(2) - 1)
        def _():
            def body(r, carry):
                rows = pl.ds(pl.multiple_of(r * 64, 64), 64)
                xr = o_ref[0, rows, :]
                ms = jnp.mean(xr * xr, axis=-1, keepdims=True)
                o_ref[0, rows, :] = (xr * lax.rsqrt(ms + EPS)) * gf_ref[...]
                return carry

            lax.fori_loop(0, tm // 64, body, 0)


def _swiglu_call(x, mods, g, w_gate, w_up, w_down, g_final, layer, mod_row, final_norm):
    b, l, d = x.shape
    ff = w_gate.shape[-1]
    tm = min(1024, l)
    tf = 512
    kern = functools.partial(_swiglu_kernel, tm=tm, final_norm=final_norm)
    return pl.pallas_call(
        kern,
        grid=(b, l // tm, ff // tf),
        in_specs=[
            pl.BlockSpec((1, tm, d), lambda bi, i, f: (bi, i, 0)),
            pl.BlockSpec((None, 1, d), lambda bi, i, f: (layer, 0, 0)),
            _mod_spec(layer, mod_row, SC2, d),
            _mod_spec(layer, mod_row, SH2, d),
            _mod_spec(layer, mod_row, GT2, d),
            pl.BlockSpec((None, d, tf), lambda bi, i, f: (layer, 0, f)),
            pl.BlockSpec((None, d, tf), lambda bi, i, f: (layer, 0, f)),
            pl.BlockSpec((None, tf, d), lambda bi, i, f: (layer, f, 0)),
            pl.BlockSpec((1, d), lambda bi, i, f: (0, 0)),
        ],
        out_specs=pl.BlockSpec((1, tm, d), lambda bi, i, f: (bi, i, 0)),
        out_shape=jax.ShapeDtypeStruct((b, l, d), F32),
        scratch_shapes=[pltpu.VMEM((tm, d), BF16)],
        compiler_params=_cparams(("parallel", "parallel", "arbitrary")),
        name="swiglu",
    )(x, g, mods, mods, mods, w_gate, w_up, w_down, g_final)


def kernel(x, c, ctx, c_ctx, w_ada, b_ada, g_norm1, w_in, w_conv, sink, w_s, b_s, w_out, g_norm2,
           w_gate, w_up, w_down, g_final):
    batch, n_pos, d = x.shape
    n_ctx = ctx.shape[1]
    depth = w_ada.shape[0]
    cw = (d // 4) // N_GROUPS
    ctx_row = batch

    w_in_b = w_in.astype(BF16)
    w_out_b = w_out.astype(BF16)
    w_gate_b = w_gate.astype(BF16)
    w_up_b = w_up.astype(BF16)
    w_down_b = w_down.astype(BF16)
    g1 = g_norm1.reshape(depth, 1, d)
    g2 = g_norm2.reshape(depth, 1, d)
    gf = g_final.reshape(1, d)

    cs = jnp.concatenate([c, c_ctx[None, :], jnp.zeros((N_MOD_ROWS - batch - 1, d), F32)], axis=0)
    mods = _ada_call(cs, w_ada, b_ada).reshape(depth, N_MOD_ROWS, 1, 6 * d)

    cos_c, sin_c = _dft_tables(cw)
    ccsc = jnp.concatenate([cos_c, sin_c], axis=1).astype(BF16)
    use_fft = n_pos % (FFT_RADIX * 64) == 0 and n_pos // FFT_RADIX >= LANES
    if use_fft:
        ccsc_neg = jnp.concatenate([cos_c, -sin_c], axis=1).astype(BF16)
        fsub, twr, twi = _fft_tables(n_pos)
        latent_fourier = lambda pp: _fourier_fft_call(pp, ccsc_neg, fsub, twr, twi, f_block)
    else:
        cos_p, sin_p = _dft_tables(n_pos)
        cpsp = jnp.concatenate([cos_p, -sin_p], axis=1).astype(BF16)
        latent_fourier = lambda pp: _fourier_call(pp, ccsc, cpsp, f_block)
    cos_x, sin_x = _dft_tables(n_ctx)
    cpsp_ctx = jnp.concatenate([cos_x, -sin_x], axis=1).astype(BF16)
    rope_cos, rope_sin = _rope_tables(n_pos)
    mlp_bias = jnp.broadcast_to(b_s[:, :, :, None], b_s.shape + (cw,))
    f_block = 1536 // (N_GROUPS * cw)

    def flat(a):
        return a.reshape(1, -1, a.shape[-1])

    def unflat(a):
        return a.reshape(batch, n_ctx, a.shape[-1])

    xc = flat(ctx)
    for layer in range(depth):
        last = layer == depth - 1
        if last:
            w_kv = w_in_b[layer:layer + 1, :, 2560:2816]
            pc = unflat(_norm_in_call(xc, mods, g1, w_kv, layer, 0, ctx_row, tn=256))
            kc_block = 0
        else:
            pc_flat = _norm_in_call(xc, mods, g1, w_in_b, layer, layer, ctx_row, tn=768)
            pc = unflat(pc_flat)
            kc_block = 2560 // (N_KV_HEADS * HEAD_DIM)
            ys_c = (_conv_call(pc_flat, w_conv, layer, n_ctx),
                    flat(_fourier_call(pc, ccsc, cpsp_ctx, f_block)),
                    flat(_ctx_attn_call(pc, sink, layer)),
                    _chunk_mlp_call(pc_flat, w_s, mlp_bias, layer))
            xc_new = _out_proj_call(ys_c, xc, mods, w_out_b, layer, ctx_row)
            xc_next = _swiglu_call(xc_new, mods, g2, w_gate_b, w_up_b, w_down_b, gf, layer, ctx_row, False)
        p = _norm_in_call(x, mods, g1, w_in_b, layer, layer, None, tn=768)
        ys = (_conv_call(p, w_conv, layer, n_pos),
              latent_fourier(p),
              _latent_attn_call(p, pc, kc_block, sink, rope_cos, rope_sin, layer),
              _chunk_mlp_call(p, w_s, mlp_bias, layer))
        x = _out_proj_call(ys, x, mods, w_out_b, layer, None)
        x = _swiglu_call(x, mods, g2, w_gate_b, w_up_b, w_down_b, gf, layer, None, last)
        if not last:
            xc = xc_next
    return x
```

```python
import functools
import math

import jax
import jax.numpy as jnp
from jax import lax
from jax.experimental import pallas as pl
from jax.experimental.pallas import tpu as pltpu

F32 = jnp.float32
BF16 = jnp.bfloat16

EPS = 1e-6
NEG = -1e30
GRID_W = 64
HEAD_DIM = 64
N_HEADS = 8
N_KV_HEADS = 2
HEADS_PER_KV = N_HEADS // N_KV_HEADS
BLOCK = 128
CHUNK = 128
N_GROUPS = 4
ROPE_THETA = 10000.0
SCALE = HEAD_DIM ** -0.5
LOG2E = math.log2(math.e)

LANES = 128
VMEM_LIMIT_BYTES = 60 * 1024 * 1024
N_MOD_ROWS = 24

SH1, SC1, GT1, SH2, SC2, GT2 = range(6)


def _cparams(sem):
    return pltpu.CompilerParams(dimension_semantics=sem, vmem_limit_bytes=VMEM_LIMIT_BYTES)


def _resident(block_shape, index_map):
    return pl.BlockSpec(block_shape, index_map, pipeline_mode=pl.Buffered(1))


def _ada_kernel(c_ref, w_ref, b_ref, o_ref):
    c = c_ref[...]
    s = (c * jax.nn.sigmoid(c)).astype(BF16)
    o_ref[...] = jnp.dot(s, w_ref[...].astype(BF16), preferred_element_type=F32) + b_ref[...]


def _ada_call(cs, w_ada, b_ada):
    depth, d, n = w_ada.shape
    tn = 1024
    return pl.pallas_call(
        _ada_kernel,
        grid=(depth, n // tn),
        in_specs=[
            pl.BlockSpec((N_MOD_ROWS, d), lambda l, j: (0, 0)),
            pl.BlockSpec((None, d, tn), lambda l, j: (l, 0, j)),
            pl.BlockSpec((None, 1, tn), lambda l, j: (l, 0, j)),
        ],
        out_specs=pl.BlockSpec((None, N_MOD_ROWS, tn), lambda l, j: (l, 0, j)),
        out_shape=jax.ShapeDtypeStruct((depth, N_MOD_ROWS, n), F32),
        compiler_params=_cparams(("parallel", "parallel")),
        name="ada_mod",
    )(cs, w_ada, b_ada.reshape(depth, 1, n))


def _mod_spec(layer, row, chunk, d):
    if row is None:
        return pl.BlockSpec((None, None, 1, d), lambda b, *_: (layer, b, 0, chunk))
    return pl.BlockSpec((None, None, 1, d), lambda b, *_: (layer, row, 0, chunk))


def _norm_modulate_rows(x_ref, g_ref, sc_ref, sh_ref, h_ref, tm, rb):
    def body(r, carry):
        rows = pl.ds(pl.multiple_of(r * rb, rb), rb)
        xr = x_ref[0, rows, :]
        ms = jnp.mean(xr * xr, axis=-1, keepdims=True)
        y = xr * lax.rsqrt(ms + EPS)
        h = (y * g_ref[...]) * (1.0 + sc_ref[...]) + sh_ref[...]
        h_ref[rows, :] = h.astype(h_ref.dtype)
        return carry

    lax.fori_loop(0, tm // rb, body, 0)


def _norm_in_kernel(x_ref, xn_ref, g_ref, sc_ref, sh_ref, scn_ref, shn_ref, w_ref, o_ref, ha_ref, hb_ref,
                    *, tm, tn):
    t = pl.program_id(0) * pl.num_programs(1) + pl.program_id(1)
    n = o_ref.shape[-1]

    @pl.when(t == 0)
    def _():
        _norm_modulate_rows(x_ref, g_ref, sc_ref, sh_ref, ha_ref, tm, min(tm, 64))

    def step(h_cur, h_next):
        rb = min(tm, 64)
        for r in range(tm // rb):
            rows = slice(r * rb, (r + 1) * rb)
            xr = xn_ref[0, rows, :]
            ms = jnp.mean(xr * xr, axis=-1, keepdims=True)
            h = ((xr * lax.rsqrt(ms + EPS)) * g_ref[...]) * (1.0 + scn_ref[...]) + shn_ref[...]
            h_next[rows, :] = h.astype(BF16)
        for j in range(n // tn):
            cols = slice(j * tn, (j + 1) * tn)
            o_ref[0, :, cols] = jnp.dot(h_cur[...], w_ref[:, cols],
                                        preferred_element_type=F32).astype(o_ref.dtype)

    parity = lax.rem(t, 2)

    @pl.when(parity == 0)
    def _():
        step(ha_ref, hb_ref)

    @pl.when(parity == 1)
    def _():
        step(hb_ref, ha_ref)


def _norm_in_call(x, mods, g, w, layer, w_layer, mod_row, tn):
    b, l, d = x.shape
    n = w.shape[-1]
    tm = min(512, l)
    nt = l // tm
    kern = functools.partial(_norm_in_kernel, tm=tm, tn=tn)

    def nxt(bi, i):
        t = jnp.minimum(bi * nt + i + 1, b * nt - 1)
        return t // nt, t % nt

    def mod_next(chunk):
        if mod_row is None:
            return pl.BlockSpec((None, None, 1, d), lambda bi, i: (layer, nxt(bi, i)[0], 0, chunk))
        return _mod_spec(layer, mod_row, chunk, d)

    return pl.pallas_call(
        kern,
        grid=(b, nt),
        in_specs=[
            pl.BlockSpec((1, tm, d), lambda bi, i: (bi, i, 0)),
            pl.BlockSpec((1, tm, d), lambda bi, i: nxt(bi, i) + (0,)),
            pl.BlockSpec((None, 1, d), lambda bi, i: (layer, 0, 0)),
            _mod_spec(layer, mod_row, SC1, d),
            _mod_spec(layer, mod_row, SH1, d),
            mod_next(SC1),
            mod_next(SH1),
            _resident((None, d, n), lambda bi, i: (w_layer, 0, 0)),
        ],
        out_specs=pl.BlockSpec((1, tm, n), lambda bi, i: (bi, i, 0)),
        out_shape=jax.ShapeDtypeStruct((b, l, n), BF16),
        scratch_shapes=[pltpu.VMEM((tm, d), BF16), pltpu.VMEM((tm, d), BF16)],
        compiler_params=_cparams(("arbitrary", "arbitrary")),
        name="norm_in",
    )(x, x, g, mods, mods, mods, mods, w)


def _conv_kernel(x_ref, gb_ref, gc_ref, w_ref, o_ref, *, seq_len):
    n_rows = x_ref.shape[1]
    pos = lax.broadcasted_iota(jnp.int32, (n_rows, LANES), 0) % seq_len
    first = pos == 0
    last = pos == seq_len - 1
    for s in range(x_ref.shape[2] // LANES):
        cols = slice(s * LANES, (s + 1) * LANES)
        z = gc_ref[0, :, cols].astype(F32) * x_ref[0, :, cols].astype(F32)
        z_prev = jnp.where(first, 0.0, pltpu.roll(z, 1, 0))
        z_next = jnp.where(last, 0.0, pltpu.roll(z, n_rows - 1, 0))
        y = z_prev * w_ref[0:1, cols] + z * w_ref[1:2, cols] + z_next * w_ref[2:3, cols]
        o_ref[0, :, cols] = (gb_ref[0, :, cols].astype(F32) * y).astype(o_ref.dtype)


def _conv_call(p, w_conv, layer, seq_len):
    b, l, _ = p.shape
    width = w_conv.shape[-1]
    blk = lambda j: pl.BlockSpec((1, l, width), lambda bi: (bi, 0, j))
    return pl.pallas_call(
        functools.partial(_conv_kernel, seq_len=seq_len),
        grid=(b,),
        in_specs=[blk(0), blk(1), blk(2),
                  pl.BlockSpec((None, 3, width), lambda bi: (layer, 0, 0))],
        out_specs=pl.BlockSpec((1, l, width), lambda bi: (bi, 0, 0)),
        out_shape=jax.ShapeDtypeStruct((b, l, width), BF16),
        compiler_params=_cparams(("parallel",)),
        name="short_conv",
    )(p, p, p, w_conv)


def _fourier_kernel(z_ref, ccsc_ref, cpsp_ref, o_ref, rhs_ref, *, out_scale):
    n_pos = z_ref.shape[1]
    cw = ccsc_ref.shape[0]
    for g in range(N_GROUPS):
        cols = slice(g * cw, (g + 1) * cw)
        ab = jnp.dot(z_ref[0, :, cols], ccsc_ref[...], preferred_element_type=F32)
        rhs_ref[0:n_pos, cols] = ab[:, :cw].astype(BF16)
        rhs_ref[n_pos:2 * n_pos, cols] = ab[:, cw:].astype(BF16)
    out = jnp.dot(cpsp_ref[...], rhs_ref[...], preferred_element_type=F32)
    o_ref[0] = (out * out_scale).astype(o_ref.dtype)


def _fourier_call(p, ccsc, cpsp, col_block):
    b, l, _ = p.shape
    cw = ccsc.shape[0]
    width = N_GROUPS * cw
    kern = functools.partial(_fourier_kernel, out_scale=1.0 / math.sqrt(l * cw))
    return pl.pallas_call(
        kern,
        grid=(b,),
        in_specs=[
            pl.BlockSpec((1, l, width), lambda bi: (bi, 0, col_block)),
            _resident((cw, 2 * cw), lambda bi: (0, 0)),
            _resident((l, 2 * l), lambda bi: (0, 0)),
        ],
        out_specs=pl.BlockSpec((1, l, width), lambda bi: (bi, 0, 0)),
        out_shape=jax.ShapeDtypeStruct((b, l, width), BF16),
        scratch_shapes=[pltpu.VMEM((2 * l, width), BF16)],
        compiler_params=_cparams(("parallel",)),
        name="fourier_mix",
    )(p, ccsc, cpsp)


def _dft_tables(n):
    k = jnp.arange(n, dtype=jnp.int32)
    ang = ((k[:, None] * k[None, :]) % n).astype(F32) * (2.0 * math.pi / n)
    return jnp.cos(ang), jnp.sin(ang)


FFT_RADIX = 8


def _fourier_fft_kernel(z_ref, ccsc_ref, fsub_ref, twr_ref, twi_ref, o_ref, w_ref, g_ref, *, out_scale):
    n_pos = z_ref.shape[1]
    m = n_pos // FFT_RADIX
    cw = ccsc_ref.shape[0]
    c = math.sqrt(0.5)
    rb = 64
    gpp = 2
    for gp in range(N_GROUPS // gpp):
        groups = range(gp * gpp, (gp + 1) * gpp)
        pcols = slice(gp * gpp * cw, (gp + 1) * gpp * cw)
        for g in groups:
            ab = jnp.dot(z_ref[0, :, g * cw:(g + 1) * cw], ccsc_ref[...], preferred_element_type=F32)
            w_ref[g] = ab[:, :cw]
            w_ref[N_GROUPS + g] = ab[:, cw:]
        for n2 in range(FFT_RADIX):
            sub = pl.ds(n2, m, stride=FFT_RADIX)
            rhs = jnp.concatenate(
                [jnp.concatenate([w_ref[part * N_GROUPS + g, sub, :] for g in groups], axis=1)
                 for part in range(2)], axis=0).astype(BF16)
            g_ref[n2, :, pcols] = jnp.dot(fsub_ref[...], rhs, preferred_element_type=F32)
        for r in range(m // rb):
            rows = slice(r * rb, (r + 1) * rb)
            irows = slice(m + r * rb, m + (r + 1) * rb)
            for g in groups:
                cols = slice(g * cw, (g + 1) * cw)
                hr, hi = [g_ref[0, rows, cols]], [g_ref[0, irows, cols]]
                for n2 in range(1, FFT_RADIX):
                    gr, gi = g_ref[n2, rows, cols], g_ref[n2, irows, cols]
                    tr, ti = twr_ref[n2, rows, :], twi_ref[n2, rows, :]
                    hr.append(tr * gr - ti * gi)
                    hi.append(tr * gi + ti * gr)
                a0, a1 = hr[0] + hr[4], hr[0] - hr[4]
                a2 = hr[2] + hr[6]
                b0, b1 = hr[1] + hr[5], hr[1] - hr[5]
                b2, b3 = hr[3] + hr[7], hr[3] - hr[7]
                d1, d2, d3 = hi[1] - hi[5], hi[2] - hi[6], hi[3] - hi[7]
                e0, e1 = a0 + a2, b0 + b2
                odd_r = c * (b1 - b3)
                odd_i = c * (d1 + d3)
                p1, p3 = a1 + odd_r, a1 - odd_r
                p2 = a0 - a2
                q1, q3 = odd_i + d2, odd_i - d2
                q2 = (hi[1] + hi[5]) - (hi[3] + hi[7])
                ys = (e0 + e1, p1 + q1, p2 + q2, p3 + q3, e0 - e1, p3 - q3, p2 - q2, p1 - q1)
                for k2, y in enumerate(ys):
                    orow = slice(k2 * m + r * rb, k2 * m + (r + 1) * rb)
                    o_ref[0, orow, cols] = (y * out_scale).astype(o_ref.dtype)


def _fourier_fft_call(p, ccsc_neg, fsub, twr, twi, col_block):
    b, l, _ = p.shape
    cw = ccsc_neg.shape[0]
    width = N_GROUPS * cw
    m = l // FFT_RADIX
    kern = functools.partial(_fourier_fft_kernel, out_scale=1.0 / math.sqrt(l * cw))
    return pl.pallas_call(
        kern,
        grid=(b,),
        in_specs=[
            pl.BlockSpec((1, l, width), lambda bi: (bi, 0, col_block)),
            _resident((cw, 2 * cw), lambda bi: (0, 0)),
            _resident((2 * m, 2 * m), lambda bi: (0, 0)),
            _resident((FFT_RADIX, m, LANES), lambda bi: (0, 0, 0)),
            _resident((FFT_RADIX, m, LANES), lambda bi: (0, 0, 0)),
        ],
        out_specs=pl.BlockSpec((1, l, width), lambda bi: (bi, 0, 0)),
        out_shape=jax.ShapeDtypeStruct((b, l, width), BF16),
        scratch_shapes=[pltpu.VMEM((2 * N_GROUPS, l, cw), F32),
                        pltpu.VMEM((FFT_RADIX, 2 * m, width), F32)],
        compiler_params=_cparams(("parallel",)),
        name="fourier_fft",
    )(p, ccsc_neg, fsub, twr, twi)


def _fft_tables(n_pos):
    m = n_pos // FFT_RADIX
    cos_m, sin_m = _dft_tables(m)
    fsub = jnp.concatenate([jnp.concatenate([cos_m, sin_m], axis=1),
                            jnp.concatenate([-sin_m, cos_m], axis=1)], axis=0).astype(BF16)
    n2 = jnp.arange(FFT_RADIX, dtype=jnp.int32)[:, None]
    k1 = jnp.arange(m, dtype=jnp.int32)[None, :]
    ang = ((n2 * k1) % n_pos).astype(F32) * (2.0 * math.pi / n_pos)
    bcast = lambda t: jnp.broadcast_to(t[:, :, None], (FFT_RADIX, m, LANES))
    return fsub, bcast(jnp.cos(ang)), bcast(-jnp.sin(ang))


def _nt_dot(a, b):
    return lax.dot_general(a, b, (((1,), (1,)), ((), ())), preferred_element_type=F32)


def _latent_attn_kernel(sink_ref, q_ref, k_ref, v_ref, kc_ref, vc_ref, cos_ref, sin_ref, o_ref,
                        qlo_ref, qhi_ref, kpad_ref, kswp_ref, vt_ref, kcs_ref, vct_ref, *, layer):
    n_pos = q_ref.shape[1]
    n_blk = n_pos // BLOCK
    n_ctx = kc_ref.shape[1]
    half = LANES // 2

    lane = lax.broadcasted_iota(jnp.int32, (BLOCK, LANES), 1)
    low16 = (lane % 32) < 16
    lo_half = lane < half

    def prep_block(r, carry):
        rows = pl.ds(pl.multiple_of(r * BLOCK, BLOCK), BLOCK)
        cos = cos_ref[rows, :]
        sin = sin_ref[rows, :]

        def rope(t):
            swapped = jnp.where(low16, pltpu.roll(t, LANES - 16, 1), pltpu.roll(t, 16, 1))
            return t * cos + swapped * sin

        for s in range(q_ref.shape[-1] // LANES):
            cols = slice(s * LANES, (s + 1) * LANES)
            t = rope(q_ref[0, rows, cols].astype(F32)) * (SCALE * LOG2E)
            qlo_ref[rows, cols] = jnp.where(lo_half, t, 0.0).astype(BF16)
            qhi_ref[rows, cols] = jnp.where(lo_half, 0.0, t).astype(BF16)
        prow = pl.ds(pl.multiple_of(r * BLOCK + BLOCK, BLOCK), BLOCK)
        tk = rope(k_ref[0, rows, :].astype(F32))
        kpad_ref[prow, :] = tk.astype(BF16)
        kswp_ref[prow, :] = pltpu.roll(tk, half, 1).astype(BF16)
        vt_ref[r + 1] = v_ref[0, rows, :].astype(F32).T.astype(BF16)
        return carry

    zeros = jnp.zeros((BLOCK, LANES), BF16)
    for ref in (kpad_ref, kswp_ref):
        ref[0:BLOCK, :] = zeros
        ref[n_pos + BLOCK:n_pos + 2 * BLOCK, :] = zeros
    vt_ref[0] = zeros
    vt_ref[n_blk + 1] = zeros
    kcs_ref[...] = pltpu.roll(kc_ref[0].astype(F32), half, 1).astype(BF16)
    for cb in range(n_ctx // BLOCK):
        crow = slice(cb * BLOCK, (cb + 1) * BLOCK)
        vct_ref[:, crow] = vc_ref[0, crow, :].astype(F32).T.astype(BF16)
    lax.fori_loop(0, n_blk, prep_block, 0)

    head_order = (0, 2, 5, 7, 1, 3, 4, 6)
    col_of = {h: i for i, h in enumerate(head_order)}
    n_col = N_HEADS * BLOCK
    kj = lax.broadcasted_iota(jnp.int32, (3 * BLOCK, BLOCK), 0)
    qi = lax.broadcasted_iota(jnp.int32, (3 * BLOCK, BLOCK), 1)
    in_window = (kj >= qi) & (kj <= qi + 2 * BLOCK)
    sk = jnp.concatenate([jnp.full((1, BLOCK), sink_ref[layer, h] * LOG2E, F32) for h in head_order], axis=1)
    neg = NEG * LOG2E

    def slab(ref, rows, s):
        return ref[rows, s * LANES:(s + 1) * LANES]

    def attn_block(n, carry):
        rows = pl.ds(pl.multiple_of(n * BLOCK, BLOCK), BLOCK)
        band = pl.ds(pl.multiple_of(n * BLOCK, BLOCK), 3 * BLOCK)
        valid = in_window & (kj >= (1 - n) * BLOCK) & (kj < n_pos - (n - 1) * BLOCK)

        def masked(s):
            return jnp.concatenate([jnp.where(valid, s[:, i * BLOCK:(i + 1) * BLOCK], neg)
                                    for i in range(s.shape[1] // BLOCK)], axis=1)

        rhs_nat = jnp.concatenate([slab(qlo_ref, rows, 0), slab(qlo_ref, rows, 1),
                                   slab(qhi_ref, rows, 2), slab(qhi_ref, rows, 3)], axis=0)
        rhs_swp = jnp.concatenate([slab(qhi_ref, rows, 0), slab(qhi_ref, rows, 1),
                                   slab(qlo_ref, rows, 2), slab(qlo_ref, rows, 3)], axis=0)
        s_loc = jnp.concatenate([masked(_nt_dot(kpad_ref[band, :], rhs_nat)),
                                 masked(_nt_dot(kswp_ref[band, :], rhs_swp))], axis=1)
        s_ctx = jnp.concatenate([_nt_dot(kc_ref[0], rhs_nat),
                                 _nt_dot(kcs_ref[...], rhs_swp)], axis=1)
        m = jnp.maximum(jnp.maximum(jnp.max(s_loc, axis=0, keepdims=True),
                                    jnp.max(s_ctx, axis=0, keepdims=True)), sk)
        e_loc = jnp.exp2(s_loc - m)
        e_ctx = jnp.exp2(s_ctx - m)
        inv_den = 1.0 / (jnp.sum(e_loc, axis=0, keepdims=True) + jnp.sum(e_ctx, axis=0, keepdims=True)
                         + jnp.exp2(sk - m))
        e_loc = e_loc.astype(BF16)
        e_ctx = e_ctx.astype(BF16)
        vt_band = jnp.concatenate([vt_ref[n], vt_ref[n + 1], vt_ref[n + 2]], axis=1)
        o_t = {}
        for g in range(N_KV_HEADS):
            heads = [4 * g + j for j in range(HEADS_PER_KV)]
            pick = lambda arr: jnp.concatenate(
                [arr[:, col_of[h] * BLOCK:(col_of[h] + 1) * BLOCK] for h in heads], axis=1)
            vrows = slice(g * HEAD_DIM, (g + 1) * HEAD_DIM)
            o = (jnp.dot(vt_band[vrows], pick(e_loc), preferred_element_type=F32)
                 + jnp.dot(vct_ref[vrows, :], pick(e_ctx), preferred_element_type=F32)) * pick(inv_den)
            for j, h in enumerate(heads):
                o_t[h] = o[:, j * BLOCK:(j + 1) * BLOCK]
        slabs = [jnp.concatenate([o_t[2 * s], o_t[2 * s + 1]], axis=0).T for s in range(N_HEADS // 2)]
        o_ref[0, rows, :] = jnp.concatenate(slabs, axis=1).astype(o_ref.dtype)
        return carry

    lax.fori_loop(0, n_blk, attn_block, 0, unroll=4)


def _latent_attn_call(p, pc, kc_block, sink, cos_t, sin_t, layer):
    b, l, _ = p.shape
    lc = pc.shape[1]
    qw = N_HEADS * HEAD_DIM
    kvw = N_KV_HEADS * HEAD_DIM
    q_block = 2048 // qw
    k_block = 2560 // kvw
    kern = functools.partial(_latent_attn_kernel, layer=layer)
    return pl.pallas_call(
        kern,
        grid=(b,),
        in_specs=[
            pl.BlockSpec(memory_space=pltpu.SMEM),
            pl.BlockSpec((1, l, qw), lambda bi: (bi, 0, q_block)),
            pl.BlockSpec((1, l, kvw), lambda bi: (bi, 0, k_block)),
            pl.BlockSpec((1, l, kvw), lambda bi: (bi, 0, k_block + 1)),
            pl.BlockSpec((1, lc, kvw), lambda bi: (bi, 0, kc_block)),
            pl.BlockSpec((1, lc, kvw), lambda bi: (bi, 0, kc_block + 1)),
            _resident((l, LANES), lambda bi: (0, 0)),
            _resident((l, LANES), lambda bi: (0, 0)),
        ],
        out_specs=pl.BlockSpec((1, l, qw), lambda bi: (bi, 0, 0)),
        out_shape=jax.ShapeDtypeStruct((b, l, qw), BF16),
        scratch_shapes=[pltpu.VMEM((l, qw), BF16),
                        pltpu.VMEM((l, qw), BF16),
                        pltpu.VMEM((l + 2 * BLOCK, kvw), BF16),
                        pltpu.VMEM((l + 2 * BLOCK, kvw), BF16),
                        pltpu.VMEM((l // BLOCK + 2, kvw, BLOCK), BF16),
                        pltpu.VMEM((lc, kvw), BF16),
                        pltpu.VMEM((kvw, lc), BF16)],
        compiler_params=_cparams(("parallel",)),
        name="latent_attention",
    )(sink, p, p, p, pc, pc, cos_t, sin_t)


def _ctx_attn_kernel(sink_ref, q_ref, kc_ref, vc_ref, o_ref, *, layer):
    outs = []
    for h in range(N_HEADS):
        kv = h // HEADS_PER_KV
        hc = slice(kv * HEAD_DIM, (kv + 1) * HEAD_DIM)
        qh = q_ref[0, :, h * HEAD_DIM:(h + 1) * HEAD_DIM] * SCALE
        s = _nt_dot(qh, kc_ref[0, :, hc])
        sk = sink_ref[layer, h]
        m = jnp.maximum(jnp.max(s, axis=-1, keepdims=True), sk)
        e = jnp.exp(s - m)
        den = jnp.sum(e, axis=-1, keepdims=True) + jnp.exp(sk - m)
        o = jnp.dot(e.astype(BF16), vc_ref[0, :, hc], preferred_element_type=F32)
        outs.append(o / den)
    o_ref[0] = jnp.concatenate(outs, axis=-1).astype(o_ref.dtype)


def _ctx_attn_call(pc, sink, layer):
    b, lc, _ = pc.shape
    qw = N_HEADS * HEAD_DIM
    kvw = N_KV_HEADS * HEAD_DIM
    kern = functools.partial(_ctx_attn_kernel, layer=layer)
    return pl.pallas_call(
        kern,
        grid=(b,),
        in_specs=[
            pl.BlockSpec(memory_space=pltpu.SMEM),
            pl.BlockSpec((1, lc, qw), lambda bi: (bi, 0, 2048 // qw)),
            pl.BlockSpec((1, lc, kvw), lambda bi: (bi, 0, 2560 // kvw)),
            pl.BlockSpec((1, lc, kvw), lambda bi: (bi, 0, 2560 // kvw + 1)),
        ],
        out_specs=pl.BlockSpec((1, lc, qw), lambda bi: (bi, 0, 0)),
        out_shape=jax.ShapeDtypeStruct((b, lc, qw), BF16),
        compiler_params=_cparams(("parallel",)),
        name="context_attention",
    )(sink, pc, pc, pc)


def _rope_tables(n_pos):
    quarter = HEAD_DIM // 4
    inv = ROPE_THETA ** (-jnp.arange(quarter, dtype=F32) / quarter)
    pos = jnp.arange(n_pos, dtype=jnp.int32)
    row = (pos // GRID_W).astype(F32)
    col = (pos % GRID_W).astype(F32)
    a_row = row[:, None] * inv[None, :]
    a_col = col[:, None] * inv[None, :]
    ang = jnp.concatenate([a_row, a_row, a_col, a_col], axis=-1)
    sign = jnp.tile(jnp.concatenate([-jnp.ones((quarter,), F32), jnp.ones((quarter,), F32)]), 2)
    reps = LANES // HEAD_DIM
    return jnp.tile(jnp.cos(ang), (1, reps)), jnp.tile(jnp.sin(ang) * sign, (1, reps))


def _gelu_tanh(x):
    k1 = -2.0 * math.sqrt(2.0 / math.pi) * LOG2E
    t = x * (k1 + (k1 * 0.044715) * (x * x))
    return x * (1.0 / (1.0 + jnp.exp2(t)))


def _chunk_mlp_kernel(u_ref, v_ref, w_ref, bias_ref, o_ref):
    n_rows = u_ref.shape[1]
    averager = jnp.full((LANES, LANES), 1.0 / LANES, BF16)
    for gi in range(u_ref.shape[2] // LANES):
        cols = slice(gi * LANES, (gi + 1) * LANES)
        u = _gelu_tanh(u_ref[0, :, cols].astype(F32))
        v = _gelu_tanh(v_ref[0, :, cols].astype(F32))
        mu = jnp.dot(v.astype(BF16), averager, preferred_element_type=F32)
        dlt = v - mu
        var = jnp.dot((dlt * dlt).astype(BF16), averager, preferred_element_type=F32)
        vn = (dlt * lax.rsqrt(var + EPS)).astype(BF16)
        w = w_ref[gi].astype(BF16)
        for n in range(n_rows // CHUNK):
            rows = slice(n * CHUNK, (n + 1) * CHUNK)
            mixed = jnp.dot(w, vn[rows], preferred_element_type=F32) + bias_ref[gi]
            o_ref[0, rows, cols] = (u[rows] * mixed).astype(o_ref.dtype)


def _chunk_mlp_call(p, w_s, bias, layer):
    b, l, _ = p.shape
    gpb = 2
    bw = gpb * LANES
    u_block = 2816 // bw
    v_block = 3328 // bw
    return pl.pallas_call(
        _chunk_mlp_kernel,
        grid=(b, N_GROUPS // gpb),
        in_specs=[
            pl.BlockSpec((1, l, bw), lambda bi, j: (bi, 0, u_block + j)),
            pl.BlockSpec((1, l, bw), lambda bi, j: (bi, 0, v_block + j)),
            pl.BlockSpec((None, gpb, CHUNK, CHUNK), lambda bi, j: (layer, j, 0, 0)),
            pl.BlockSpec((None, gpb, CHUNK, LANES), lambda bi, j: (layer, j, 0, 0)),
        ],
        out_specs=pl.BlockSpec((1, l, bw), lambda bi, j: (bi, 0, j)),
        out_shape=jax.ShapeDtypeStruct((b, l, N_GROUPS * LANES), BF16),
        compiler_params=_cparams(("parallel", "parallel")),
        name="chunk_mlp",
    )(p, p, w_s, bias)


def _out_proj_kernel(ya_ref, yb_ref, yc_ref, yd_ref, x_ref, gt_ref, w_ref, o_ref, *, tn):
    d = o_ref.shape[-1]
    kw = ya_ref.shape[-1]
    ys = (ya_ref, yb_ref, yc_ref, yd_ref)
    for j in range(d // tn):
        cols = slice(j * tn, (j + 1) * tn)
        acc = jnp.dot(ys[0][0], w_ref[0:kw, cols], preferred_element_type=F32)
        for k in range(1, 4):
            acc = acc + jnp.dot(ys[k][0], w_ref[k * kw:(k + 1) * kw, cols], preferred_element_type=F32)
        o_ref[0, :, cols] = x_ref[0, :, cols] + gt_ref[:, cols] * acc


def _out_proj_call(ys, x, mods, w_out, layer, mod_row):
    b, l, d = x.shape
    kw = ys[0].shape[-1]
    tm = min(1024, l)
    kern = functools.partial(_out_proj_kernel, tn=512)
    yspec = pl.BlockSpec((1, tm, kw), lambda bi, i: (bi, i, 0))
    return pl.pallas_call(
        kern,
        grid=(b, l // tm),
        in_specs=[yspec, yspec, yspec, yspec,
                  pl.BlockSpec((1, tm, d), lambda bi, i: (bi, i, 0)),
                  _mod_spec(layer, mod_row, GT1, d),
                  _resident((None, d, d), lambda bi, i: (layer, 0, 0))],
        out_specs=pl.BlockSpec((1, tm, d), lambda bi, i: (bi, i, 0)),
        out_shape=jax.ShapeDtypeStruct((b, l, d), F32),
        compiler_params=_cparams(("parallel", "parallel")),
        name="out_proj",
    )(*ys, x, mods, w_out)


def _swiglu_kernel(x_ref, g_ref, sc_ref, sh_ref, gt_ref, wg_hbm, wu_hbm, wd_hbm, gf_ref, o_ref,
                   h_ref, wg_buf, wu_buf, wd_buf, sem, *, tm, tf, layer, final_norm):
    n_f = wg_hbm.shape[-1] // tf
    tile = pl.program_id(0) * pl.num_programs(1) + pl.program_id(1)
    n_tiles = pl.num_programs(0) * pl.num_programs(1)
    first_chunk = tile * n_f

    def chunk_copies(f, slot):
        cols = pl.ds(pl.multiple_of(f * tf, tf), tf)
        return (pltpu.make_async_copy(wg_hbm.at[layer, :, cols], wg_buf.at[slot], sem.at[0, slot]),
                pltpu.make_async_copy(wu_hbm.at[layer, :, cols], wu_buf.at[slot], sem.at[1, slot]),
                pltpu.make_async_copy(wd_hbm.at[layer, cols, :], wd_buf.at[slot], sem.at[2, slot]))

    def start_chunk(f, slot):
        for cp in chunk_copies(f, slot):
            cp.start()

    def wait_chunk(f, slot):
        for cp in chunk_copies(f, slot):
            cp.wait()

    def gated_ffn(h, slot):
        gate = jnp.dot(h, wg_buf[slot], preferred_element_type=F32)
        up = jnp.dot(h, wu_buf[slot], preferred_element_type=F32)
        a = (gate * jax.nn.sigmoid(gate) * up).astype(BF16)
        return gt_ref[...] * jnp.dot(a, wd_buf[slot], preferred_element_type=F32)

    @pl.when(tile == 0)
    def _():
        start_chunk(0, 0)

    slot0 = lax.rem(first_chunk, 2)
    wait_chunk(0, slot0)
    start_chunk(1, 1 - slot0)
    rb = min(tm, 256)
    for r in range(tm // rb):
        rows = slice(r * rb, (r + 1) * rb)
        xr = x_ref[0, rows, :]
        ms = jnp.mean(xr * xr, axis=-1, keepdims=True)
        h = (((xr * lax.rsqrt(ms + EPS)) * g_ref[...]) * (1.0 + sc_ref[...]) + sh_ref[...]).astype(BF16)
        h_ref[rows, :] = h
        o_ref[0, rows, :] = xr + gated_ffn(h, slot0)

    def chunk_step(f, carry):
        slot = lax.rem(first_chunk + f, 2)
        wait_chunk(f, slot)

        @pl.when(jnp.logical_or(f + 1 < n_f, tile + 1 < n_tiles))
        def _():
            start_chunk(lax.rem(f + 1, n_f), 1 - slot)

        o_ref[0] += gated_ffn(h_ref[...], slot)
        return carry

    lax.fori_loop(1, n_f, chunk_step, 0)

    if final_norm:
        def body(r, carry):
            rows = pl.ds(pl.multiple_of(r * 64, 64), 64)
            xr = o_ref[0, rows, :]
            ms = jnp.mean(xr * xr, axis=-1, keepdims=True)
            o_ref[0, rows, :] = (xr * lax.rsqrt(ms + EPS)) * gf_ref[...]
            return carry

        lax.fori_loop(0, tm // 64, body, 0)


def _swiglu_call(x, mods, g, w_gate, w_up, w_down, g_final, layer, mod_row, final_norm):
    b, l, d = x.shape
    ff = w_gate.shape[-1]
    tm = min(1024, l)
    tf = 512
    assert ff % tf == 0 and ff // tf >= 2
    kern = functools.partial(_swiglu_kernel, tm=tm, tf=tf, layer=layer, final_norm=final_norm)
    hbm = pl.BlockSpec(memory_space=pl.ANY)
    return pl.pallas_call(
        kern,
        grid=(b, l // tm),
        in_specs=[
            pl.BlockSpec((1, tm, d), lambda bi, i: (bi, i, 0)),
            pl.BlockSpec((None, 1, d), lambda bi, i: (layer, 0, 0)),
            _mod_spec(layer, mod_row, SC2, d),
            _mod_spec(layer, mod_row, SH2, d),
            _mod_spec(layer, mod_row, GT2, d),
            hbm, hbm, hbm,
            pl.BlockSpec((1, d), lambda bi, i: (0, 0)),
        ],
        out_specs=pl.BlockSpec((1, tm, d), lambda bi, i: (bi, i, 0)),
        out_shape=jax.ShapeDtypeStruct((b, l, d), F32),
        scratch_shapes=[pltpu.VMEM((tm, d), BF16),
                        pltpu.VMEM((2, d, tf), BF16),
                        pltpu.VMEM((2, d, tf), BF16),
                        pltpu.VMEM((2, tf, d), BF16),
                        pltpu.SemaphoreType.DMA((3, 2))],
        compiler_params=_cparams(("arbitrary", "arbitrary")),
        name="swiglu",
    )(x, g, mods, mods, mods, w_gate, w_up, w_down, g_final)


def kernel(x, c, ctx, c_ctx, w_ada, b_ada, g_norm1, w_in, w_conv, sink, w_s, b_s, w_out, g_norm2,
           w_gate, w_up, w_down, g_final):
    batch, n_pos, d = x.shape
    n_ctx = ctx.shape[1]
    depth = w_ada.shape[0]
    cw = (d // 4) // N_GROUPS
    ctx_row = batch

    w_in_b = w_in.astype(BF16)
    w_out_b = w_out.astype(BF16)
    w_gate_b = w_gate.astype(BF16)
    w_up_b = w_up.astype(BF16)
    w_down_b = w_down.astype(BF16)
    g1 = g_norm1.reshape(depth, 1, d)
    g2 = g_norm2.reshape(depth, 1, d)
    gf = g_final.reshape(1, d)

    cs = jnp.concatenate([c, c_ctx[None, :], jnp.zeros((N_MOD_ROWS - batch - 1, d), F32)], axis=0)
    mods = _ada_call(cs, w_ada, b_ada).reshape(depth, N_MOD_ROWS, 1, 6 * d)

    cos_c, sin_c = _dft_tables(cw)
    ccsc = jnp.concatenate([cos_c, sin_c], axis=1).astype(BF16)
    use_fft = n_pos % (FFT_RADIX * 64) == 0 and n_pos // FFT_RADIX >= LANES
    if use_fft:
        ccsc_neg = jnp.concatenate([cos_c, -sin_c], axis=1).astype(BF16)
        fsub, twr, twi = _fft_tables(n_pos)
        latent_fourier = lambda pp: _fourier_fft_call(pp, ccsc_neg, fsub, twr, twi, f_block)
    else:
        cos_p, sin_p = _dft_tables(n_pos)
        cpsp = jnp.concatenate([cos_p, -sin_p], axis=1).astype(BF16)
        latent_fourier = lambda pp: _fourier_call(pp, ccsc, cpsp, f_block)
    cos_x, sin_x = _dft_tables(n_ctx)
    cpsp_ctx = jnp.concatenate([cos_x, -sin_x], axis=1).astype(BF16)
    rope_cos, rope_sin = _rope_tables(n_pos)
    mlp_bias = jnp.broadcast_to(b_s[:, :, :, None], b_s.shape + (cw,))
    f_block = 1536 // (N_GROUPS * cw)

    def flat(a):
        return a.reshape(1, -1, a.shape[-1])

    def unflat(a):
        return a.reshape(batch, n_ctx, a.shape[-1])

    xc = flat(ctx)
    for layer in range(depth):
        last = layer == depth - 1
        if last:
            w_kv = w_in_b[layer:layer + 1, :, 2560:2816]
            pc = unflat(_norm_in_call(xc, mods, g1, w_kv, layer, 0, ctx_row, tn=256))
            kc_block = 0
        else:
            pc_flat = _norm_in_call(xc, mods, g1, w_in_b, layer, layer, ctx_row, tn=768)
            pc = unflat(pc_flat)
            kc_block = 2560 // (N_KV_HEADS * HEAD_DIM)
            ys_c = (_conv_call(pc_flat, w_conv, layer, n_ctx),
                    flat(_fourier_call(pc, ccsc, cpsp_ctx, f_block)),
                    flat(_ctx_attn_call(pc, sink, layer)),
                    _chunk_mlp_call(pc_flat, w_s, mlp_bias, layer))
            xc_new = _out_proj_call(ys_c, xc, mods, w_out_b, layer, ctx_row)
            xc_next = _swiglu_call(xc_new, mods, g2, w_gate_b, w_up_b, w_down_b, gf, layer, ctx_row, False)
        p = _norm_in_call(x, mods, g1, w_in_b, layer, layer, None, tn=768)
        ys = (_conv_call(p, w_conv, layer, n_pos),
              latent_fourier(p),
              _latent_attn_call(p, pc, kc_block, sink, rope_cos, rope_sin, layer),
              _chunk_mlp_call(p, w_s, mlp_bias, layer))
        x = _out_proj_call(ys, x, mods, w_out_b, layer, None)
        x = _swiglu_call(x, mods, g2, w_gate_b, w_up_b, w_down_b, gf, layer, None, last)
        if not last:
            xc = xc_next
    return x
```

```python
import functools
import math

import jax
import jax.numpy as jnp
from jax import lax
from jax.experimental import pallas as pl
from jax.experimental.pallas import tpu as pltpu

F32 = jnp.float32
BF16 = jnp.bfloat16

EPS = 1e-6
NEG = -1e30
GRID_W = 64
HEAD_DIM = 64
N_HEADS = 8
N_KV_HEADS = 2
HEADS_PER_KV = N_HEADS // N_KV_HEADS
BLOCK = 128
CHUNK = 128
N_GROUPS = 4
ROPE_THETA = 10000.0
SCALE = HEAD_DIM ** -0.5
LOG2E = math.log2(math.e)

LANES = 128
VMEM_LIMIT_BYTES = 60 * 1024 * 1024
N_MOD_ROWS = 24

SH1, SC1, GT1, SH2, SC2, GT2 = range(6)


def _cparams(sem):
    return pltpu.CompilerParams(dimension_semantics=sem, vmem_limit_bytes=VMEM_LIMIT_BYTES)


def _resident(block_shape, index_map):
    return pl.BlockSpec(block_shape, index_map, pipeline_mode=pl.Buffered(1))


def _ada_kernel(c_ref, w_ref, b_ref, o_ref):
    c = c_ref[...]
    s = (c * jax.nn.sigmoid(c)).astype(BF16)
    o_ref[...] = jnp.dot(s, w_ref[...].astype(BF16), preferred_element_type=F32) + b_ref[...]


def _ada_call(cs, w_ada, b_ada):
    depth, d, n = w_ada.shape
    tn = 1024
    return pl.pallas_call(
        _ada_kernel,
        grid=(depth, n // tn),
        in_specs=[
            pl.BlockSpec((N_MOD_ROWS, d), lambda l, j: (0, 0)),
            pl.BlockSpec((None, d, tn), lambda l, j: (l, 0, j)),
            pl.BlockSpec((None, 1, tn), lambda l, j: (l, 0, j)),
        ],
        out_specs=pl.BlockSpec((None, N_MOD_ROWS, tn), lambda l, j: (l, 0, j)),
        out_shape=jax.ShapeDtypeStruct((depth, N_MOD_ROWS, n), F32),
        compiler_params=_cparams(("parallel", "parallel")),
        name="ada_mod",
    )(cs, w_ada, b_ada.reshape(depth, 1, n))


def _mod_spec(layer, row, chunk, d):
    if row is None:
        return pl.BlockSpec((None, None, 1, d), lambda b, *_: (layer, b, 0, chunk))
    return pl.BlockSpec((None, None, 1, d), lambda b, *_: (layer, row, 0, chunk))


def _norm_modulate_rows(x_ref, g_ref, sc_ref, sh_ref, h_ref, tm, rb):
    def body(r, carry):
        rows = pl.ds(pl.multiple_of(r * rb, rb), rb)
        xr = x_ref[0, rows, :]
        ms = jnp.mean(xr * xr, axis=-1, keepdims=True)
        y = xr * lax.rsqrt(ms + EPS)
        h = (y * g_ref[...]) * (1.0 + sc_ref[...]) + sh_ref[...]
        h_ref[rows, :] = h.astype(h_ref.dtype)
        return carry

    lax.fori_loop(0, tm // rb, body, 0)


def _norm_in_kernel(x_ref, xn_ref, g_ref, sc_ref, sh_ref, scn_ref, shn_ref, w_ref, o_ref, ha_ref, hb_ref,
                    *, tm, tn):
    t = pl.program_id(0) * pl.num_programs(1) + pl.program_id(1)
    n = o_ref.shape[-1]

    @pl.when(t == 0)
    def _():
        _norm_modulate_rows(x_ref, g_ref, sc_ref, sh_ref, ha_ref, tm, min(tm, 64))

    def step(h_cur, h_next):
        rb = min(tm, 32)
        n_rb = tm // rb
        ct = 256 if tn % 256 == 0 else tn
        n_ct = n // ct
        anchors = [None] * n_ct
        for r in range(n_rb):
            rows = slice(r * rb, (r + 1) * rb)
            xr = xn_ref[0, rows, :]
            ms = jnp.mean(xr * xr, axis=-1, keepdims=True)
            h = ((xr * lax.rsqrt(ms + EPS)) * g_ref[...]) * (1.0 + scn_ref[...]) + shn_ref[...]
            hb = h.astype(BF16)
            h_next[rows, :] = hb
            folded = hb[:, 0:LANES]
            for k in range(1, hb.shape[1] // LANES):
                folded = folded + hb[:, k * LANES:(k + 1) * LANES]
            while folded.shape[0] > 16:
                half_rows = folded.shape[0] // 2
                folded = folded[:half_rows] + folded[half_rows:]
            word = pltpu.bitcast(folded, jnp.uint32)
            zero = lax.shift_right_logical(lax.shift_right_logical(word, jnp.uint32(16)), jnp.uint32(16))
            j = r * n_ct // n_rb
            anchors[j] = zero if anchors[j] is None else anchors[j] | zero
        for j in range(n // tn):
            cols = slice(j * tn, (j + 1) * tn)
            out = jnp.dot(h_cur[...], w_ref[:, cols], preferred_element_type=F32)
            pieces = []
            for s in range(tn // ct):
                piece = out[:, s * ct:(s + 1) * ct]
                anchor = anchors[j * (tn // ct) + s]
                if anchor is not None and ct % LANES == 0:
                    zrow = anchor[0:1, :].astype(F32)
                    piece = piece + jnp.concatenate([zrow] * (ct // LANES), axis=1)
                pieces.append(piece)
            o_ref[0, :, cols] = jnp.concatenate(pieces, axis=1).astype(o_ref.dtype)

    parity = lax.rem(t, 2)

    @pl.when(parity == 0)
    def _():
        step(ha_ref, hb_ref)

    @pl.when(parity == 1)
    def _():
        step(hb_ref, ha_ref)


def _norm_in_call(x, mods, g, w, layer, w_layer, mod_row, tn):
    b, l, d = x.shape
    n = w.shape[-1]
    tm = min(512, l)
    nt = l // tm
    kern = functools.partial(_norm_in_kernel, tm=tm, tn=tn)

    def nxt(bi, i):
        t = jnp.minimum(bi * nt + i + 1, b * nt - 1)
        return t // nt, t % nt

    def mod_next(chunk):
        if mod_row is None:
            return pl.BlockSpec((None, None, 1, d), lambda bi, i: (layer, nxt(bi, i)[0], 0, chunk))
        return _mod_spec(layer, mod_row, chunk, d)

    return pl.pallas_call(
        kern,
        grid=(b, nt),
        in_specs=[
            pl.BlockSpec((1, tm, d), lambda bi, i: (bi, i, 0)),
            pl.BlockSpec((1, tm, d), lambda bi, i: nxt(bi, i) + (0,)),
            pl.BlockSpec((None, 1, d), lambda bi, i: (layer, 0, 0)),
            _mod_spec(layer, mod_row, SC1, d),
            _mod_spec(layer, mod_row, SH1, d),
            mod_next(SC1),
            mod_next(SH1),
            _resident((None, d, n), lambda bi, i: (w_layer, 0, 0)),
        ],
        out_specs=pl.BlockSpec((1, tm, n), lambda bi, i: (bi, i, 0)),
        out_shape=jax.ShapeDtypeStruct((b, l, n), BF16),
        scratch_shapes=[pltpu.VMEM((tm, d), BF16), pltpu.VMEM((tm, d), BF16)],
        compiler_params=_cparams(("arbitrary", "arbitrary")),
        name="norm_in",
    )(x, x, g, mods, mods, mods, mods, w)


def _conv_kernel(x_ref, gb_ref, gc_ref, w_ref, o_ref, *, seq_len):
    n_rows = x_ref.shape[1]
    pos = lax.broadcasted_iota(jnp.int32, (n_rows, LANES), 0) % seq_len
    first = pos == 0
    last = pos == seq_len - 1
    for s in range(x_ref.shape[2] // LANES):
        cols = slice(s * LANES, (s + 1) * LANES)
        z = gc_ref[0, :, cols].astype(F32) * x_ref[0, :, cols].astype(F32)
        z_prev = jnp.where(first, 0.0, pltpu.roll(z, 1, 0))
        z_next = jnp.where(last, 0.0, pltpu.roll(z, n_rows - 1, 0))
        y = z_prev * w_ref[0:1, cols] + z * w_ref[1:2, cols] + z_next * w_ref[2:3, cols]
        o_ref[0, :, cols] = (gb_ref[0, :, cols].astype(F32) * y).astype(o_ref.dtype)


def _conv_call(p, w_conv, layer, seq_len):
    b, l, _ = p.shape
    width = w_conv.shape[-1]
    blk = lambda j: pl.BlockSpec((1, l, width), lambda bi: (bi, 0, j))
    return pl.pallas_call(
        functools.partial(_conv_kernel, seq_len=seq_len),
        grid=(b,),
        in_specs=[blk(0), blk(1), blk(2),
                  pl.BlockSpec((None, 3, width), lambda bi: (layer, 0, 0))],
        out_specs=pl.BlockSpec((1, l, width), lambda bi: (bi, 0, 0)),
        out_shape=jax.ShapeDtypeStruct((b, l, width), BF16),
        compiler_params=_cparams(("parallel",)),
        name="short_conv",
    )(p, p, p, w_conv)


def _fourier_kernel(z_ref, ccsc_ref, cpsp_ref, o_ref, rhs_ref, *, out_scale):
    n_pos = z_ref.shape[1]
    cw = ccsc_ref.shape[0]
    for g in range(N_GROUPS):
        cols = slice(g * cw, (g + 1) * cw)
        ab = jnp.dot(z_ref[0, :, cols], ccsc_ref[...], preferred_element_type=F32)
        rhs_ref[0:n_pos, cols] = ab[:, :cw].astype(BF16)
        rhs_ref[n_pos:2 * n_pos, cols] = ab[:, cw:].astype(BF16)
    out = jnp.dot(cpsp_ref[...], rhs_ref[...], preferred_element_type=F32)
    o_ref[0] = (out * out_scale).astype(o_ref.dtype)


def _fourier_call(p, ccsc, cpsp, col_block):
    b, l, _ = p.shape
    cw = ccsc.shape[0]
    width = N_GROUPS * cw
    kern = functools.partial(_fourier_kernel, out_scale=1.0 / math.sqrt(l * cw))
    return pl.pallas_call(
        kern,
        grid=(b,),
        in_specs=[
            pl.BlockSpec((1, l, width), lambda bi: (bi, 0, col_block)),
            _resident((cw, 2 * cw), lambda bi: (0, 0)),
            _resident((l, 2 * l), lambda bi: (0, 0)),
        ],
        out_specs=pl.BlockSpec((1, l, width), lambda bi: (bi, 0, 0)),
        out_shape=jax.ShapeDtypeStruct((b, l, width), BF16),
        scratch_shapes=[pltpu.VMEM((2 * l, width), BF16)],
        compiler_params=_cparams(("parallel",)),
        name="fourier_mix",
    )(p, ccsc, cpsp)


def _dft_tables(n):
    k = jnp.arange(n, dtype=jnp.int32)
    ang = ((k[:, None] * k[None, :]) % n).astype(F32) * (2.0 * math.pi / n)
    return jnp.cos(ang), jnp.sin(ang)


FFT_RADIX = 8


def _fourier_fft_kernel(z_ref, ccsc_ref, fsub_ref, twr_ref, twi_ref, o_ref, w_ref, g_ref, *, out_scale):
    n_pos = z_ref.shape[1]
    m = n_pos // FFT_RADIX
    cw = ccsc_ref.shape[0]
    c = math.sqrt(0.5)
    rb = 64
    gpp = 2
    for gp in range(N_GROUPS // gpp):
        groups = range(gp * gpp, (gp + 1) * gpp)
        pcols = slice(gp * gpp * cw, (gp + 1) * gpp * cw)
        for g in groups:
            ab = jnp.dot(z_ref[0, :, g * cw:(g + 1) * cw], ccsc_ref[...], preferred_element_type=F32)
            w_ref[g] = ab[:, :cw]
            w_ref[N_GROUPS + g] = ab[:, cw:]
        for n2 in range(FFT_RADIX):
            sub = pl.ds(n2, m, stride=FFT_RADIX)
            rhs = jnp.concatenate(
                [jnp.concatenate([w_ref[part * N_GROUPS + g, sub, :] for g in groups], axis=1)
                 for part in range(2)], axis=0).astype(BF16)
            g_ref[n2, :, pcols] = jnp.dot(fsub_ref[...], rhs, preferred_element_type=F32)
        for r in range(m // rb):
            rows = slice(r * rb, (r + 1) * rb)
            irows = slice(m + r * rb, m + (r + 1) * rb)
            for g in groups:
                cols = slice(g * cw, (g + 1) * cw)
                hr, hi = [g_ref[0, rows, cols]], [g_ref[0, irows, cols]]
                for n2 in range(1, FFT_RADIX):
                    gr, gi = g_ref[n2, rows, cols], g_ref[n2, irows, cols]
                    tr, ti = twr_ref[n2, rows, :], twi_ref[n2, rows, :]
                    hr.append(tr * gr - ti * gi)
                    hi.append(tr * gi + ti * gr)
                a0, a1 = hr[0] + hr[4], hr[0] - hr[4]
                a2 = hr[2] + hr[6]
                b0, b1 = hr[1] + hr[5], hr[1] - hr[5]
                b2, b3 = hr[3] + hr[7], hr[3] - hr[7]
                d1, d2, d3 = hi[1] - hi[5], hi[2] - hi[6], hi[3] - hi[7]
                e0, e1 = a0 + a2, b0 + b2
                odd_r = c * (b1 - b3)
                odd_i = c * (d1 + d3)
                p1, p3 = a1 + odd_r, a1 - odd_r
                p2 = a0 - a2
                q1, q3 = odd_i + d2, odd_i - d2
                q2 = (hi[1] + hi[5]) - (hi[3] + hi[7])
                ys = (e0 + e1, p1 + q1, p2 + q2, p3 + q3, e0 - e1, p3 - q3, p2 - q2, p1 - q1)
                for k2, y in enumerate(ys):
                    orow = slice(k2 * m + r * rb, k2 * m + (r + 1) * rb)
                    o_ref[0, orow, cols] = (y * out_scale).astype(o_ref.dtype)


def _fourier_fft_call(p, ccsc_neg, fsub, twr, twi, col_block):
    b, l, _ = p.shape
    cw = ccsc_neg.shape[0]
    width = N_GROUPS * cw
    m = l // FFT_RADIX
    kern = functools.partial(_fourier_fft_kernel, out_scale=1.0 / math.sqrt(l * cw))
    return pl.pallas_call(
        kern,
        grid=(b,),
        in_specs=[
            pl.BlockSpec((1, l, width), lambda bi: (bi, 0, col_block)),
            _resident((cw, 2 * cw), lambda bi: (0, 0)),
            _resident((2 * m, 2 * m), lambda bi: (0, 0)),
            _resident((FFT_RADIX, m, LANES), lambda bi: (0, 0, 0)),
            _resident((FFT_RADIX, m, LANES), lambda bi: (0, 0, 0)),
        ],
        out_specs=pl.BlockSpec((1, l, width), lambda bi: (bi, 0, 0)),
        out_shape=jax.ShapeDtypeStruct((b, l, width), BF16),
        scratch_shapes=[pltpu.VMEM((2 * N_GROUPS, l, cw), F32),
                        pltpu.VMEM((FFT_RADIX, 2 * m, width), F32)],
        compiler_params=_cparams(("parallel",)),
        name="fourier_fft",
    )(p, ccsc_neg, fsub, twr, twi)


def _fft_tables(n_pos):
    m = n_pos // FFT_RADIX
    cos_m, sin_m = _dft_tables(m)
    fsub = jnp.concatenate([jnp.concatenate([cos_m, sin_m], axis=1),
                            jnp.concatenate([-sin_m, cos_m], axis=1)], axis=0).astype(BF16)
    n2 = jnp.arange(FFT_RADIX, dtype=jnp.int32)[:, None]
    k1 = jnp.arange(m, dtype=jnp.int32)[None, :]
    ang = ((n2 * k1) % n_pos).astype(F32) * (2.0 * math.pi / n_pos)
    bcast = lambda t: jnp.broadcast_to(t[:, :, None], (FFT_RADIX, m, LANES))
    return fsub, bcast(jnp.cos(ang)), bcast(-jnp.sin(ang))


def _nt_dot(a, b):
    return lax.dot_general(a, b, (((1,), (1,)), ((), ())), preferred_element_type=F32)


def _latent_attn_kernel(sink_ref, q_ref, k_ref, v_ref, kc_ref, vc_ref, cos_ref, sin_ref, o_ref,
                        qlo_ref, qhi_ref, kpad_ref, kswp_ref, vt_ref, kcs_ref, vct_ref, *, layer):
    n_pos = q_ref.shape[1]
    n_blk = n_pos // BLOCK
    n_ctx = kc_ref.shape[1]
    half = LANES // 2

    lane = lax.broadcasted_iota(jnp.int32, (BLOCK, LANES), 1)
    low16 = (lane % 32) < 16
    lo_half = lane < half

    def prep_block(r, carry):
        rows = pl.ds(pl.multiple_of(r * BLOCK, BLOCK), BLOCK)
        cos = cos_ref[rows, :]
        sin = sin_ref[rows, :]

        def rope(t):
            swapped = jnp.where(low16, pltpu.roll(t, LANES - 16, 1), pltpu.roll(t, 16, 1))
            return t * cos + swapped * sin

        for s in range(q_ref.shape[-1] // LANES):
            cols = slice(s * LANES, (s + 1) * LANES)
            t = rope(q_ref[0, rows, cols].astype(F32)) * (SCALE * LOG2E)
            qlo_ref[rows, cols] = jnp.where(lo_half, t, 0.0).astype(BF16)
            qhi_ref[rows, cols] = jnp.where(lo_half, 0.0, t).astype(BF16)
        prow = pl.ds(pl.multiple_of(r * BLOCK + BLOCK, BLOCK), BLOCK)
        tk = rope(k_ref[0, rows, :].astype(F32))
        kpad_ref[prow, :] = tk.astype(BF16)
        kswp_ref[prow, :] = pltpu.roll(tk, half, 1).astype(BF16)
        vt_ref[r + 1] = v_ref[0, rows, :].astype(F32).T.astype(BF16)
        return carry

    zeros = jnp.zeros((BLOCK, LANES), BF16)
    for ref in (kpad_ref, kswp_ref):
        ref[0:BLOCK, :] = zeros
        ref[n_pos + BLOCK:n_pos + 2 * BLOCK, :] = zeros
    vt_ref[0] = zeros
    vt_ref[n_blk + 1] = zeros
    kcs_ref[...] = pltpu.roll(kc_ref[0].astype(F32), half, 1).astype(BF16)
    for cb in range(n_ctx // BLOCK):
        crow = slice(cb * BLOCK, (cb + 1) * BLOCK)
        vct_ref[:, crow] = vc_ref[0, crow, :].astype(F32).T.astype(BF16)
    lax.fori_loop(0, n_blk, prep_block, 0)

    head_order = (0, 2, 5, 7, 1, 3, 4, 6)
    col_of = {h: i for i, h in enumerate(head_order)}
    n_col = N_HEADS * BLOCK
    kj = lax.broadcasted_iota(jnp.int32, (3 * BLOCK, BLOCK), 0)
    qi = lax.broadcasted_iota(jnp.int32, (3 * BLOCK, BLOCK), 1)
    in_window = (kj >= qi) & (kj <= qi + 2 * BLOCK)
    sk = jnp.concatenate([jnp.full((1, BLOCK), sink_ref[layer, h] * LOG2E, F32) for h in head_order], axis=1)
    neg = NEG * LOG2E

    def slab(ref, rows, s):
        return ref[rows, s * LANES:(s + 1) * LANES]

    def attn_block(n, carry):
        rows = pl.ds(pl.multiple_of(n * BLOCK, BLOCK), BLOCK)
        band = pl.ds(pl.multiple_of(n * BLOCK, BLOCK), 3 * BLOCK)
        valid = in_window & (kj >= (1 - n) * BLOCK) & (kj < n_pos - (n - 1) * BLOCK)

        def masked(s):
            return jnp.concatenate([jnp.where(valid, s[:, i * BLOCK:(i + 1) * BLOCK], neg)
                                    for i in range(s.shape[1] // BLOCK)], axis=1)

        rhs_nat = jnp.concatenate([slab(qlo_ref, rows, 0), slab(qlo_ref, rows, 1),
                                   slab(qhi_ref, rows, 2), slab(qhi_ref, rows, 3)], axis=0)
        rhs_swp = jnp.concatenate([slab(qhi_ref, rows, 0), slab(qhi_ref, rows, 1),
                                   slab(qlo_ref, rows, 2), slab(qlo_ref, rows, 3)], axis=0)
        s_loc = jnp.concatenate([masked(_nt_dot(kpad_ref[band, :], rhs_nat)),
                                 masked(_nt_dot(kswp_ref[band, :], rhs_swp))], axis=1)
        s_ctx = jnp.concatenate([_nt_dot(kc_ref[0], rhs_nat),
                                 _nt_dot(kcs_ref[...], rhs_swp)], axis=1)
        m = jnp.maximum(jnp.maximum(jnp.max(s_loc, axis=0, keepdims=True),
                                    jnp.max(s_ctx, axis=0, keepdims=True)), sk)
        e_loc = jnp.exp2(s_loc - m)
        e_ctx = jnp.exp2(s_ctx - m)
        inv_den = 1.0 / (jnp.sum(e_loc, axis=0, keepdims=True) + jnp.sum(e_ctx, axis=0, keepdims=True)
                         + jnp.exp2(sk - m))
        e_loc = e_loc.astype(BF16)
        e_ctx = e_ctx.astype(BF16)
        vt_band = jnp.concatenate([vt_ref[n], vt_ref[n + 1], vt_ref[n + 2]], axis=1)
        o_t = {}
        for g in range(N_KV_HEADS):
            heads = [4 * g + j for j in range(HEADS_PER_KV)]
            pick = lambda arr: jnp.concatenate(
                [arr[:, col_of[h] * BLOCK:(col_of[h] + 1) * BLOCK] for h in heads], axis=1)
            vrows = slice(g * HEAD_DIM, (g + 1) * HEAD_DIM)
            o = (jnp.dot(vt_band[vrows], pick(e_loc), preferred_element_type=F32)
                 + jnp.dot(vct_ref[vrows, :], pick(e_ctx), preferred_element_type=F32)) * pick(inv_den)
            for j, h in enumerate(heads):
                o_t[h] = o[:, j * BLOCK:(j + 1) * BLOCK]
        slabs = [jnp.concatenate([o_t[2 * s], o_t[2 * s + 1]], axis=0).T for s in range(N_HEADS // 2)]
        o_ref[0, rows, :] = jnp.concatenate(slabs, axis=1).astype(o_ref.dtype)
        return carry

    lax.fori_loop(0, n_blk, attn_block, 0, unroll=4)


def _latent_attn_call(p, pc, kc_block, sink, cos_t, sin_t, layer):
    b, l, _ = p.shape
    lc = pc.shape[1]
    qw = N_HEADS * HEAD_DIM
    kvw = N_KV_HEADS * HEAD_DIM
    q_block = 2048 // qw
    k_block = 2560 // kvw
    kern = functools.partial(_latent_attn_kernel, layer=layer)
    return pl.pallas_call(
        kern,
        grid=(b,),
        in_specs=[
            pl.BlockSpec(memory_space=pltpu.SMEM),
            pl.BlockSpec((1, l, qw), lambda bi: (bi, 0, q_block)),
            pl.BlockSpec((1, l, kvw), lambda bi: (bi, 0, k_block)),
            pl.BlockSpec((1, l, kvw), lambda bi: (bi, 0, k_block + 1)),
            pl.BlockSpec((1, lc, kvw), lambda bi: (bi, 0, kc_block)),
            pl.BlockSpec((1, lc, kvw), lambda bi: (bi, 0, kc_block + 1)),
            _resident((l, LANES), lambda bi: (0, 0)),
            _resident((l, LANES), lambda bi: (0, 0)),
        ],
        out_specs=pl.BlockSpec((1, l, qw), lambda bi: (bi, 0, 0)),
        out_shape=jax.ShapeDtypeStruct((b, l, qw), BF16),
        scratch_shapes=[pltpu.VMEM((l, qw), BF16),
                        pltpu.VMEM((l, qw), BF16),
                        pltpu.VMEM((l + 2 * BLOCK, kvw), BF16),
                        pltpu.VMEM((l + 2 * BLOCK, kvw), BF16),
                        pltpu.VMEM((l // BLOCK + 2, kvw, BLOCK), BF16),
                        pltpu.VMEM((lc, kvw), BF16),
                        pltpu.VMEM((kvw, lc), BF16)],
        compiler_params=_cparams(("parallel",)),
        name="latent_attention",
    )(sink, p, p, p, pc, pc, cos_t, sin_t)


def _ctx_attn_kernel(sink_ref, q_ref, kc_ref, vc_ref, o_ref, *, layer):
    outs = []
    for h in range(N_HEADS):
        kv = h // HEADS_PER_KV
        hc = slice(kv * HEAD_DIM, (kv + 1) * HEAD_DIM)
        qh = q_ref[0, :, h * HEAD_DIM:(h + 1) * HEAD_DIM] * SCALE
        s = _nt_dot(qh, kc_ref[0, :, hc])
        sk = sink_ref[layer, h]
        m = jnp.maximum(jnp.max(s, axis=-1, keepdims=True), sk)
        e = jnp.exp(s - m)
        den = jnp.sum(e, axis=-1, keepdims=True) + jnp.exp(sk - m)
        o = jnp.dot(e.astype(BF16), vc_ref[0, :, hc], preferred_element_type=F32)
        outs.append(o / den)
    o_ref[0] = jnp.concatenate(outs, axis=-1).astype(o_ref.dtype)


def _ctx_attn_call(pc, sink, layer):
    b, lc, _ = pc.shape
    qw = N_HEADS * HEAD_DIM
    kvw = N_KV_HEADS * HEAD_DIM
    kern = functools.partial(_ctx_attn_kernel, layer=layer)
    return pl.pallas_call(
        kern,
        grid=(b,),
        in_specs=[
            pl.BlockSpec(memory_space=pltpu.SMEM),
            pl.BlockSpec((1, lc, qw), lambda bi: (bi, 0, 2048 // qw)),
            pl.BlockSpec((1, lc, kvw), lambda bi: (bi, 0, 2560 // kvw)),
            pl.BlockSpec((1, lc, kvw), lambda bi: (bi, 0, 2560 // kvw + 1)),
        ],
        out_specs=pl.BlockSpec((1, lc, qw), lambda bi: (bi, 0, 0)),
        out_shape=jax.ShapeDtypeStruct((b, lc, qw), BF16),
        compiler_params=_cparams(("parallel",)),
        name="context_attention",
    )(sink, pc, pc, pc)


def _rope_tables(n_pos):
    quarter = HEAD_DIM // 4
    inv = ROPE_THETA ** (-jnp.arange(quarter, dtype=F32) / quarter)
    pos = jnp.arange(n_pos, dtype=jnp.int32)
    row = (pos // GRID_W).astype(F32)
    col = (pos % GRID_W).astype(F32)
    a_row = row[:, None] * inv[None, :]
    a_col = col[:, None] * inv[None, :]
    ang = jnp.concatenate([a_row, a_row, a_col, a_col], axis=-1)
    sign = jnp.tile(jnp.concatenate([-jnp.ones((quarter,), F32), jnp.ones((quarter,), F32)]), 2)
    reps = LANES // HEAD_DIM
    return jnp.tile(jnp.cos(ang), (1, reps)), jnp.tile(jnp.sin(ang) * sign, (1, reps))


def _gelu_tanh(x):
    k1 = -2.0 * math.sqrt(2.0 / math.pi) * LOG2E
    t = x * (k1 + (k1 * 0.044715) * (x * x))
    return x * (1.0 / (1.0 + jnp.exp2(t)))


def _chunk_mlp_kernel(u_ref, v_ref, w_ref, bias_ref, o_ref):
    n_rows = u_ref.shape[1]
    averager = jnp.full((LANES, LANES), 1.0 / LANES, BF16)
    for gi in range(u_ref.shape[2] // LANES):
        cols = slice(gi * LANES, (gi + 1) * LANES)
        u = _gelu_tanh(u_ref[0, :, cols].astype(F32))
        v = _gelu_tanh(v_ref[0, :, cols].astype(F32))
        mu = jnp.dot(v.astype(BF16), averager, preferred_element_type=F32)
        dlt = v - mu
        var = jnp.dot((dlt * dlt).astype(BF16), averager, preferred_element_type=F32)
        vn = (dlt * lax.rsqrt(var + EPS)).astype(BF16)
        w = w_ref[gi].astype(BF16)
        for n in range(n_rows // CHUNK):
            rows = slice(n * CHUNK, (n + 1) * CHUNK)
            mixed = jnp.dot(w, vn[rows], preferred_element_type=F32) + bias_ref[gi]
            o_ref[0, rows, cols] = (u[rows] * mixed).astype(o_ref.dtype)


def _chunk_mlp_call(p, w_s, bias, layer):
    b, l, _ = p.shape
    gpb = 2
    bw = gpb * LANES
    u_block = 2816 // bw
    v_block = 3328 // bw
    return pl.pallas_call(
        _chunk_mlp_kernel,
        grid=(b, N_GROUPS // gpb),
        in_specs=[
            pl.BlockSpec((1, l, bw), lambda bi, j: (bi, 0, u_block + j)),
            pl.BlockSpec((1, l, bw), lambda bi, j: (bi, 0, v_block + j)),
            pl.BlockSpec((None, gpb, CHUNK, CHUNK), lambda bi, j: (layer, j, 0, 0)),
            pl.BlockSpec((None, gpb, CHUNK, LANES), lambda bi, j: (layer, j, 0, 0)),
        ],
        out_specs=pl.BlockSpec((1, l, bw), lambda bi, j: (bi, 0, j)),
        out_shape=jax.ShapeDtypeStruct((b, l, N_GROUPS * LANES), BF16),
        compiler_params=_cparams(("parallel", "parallel")),
        name="chunk_mlp",
    )(p, p, w_s, bias)


def _out_proj_kernel(ya_ref, yb_ref, yc_ref, yd_ref, x_ref, gt_ref, w_ref, o_ref, *, tn):
    d = o_ref.shape[-1]
    kw = ya_ref.shape[-1]
    ys = (ya_ref, yb_ref, yc_ref, yd_ref)
    for j in range(d // tn):
        cols = slice(j * tn, (j + 1) * tn)
        acc = jnp.dot(ys[0][0], w_ref[0:kw, cols], preferred_element_type=F32)
        for k in range(1, 4):
            acc = acc + jnp.dot(ys[k][0], w_ref[k * kw:(k + 1) * kw, cols], preferred_element_type=F32)
        o_ref[0, :, cols] = x_ref[0, :, cols] + gt_ref[:, cols] * acc


def _out_proj_call(ys, x, mods, w_out, layer, mod_row):
    b, l, d = x.shape
    kw = ys[0].shape[-1]
    tm = min(1024, l)
    kern = functools.partial(_out_proj_kernel, tn=512)
    yspec = pl.BlockSpec((1, tm, kw), lambda bi, i: (bi, i, 0))
    return pl.pallas_call(
        kern,
        grid=(b, l // tm),
        in_specs=[yspec, yspec, yspec, yspec,
                  pl.BlockSpec((1, tm, d), lambda bi, i: (bi, i, 0)),
                  _mod_spec(layer, mod_row, GT1, d),
                  _resident((None, d, d), lambda bi, i: (layer, 0, 0))],
        out_specs=pl.BlockSpec((1, tm, d), lambda bi, i: (bi, i, 0)),
        out_shape=jax.ShapeDtypeStruct((b, l, d), F32),
        compiler_params=_cparams(("parallel", "parallel")),
        name="out_proj",
    )(*ys, x, mods, w_out)


def _swiglu_kernel(x_ref, g_ref, sc_ref, sh_ref, gt_ref, wg_hbm, wu_hbm, wd_hbm, gf_ref, o_ref,
                   h_ref, wg_buf, wu_buf, wd_buf, sem, *, tm, tf, layer, final_norm):
    n_f = wg_hbm.shape[-1] // tf
    tile = pl.program_id(0) * pl.num_programs(1) + pl.program_id(1)
    n_tiles = pl.num_programs(0) * pl.num_programs(1)
    first_chunk = tile * n_f

    def chunk_copies(f, slot):
        cols = pl.ds(pl.multiple_of(f * tf, tf), tf)
        return (pltpu.make_async_copy(wg_hbm.at[layer, :, cols], wg_buf.at[slot], sem.at[0, slot]),
                pltpu.make_async_copy(wu_hbm.at[layer, :, cols], wu_buf.at[slot], sem.at[1, slot]),
                pltpu.make_async_copy(wd_hbm.at[layer, cols, :], wd_buf.at[slot], sem.at[2, slot]))

    def start_chunk(f, slot):
        for cp in chunk_copies(f, slot):
            cp.start()

    def wait_chunk(f, slot):
        for cp in chunk_copies(f, slot):
            cp.wait()

    def gated_ffn(h, slot):
        gate = jnp.dot(h, wg_buf[slot], preferred_element_type=F32)
        up = jnp.dot(h, wu_buf[slot], preferred_element_type=F32)
        a = (gate * jax.nn.sigmoid(gate) * up).astype(BF16)
        return gt_ref[...] * jnp.dot(a, wd_buf[slot], preferred_element_type=F32)

    @pl.when(tile == 0)
    def _():
        start_chunk(0, 0)

    slot0 = lax.rem(first_chunk, 2)
    wait_chunk(0, slot0)
    start_chunk(1, 1 - slot0)
    rb = min(tm, 256)
    for r in range(tm // rb):
        rows = slice(r * rb, (r + 1) * rb)
        xr = x_ref[0, rows, :]
        ms = jnp.mean(xr * xr, axis=-1, keepdims=True)
        h = (((xr * lax.rsqrt(ms + EPS)) * g_ref[...]) * (1.0 + sc_ref[...]) + sh_ref[...]).astype(BF16)
        h_ref[rows, :] = h
        o_ref[0, rows, :] = xr + gated_ffn(h, slot0)

    def chunk_step(f, carry):
        slot = lax.rem(first_chunk + f, 2)
        wait_chunk(f, slot)

        @pl.when(jnp.logical_or(f + 1 < n_f, tile + 1 < n_tiles))
        def _():
            start_chunk(lax.rem(f + 1, n_f), 1 - slot)

        o_ref[0] += gated_ffn(h_ref[...], slot)
        return carry

    lax.fori_loop(1, n_f, chunk_step, 0)

    if final_norm:
        def body(r, carry):
            rows = pl.ds(pl.multiple_of(r * 64, 64), 64)
            xr = o_ref[0, rows, :]
            ms = jnp.mean(xr * xr, axis=-1, keepdims=True)
            o_ref[0, rows, :] = (xr * lax.rsqrt(ms + EPS)) * gf_ref[...]
            return carry

        lax.fori_loop(0, tm // 64, body, 0)


def _swiglu_call(x, mods, g, w_gate, w_up, w_down, g_final, layer, mod_row, final_norm):
    b, l, d = x.shape
    ff = w_gate.shape[-1]
    tm = min(1024, l)
    tf = 512
    assert ff % tf == 0 and ff // tf >= 2
    kern = functools.partial(_swiglu_kernel, tm=tm, tf=tf, layer=layer, final_norm=final_norm)
    hbm = pl.BlockSpec(memory_space=pl.ANY)
    return pl.pallas_call(
        kern,
        grid=(b, l // tm),
        in_specs=[
            pl.BlockSpec((1, tm, d), lambda bi, i: (bi, i, 0)),
            pl.BlockSpec((None, 1, d), lambda bi, i: (layer, 0, 0)),
            _mod_spec(layer, mod_row, SC2, d),
            _mod_spec(layer, mod_row, SH2, d),
            _mod_spec(layer, mod_row, GT2, d),
            hbm, hbm, hbm,
            pl.BlockSpec((1, d), lambda bi, i: (0, 0)),
        ],
        out_specs=pl.BlockSpec((1, tm, d), lambda bi, i: (bi, i, 0)),
        out_shape=jax.ShapeDtypeStruct((b, l, d), F32),
        scratch_shapes=[pltpu.VMEM((tm, d), BF16),
                        pltpu.VMEM((2, d, tf), BF16),
                        pltpu.VMEM((2, d, tf), BF16),
                        pltpu.VMEM((2, tf, d), BF16),
                        pltpu.SemaphoreType.DMA((3, 2))],
        compiler_params=_cparams(("arbitrary", "arbitrary")),
        name="swiglu",
    )(x, g, mods, mods, mods, w_gate, w_up, w_down, g_final)


def kernel(x, c, ctx, c_ctx, w_ada, b_ada, g_norm1, w_in, w_conv, sink, w_s, b_s, w_out, g_norm2,
           w_gate, w_up, w_down, g_final):
    batch, n_pos, d = x.shape
    n_ctx = ctx.shape[1]
    depth = w_ada.shape[0]
    cw = (d // 4) // N_GROUPS
    ctx_row = batch

    w_in_b = w_in.astype(BF16)
    w_out_b = w_out.astype(BF16)
    w_gate_b = w_gate.astype(BF16)
    w_up_b = w_up.astype(BF16)
    w_down_b = w_down.astype(BF16)
    g1 = g_norm1.reshape(depth, 1, d)
    g2 = g_norm2.reshape(depth, 1, d)
    gf = g_final.reshape(1, d)

    cs = jnp.concatenate([c, c_ctx[None, :], jnp.zeros((N_MOD_ROWS - batch - 1, d), F32)], axis=0)
    mods = _ada_call(cs, w_ada, b_ada).reshape(depth, N_MOD_ROWS, 1, 6 * d)

    cos_c, sin_c = _dft_tables(cw)
    ccsc = jnp.concatenate([cos_c, sin_c], axis=1).astype(BF16)
    use_fft = n_pos % (FFT_RADIX * 64) == 0 and n_pos // FFT_RADIX >= LANES
    if use_fft:
        ccsc_neg = jnp.concatenate([cos_c, -sin_c], axis=1).astype(BF16)
        fsub, twr, twi = _fft_tables(n_pos)
        latent_fourier = lambda pp: _fourier_fft_call(pp, ccsc_neg, fsub, twr, twi, f_block)
    else:
        cos_p, sin_p = _dft_tables(n_pos)
        cpsp = jnp.concatenate([cos_p, -sin_p], axis=1).astype(BF16)
        latent_fourier = lambda pp: _fourier_call(pp, ccsc, cpsp, f_block)
    cos_x, sin_x = _dft_tables(n_ctx)
    cpsp_ctx = jnp.concatenate([cos_x, -sin_x], axis=1).astype(BF16)
    rope_cos, rope_sin = _rope_tables(n_pos)
    mlp_bias = jnp.broadcast_to(b_s[:, :, :, None], b_s.shape + (cw,))
    f_block = 1536 // (N_GROUPS * cw)

    def flat(a):
        return a.reshape(1, -1, a.shape[-1])

    def unflat(a):
        return a.reshape(batch, n_ctx, a.shape[-1])

    xc = flat(ctx)
    for layer in range(depth):
        last = layer == depth - 1
        if last:
            w_kv = w_in_b[layer:layer + 1, :, 2560:2816]
            pc = unflat(_norm_in_call(xc, mods, g1, w_kv, layer, 0, ctx_row, tn=256))
            kc_block = 0
        else:
            pc_flat = _norm_in_call(xc, mods, g1, w_in_b, layer, layer, ctx_row, tn=768)
            pc = unflat(pc_flat)
            kc_block = 2560 // (N_KV_HEADS * HEAD_DIM)
            ys_c = (_conv_call(pc_flat, w_conv, layer, n_ctx),
                    flat(_fourier_call(pc, ccsc, cpsp_ctx, f_block)),
                    flat(_ctx_attn_call(pc, sink, layer)),
                    _chunk_mlp_call(pc_flat, w_s, mlp_bias, layer))
            xc_new = _out_proj_call(ys_c, xc, mods, w_out_b, layer, ctx_row)
            xc_next = _swiglu_call(xc_new, mods, g2, w_gate_b, w_up_b, w_down_b, gf, layer, ctx_row, False)
        p = _norm_in_call(x, mods, g1, w_in_b, layer, layer, None, tn=768)
        ys = (_conv_call(p, w_conv, layer, n_pos),
              latent_fourier(p),
              _latent_attn_call(p, pc, kc_block, sink, rope_cos, rope_sin, layer),
              _chunk_mlp_call(p, w_s, mlp_bias, layer))
        x = _out_proj_call(ys, x, mods, w_out_b, layer, None)
        x = _swiglu_call(x, mods, g2, w_gate_b, w_up_b, w_down_b, gf, layer, None, last)
        if not last:
            xc = xc_next
    return x
```

```python
import functools
import math

import jax
import jax.numpy as jnp
from jax import lax
from jax.experimental import pallas as pl
from jax.experimental.pallas import tpu as pltpu

F32 = jnp.float32
BF16 = jnp.bfloat16

EPS = 1e-6
NEG = -1e30
GRID_W = 64
HEAD_DIM = 64
N_HEADS = 8
N_KV_HEADS = 2
HEADS_PER_KV = N_HEADS // N_KV_HEADS
BLOCK = 128
CHUNK = 128
N_GROUPS = 4
ROPE_THETA = 10000.0
SCALE = HEAD_DIM ** -0.5
LOG2E = math.log2(math.e)

LANES = 128
VMEM_LIMIT_BYTES = 60 * 1024 * 1024
N_MOD_ROWS = 24

SH1, SC1, GT1, SH2, SC2, GT2 = range(6)


def _cparams(sem):
    return pltpu.CompilerParams(dimension_semantics=sem, vmem_limit_bytes=VMEM_LIMIT_BYTES)


def _resident(block_shape, index_map):
    return pl.BlockSpec(block_shape, index_map, pipeline_mode=pl.Buffered(1))


def _ada_kernel(c_ref, w_ref, b_ref, o_ref):
    c = c_ref[...]
    s = (c * jax.nn.sigmoid(c)).astype(BF16)
    o_ref[...] = jnp.dot(s, w_ref[...].astype(BF16), preferred_element_type=F32) + b_ref[...]


def _ada_call(cs, w_ada, b_ada):
    depth, d, n = w_ada.shape
    tn = 1024
    return pl.pallas_call(
        _ada_kernel,
        grid=(depth, n // tn),
        in_specs=[
            pl.BlockSpec((N_MOD_ROWS, d), lambda l, j: (0, 0)),
            pl.BlockSpec((None, d, tn), lambda l, j: (l, 0, j)),
            pl.BlockSpec((None, 1, tn), lambda l, j: (l, 0, j)),
        ],
        out_specs=pl.BlockSpec((None, N_MOD_ROWS, tn), lambda l, j: (l, 0, j)),
        out_shape=jax.ShapeDtypeStruct((depth, N_MOD_ROWS, n), F32),
        compiler_params=_cparams(("parallel", "parallel")),
        name="ada_mod",
    )(cs, w_ada, b_ada.reshape(depth, 1, n))


def _mod_spec(layer, row, chunk, d):
    if row is None:
        return pl.BlockSpec((None, None, 1, d), lambda b, *_: (layer, b, 0, chunk))
    return pl.BlockSpec((None, None, 1, d), lambda b, *_: (layer, row, 0, chunk))


def _norm_modulate_rows(x_ref, g_ref, sc_ref, sh_ref, h_ref, tm, rb):
    def body(r, carry):
        rows = pl.ds(pl.multiple_of(r * rb, rb), rb)
        xr = x_ref[0, rows, :]
        ms = jnp.mean(xr * xr, axis=-1, keepdims=True)
        y = xr * lax.rsqrt(ms + EPS)
        h = (y * g_ref[...]) * (1.0 + sc_ref[...]) + sh_ref[...]
        h_ref[rows, :] = h.astype(h_ref.dtype)
        return carry

    lax.fori_loop(0, tm // rb, body, 0)


def _norm_in_kernel(x_ref, xn_ref, g_ref, sc_ref, sh_ref, scn_ref, shn_ref, w_ref, o_ref, ha_ref, hb_ref,
                    *, tm, tn):
    t = pl.program_id(0) * pl.num_programs(1) + pl.program_id(1)
    n = o_ref.shape[-1]

    @pl.when(t == 0)
    def _():
        _norm_modulate_rows(x_ref, g_ref, sc_ref, sh_ref, ha_ref, tm, min(tm, 64))

    def step(h_cur, h_next):
        rb = min(tm, 32)
        n_rb = tm // rb
        ct = 256 if tn % 256 == 0 else tn
        n_ct = n // ct
        anchors = [None] * n_ct
        for r in range(n_rb):
            rows = slice(r * rb, (r + 1) * rb)
            xr = xn_ref[0, rows, :]
            ms = jnp.mean(xr * xr, axis=-1, keepdims=True)
            h = ((xr * lax.rsqrt(ms + EPS)) * g_ref[...]) * (1.0 + scn_ref[...]) + shn_ref[...]
            hb = h.astype(BF16)
            h_next[rows, :] = hb
            folded = hb[:, 0:LANES]
            for k in range(1, hb.shape[1] // LANES):
                folded = folded + hb[:, k * LANES:(k + 1) * LANES]
            while folded.shape[0] > 16:
                half_rows = folded.shape[0] // 2
                folded = folded[:half_rows] + folded[half_rows:]
            word = pltpu.bitcast(folded, jnp.uint32)
            zero = lax.shift_right_logical(lax.shift_right_logical(word, jnp.uint32(16)), jnp.uint32(16))
            j = r * n_ct // n_rb
            anchors[j] = zero if anchors[j] is None else anchors[j] | zero
        for j in range(n // tn):
            cols = slice(j * tn, (j + 1) * tn)
            out = jnp.dot(h_cur[...], w_ref[:, cols], preferred_element_type=F32)
            pieces = []
            for s in range(tn // ct):
                piece = out[:, s * ct:(s + 1) * ct]
                anchor = anchors[j * (tn // ct) + s]
                if anchor is not None and ct % LANES == 0:
                    zrow = anchor[0:1, :].astype(F32)
                    piece = piece + jnp.concatenate([zrow] * (ct // LANES), axis=1)
                pieces.append(piece)
            o_ref[0, :, cols] = jnp.concatenate(pieces, axis=1).astype(o_ref.dtype)

    parity = lax.rem(t, 2)

    @pl.when(parity == 0)
    def _():
        step(ha_ref, hb_ref)

    @pl.when(parity == 1)
    def _():
        step(hb_ref, ha_ref)


def _norm_in_call(x, mods, g, w, layer, w_layer, mod_row, tn):
    b, l, d = x.shape
    n = w.shape[-1]
    tm = min(512, l)
    nt = l // tm
    kern = functools.partial(_norm_in_kernel, tm=tm, tn=tn)

    def nxt(bi, i):
        t = jnp.minimum(bi * nt + i + 1, b * nt - 1)
        return t // nt, t % nt

    def mod_next(chunk):
        if mod_row is None:
            return pl.BlockSpec((None, None, 1, d), lambda bi, i: (layer, nxt(bi, i)[0], 0, chunk))
        return _mod_spec(layer, mod_row, chunk, d)

    return pl.pallas_call(
        kern,
        grid=(b, nt),
        in_specs=[
            pl.BlockSpec((1, tm, d), lambda bi, i: (0, 0, 0)),
            pl.BlockSpec((1, tm, d), lambda bi, i: nxt(bi, i) + (0,)),
            pl.BlockSpec((None, 1, d), lambda bi, i: (layer, 0, 0)),
            _mod_spec(layer, mod_row, SC1, d),
            _mod_spec(layer, mod_row, SH1, d),
            mod_next(SC1),
            mod_next(SH1),
            _resident((None, d, n), lambda bi, i: (w_layer, 0, 0)),
        ],
        out_specs=pl.BlockSpec((1, tm, n), lambda bi, i: (bi, i, 0)),
        out_shape=jax.ShapeDtypeStruct((b, l, n), BF16),
        scratch_shapes=[pltpu.VMEM((tm, d), BF16), pltpu.VMEM((tm, d), BF16)],
        compiler_params=_cparams(("arbitrary", "arbitrary")),
        name="norm_in",
    )(x, x, g, mods, mods, mods, mods, w)


def _conv_kernel(x_ref, gb_ref, gc_ref, w_ref, o_ref, *, seq_len):
    n_rows = x_ref.shape[1]
    pos = lax.broadcasted_iota(jnp.int32, (n_rows, LANES), 0) % seq_len
    first = pos == 0
    last = pos == seq_len - 1
    for s in range(x_ref.shape[2] // LANES):
        cols = slice(s * LANES, (s + 1) * LANES)
        z = gc_ref[0, :, cols].astype(F32) * x_ref[0, :, cols].astype(F32)
        z_prev = jnp.where(first, 0.0, pltpu.roll(z, 1, 0))
        z_next = jnp.where(last, 0.0, pltpu.roll(z, n_rows - 1, 0))
        y = z_prev * w_ref[0:1, cols] + z * w_ref[1:2, cols] + z_next * w_ref[2:3, cols]
        o_ref[0, :, cols] = (gb_ref[0, :, cols].astype(F32) * y).astype(o_ref.dtype)


def _conv_call(p, w_conv, layer, seq_len):
    b, l, _ = p.shape
    width = w_conv.shape[-1]
    blk = lambda j: pl.BlockSpec((1, l, width), lambda bi: (bi, 0, j))
    return pl.pallas_call(
        functools.partial(_conv_kernel, seq_len=seq_len),
        grid=(b,),
        in_specs=[blk(0), blk(1), blk(2),
                  pl.BlockSpec((None, 3, width), lambda bi: (layer, 0, 0))],
        out_specs=pl.BlockSpec((1, l, width), lambda bi: (bi, 0, 0)),
        out_shape=jax.ShapeDtypeStruct((b, l, width), BF16),
        compiler_params=_cparams(("parallel",)),
        name="short_conv",
    )(p, p, p, w_conv)


def _fourier_kernel(z_ref, ccsc_ref, cpsp_ref, o_ref, rhs_ref, *, out_scale):
    n_pos = z_ref.shape[1]
    cw = ccsc_ref.shape[0]
    for g in range(N_GROUPS):
        cols = slice(g * cw, (g + 1) * cw)
        ab = jnp.dot(z_ref[0, :, cols], ccsc_ref[...], preferred_element_type=F32)
        rhs_ref[0:n_pos, cols] = ab[:, :cw].astype(BF16)
        rhs_ref[n_pos:2 * n_pos, cols] = ab[:, cw:].astype(BF16)
    out = jnp.dot(cpsp_ref[...], rhs_ref[...], preferred_element_type=F32)
    o_ref[0] = (out * out_scale).astype(o_ref.dtype)


def _fourier_call(p, ccsc, cpsp, col_block):
    b, l, _ = p.shape
    cw = ccsc.shape[0]
    width = N_GROUPS * cw
    kern = functools.partial(_fourier_kernel, out_scale=1.0 / math.sqrt(l * cw))
    return pl.pallas_call(
        kern,
        grid=(b,),
        in_specs=[
            pl.BlockSpec((1, l, width), lambda bi: (bi, 0, col_block)),
            _resident((cw, 2 * cw), lambda bi: (0, 0)),
            _resident((l, 2 * l), lambda bi: (0, 0)),
        ],
        out_specs=pl.BlockSpec((1, l, width), lambda bi: (bi, 0, 0)),
        out_shape=jax.ShapeDtypeStruct((b, l, width), BF16),
        scratch_shapes=[pltpu.VMEM((2 * l, width), BF16)],
        compiler_params=_cparams(("parallel",)),
        name="fourier_mix",
    )(p, ccsc, cpsp)


def _dft_tables(n):
    k = jnp.arange(n, dtype=jnp.int32)
    ang = ((k[:, None] * k[None, :]) % n).astype(F32) * (2.0 * math.pi / n)
    return jnp.cos(ang), jnp.sin(ang)


FFT_RADIX = 8


def _fourier_fft_kernel(z_ref, ccsc_ref, fsub_ref, twr_ref, twi_ref, o_ref, w_ref, g_ref, *, out_scale):
    n_pos = z_ref.shape[1]
    m = n_pos // FFT_RADIX
    cw = ccsc_ref.shape[0]
    c = math.sqrt(0.5)
    rb = 64
    gpp = 2
    for gp in range(N_GROUPS // gpp):
        groups = range(gp * gpp, (gp + 1) * gpp)
        pcols = slice(gp * gpp * cw, (gp + 1) * gpp * cw)
        for g in groups:
            ab = jnp.dot(z_ref[0, :, g * cw:(g + 1) * cw], ccsc_ref[...], preferred_element_type=F32)
            w_ref[g] = ab[:, :cw]
            w_ref[N_GROUPS + g] = ab[:, cw:]
        for n2 in range(FFT_RADIX):
            sub = pl.ds(n2, m, stride=FFT_RADIX)
            rhs = jnp.concatenate(
                [jnp.concatenate([w_ref[part * N_GROUPS + g, sub, :] for g in groups], axis=1)
                 for part in range(2)], axis=0).astype(BF16)
            g_ref[n2, :, pcols] = jnp.dot(fsub_ref[...], rhs, preferred_element_type=F32)
        for r in range(m // rb):
            rows = slice(r * rb, (r + 1) * rb)
            irows = slice(m + r * rb, m + (r + 1) * rb)
            for g in groups:
                cols = slice(g * cw, (g + 1) * cw)
                hr, hi = [g_ref[0, rows, cols]], [g_ref[0, irows, cols]]
                for n2 in range(1, FFT_RADIX):
                    gr, gi = g_ref[n2, rows, cols], g_ref[n2, irows, cols]
                    tr, ti = twr_ref[n2, rows, :], twi_ref[n2, rows, :]
                    hr.append(tr * gr - ti * gi)
                    hi.append(tr * gi + ti * gr)
                a0, a1 = hr[0] + hr[4], hr[0] - hr[4]
                a2 = hr[2] + hr[6]
                b0, b1 = hr[1] + hr[5], hr[1] - hr[5]
                b2, b3 = hr[3] + hr[7], hr[3] - hr[7]
                d1, d2, d3 = hi[1] - hi[5], hi[2] - hi[6], hi[3] - hi[7]
                e0, e1 = a0 + a2, b0 + b2
                odd_r = c * (b1 - b3)
                odd_i = c * (d1 + d3)
                p1, p3 = a1 + odd_r, a1 - odd_r
                p2 = a0 - a2
                q1, q3 = odd_i + d2, odd_i - d2
                q2 = (hi[1] + hi[5]) - (hi[3] + hi[7])
                ys = (e0 + e1, p1 + q1, p2 + q2, p3 + q3, e0 - e1, p3 - q3, p2 - q2, p1 - q1)
                for k2, y in enumerate(ys):
                    orow = slice(k2 * m + r * rb, k2 * m + (r + 1) * rb)
                    o_ref[0, orow, cols] = (y * out_scale).astype(o_ref.dtype)


def _fourier_fft_call(p, ccsc_neg, fsub, twr, twi, col_block):
    b, l, _ = p.shape
    cw = ccsc_neg.shape[0]
    width = N_GROUPS * cw
    m = l // FFT_RADIX
    kern = functools.partial(_fourier_fft_kernel, out_scale=1.0 / math.sqrt(l * cw))
    return pl.pallas_call(
        kern,
        grid=(b,),
        in_specs=[
            pl.BlockSpec((1, l, width), lambda bi: (bi, 0, col_block)),
            _resident((cw, 2 * cw), lambda bi: (0, 0)),
            _resident((2 * m, 2 * m), lambda bi: (0, 0)),
            _resident((FFT_RADIX, m, LANES), lambda bi: (0, 0, 0)),
            _resident((FFT_RADIX, m, LANES), lambda bi: (0, 0, 0)),
        ],
        out_specs=pl.BlockSpec((1, l, width), lambda bi: (bi, 0, 0)),
        out_shape=jax.ShapeDtypeStruct((b, l, width), BF16),
        scratch_shapes=[pltpu.VMEM((2 * N_GROUPS, l, cw), F32),
                        pltpu.VMEM((FFT_RADIX, 2 * m, width), F32)],
        compiler_params=_cparams(("parallel",)),
        name="fourier_fft",
    )(p, ccsc_neg, fsub, twr, twi)


def _fft_tables(n_pos):
    m = n_pos // FFT_RADIX
    cos_m, sin_m = _dft_tables(m)
    fsub = jnp.concatenate([jnp.concatenate([cos_m, sin_m], axis=1),
                            jnp.concatenate([-sin_m, cos_m], axis=1)], axis=0).astype(BF16)
    n2 = jnp.arange(FFT_RADIX, dtype=jnp.int32)[:, None]
    k1 = jnp.arange(m, dtype=jnp.int32)[None, :]
    ang = ((n2 * k1) % n_pos).astype(F32) * (2.0 * math.pi / n_pos)
    bcast = lambda t: jnp.broadcast_to(t[:, :, None], (FFT_RADIX, m, LANES))
    return fsub, bcast(jnp.cos(ang)), bcast(-jnp.sin(ang))


def _nt_dot(a, b):
    return lax.dot_general(a, b, (((1,), (1,)), ((), ())), preferred_element_type=F32)


def _latent_attn_kernel(sink_ref, q_ref, k_ref, v_ref, kc_ref, vc_ref, cos_ref, sin_ref, o_ref,
                        qlo_ref, qhi_ref, kpad_ref, kswp_ref, vt_ref, kcs_ref, vct_ref, *, layer):
    n_pos = q_ref.shape[1]
    n_blk = n_pos // BLOCK
    n_ctx = kc_ref.shape[1]
    half = LANES // 2

    lane = lax.broadcasted_iota(jnp.int32, (BLOCK, LANES), 1)
    low16 = (lane % 32) < 16
    lo_half = lane < half

    def prep_block(r, carry):
        rows = pl.ds(pl.multiple_of(r * BLOCK, BLOCK), BLOCK)
        cos = cos_ref[rows, :]
        sin = sin_ref[rows, :]

        def rope(t):
            swapped = jnp.where(low16, pltpu.roll(t, LANES - 16, 1), pltpu.roll(t, 16, 1))
            return t * cos + swapped * sin

        for s in range(q_ref.shape[-1] // LANES):
            cols = slice(s * LANES, (s + 1) * LANES)
            t = rope(q_ref[0, rows, cols].astype(F32)) * (SCALE * LOG2E)
            qlo_ref[rows, cols] = jnp.where(lo_half, t, 0.0).astype(BF16)
            qhi_ref[rows, cols] = jnp.where(lo_half, 0.0, t).astype(BF16)
        prow = pl.ds(pl.multiple_of(r * BLOCK + BLOCK, BLOCK), BLOCK)
        tk = rope(k_ref[0, rows, :].astype(F32))
        kpad_ref[prow, :] = tk.astype(BF16)
        kswp_ref[prow, :] = pltpu.roll(tk, half, 1).astype(BF16)
        vt_ref[r + 1] = v_ref[0, rows, :].astype(F32).T.astype(BF16)
        return carry

    zeros = jnp.zeros((BLOCK, LANES), BF16)
    for ref in (kpad_ref, kswp_ref):
        ref[0:BLOCK, :] = zeros
        ref[n_pos + BLOCK:n_pos + 2 * BLOCK, :] = zeros
    vt_ref[0] = zeros
    vt_ref[n_blk + 1] = zeros
    kcs_ref[...] = pltpu.roll(kc_ref[0].astype(F32), half, 1).astype(BF16)
    for cb in range(n_ctx // BLOCK):
        crow = slice(cb * BLOCK, (cb + 1) * BLOCK)
        vct_ref[:, crow] = vc_ref[0, crow, :].astype(F32).T.astype(BF16)
    lax.fori_loop(0, n_blk, prep_block, 0, unroll=4)

    head_order = (0, 2, 5, 7, 1, 3, 4, 6)
    col_of = {h: i for i, h in enumerate(head_order)}
    n_col = N_HEADS * BLOCK
    kj = lax.broadcasted_iota(jnp.int32, (3 * BLOCK, BLOCK), 0)
    qi = lax.broadcasted_iota(jnp.int32, (3 * BLOCK, BLOCK), 1)
    in_window = (kj >= qi) & (kj <= qi + 2 * BLOCK)
    sk = jnp.concatenate([jnp.full((1, BLOCK), sink_ref[layer, h] * LOG2E, F32) for h in head_order], axis=1)
    neg = NEG * LOG2E

    def slab(ref, rows, s):
        return ref[rows, s * LANES:(s + 1) * LANES]

    def attn_block(n, carry):
        rows = pl.ds(pl.multiple_of(n * BLOCK, BLOCK), BLOCK)
        band = pl.ds(pl.multiple_of(n * BLOCK, BLOCK), 3 * BLOCK)
        valid = in_window & (kj >= (1 - n) * BLOCK) & (kj < n_pos - (n - 1) * BLOCK)

        def masked(s):
            return jnp.concatenate([jnp.where(valid, s[:, i * BLOCK:(i + 1) * BLOCK], neg)
                                    for i in range(s.shape[1] // BLOCK)], axis=1)

        rhs_nat = jnp.concatenate([slab(qlo_ref, rows, 0), slab(qlo_ref, rows, 1),
                                   slab(qhi_ref, rows, 2), slab(qhi_ref, rows, 3)], axis=0)
        rhs_swp = jnp.concatenate([slab(qhi_ref, rows, 0), slab(qhi_ref, rows, 1),
                                   slab(qlo_ref, rows, 2), slab(qlo_ref, rows, 3)], axis=0)
        s_loc = jnp.concatenate([masked(_nt_dot(kpad_ref[band, :], rhs_nat)),
                                 masked(_nt_dot(kswp_ref[band, :], rhs_swp))], axis=1)
        s_ctx = jnp.concatenate([_nt_dot(kc_ref[0], rhs_nat),
                                 _nt_dot(kcs_ref[...], rhs_swp)], axis=1)
        m = jnp.maximum(jnp.maximum(jnp.max(s_loc, axis=0, keepdims=True),
                                    jnp.max(s_ctx, axis=0, keepdims=True)), sk)
        e_loc = jnp.exp2(s_loc - m)
        e_ctx = jnp.exp2(s_ctx - m)
        inv_den = 1.0 / (jnp.sum(e_loc, axis=0, keepdims=True) + jnp.sum(e_ctx, axis=0, keepdims=True)
                         + jnp.exp2(sk - m))
        e_loc = e_loc.astype(BF16)
        e_ctx = e_ctx.astype(BF16)
        vt_band = jnp.concatenate([vt_ref[n], vt_ref[n + 1], vt_ref[n + 2]], axis=1)
        o_t = {}
        for g in range(N_KV_HEADS):
            heads = [4 * g + j for j in range(HEADS_PER_KV)]
            pick = lambda arr: jnp.concatenate(
                [arr[:, col_of[h] * BLOCK:(col_of[h] + 1) * BLOCK] for h in heads], axis=1)
            vrows = slice(g * HEAD_DIM, (g + 1) * HEAD_DIM)
            o = (jnp.dot(vt_band[vrows], pick(e_loc), preferred_element_type=F32)
                 + jnp.dot(vct_ref[vrows, :], pick(e_ctx), preferred_element_type=F32)) * pick(inv_den)
            for j, h in enumerate(heads):
                o_t[h] = o[:, j * BLOCK:(j + 1) * BLOCK]
        slabs = [jnp.concatenate([o_t[2 * s], o_t[2 * s + 1]], axis=0).T for s in range(N_HEADS // 2)]
        o_ref[0, rows, :] = jnp.concatenate(slabs, axis=1).astype(o_ref.dtype)
        return carry

    lax.fori_loop(0, n_blk, attn_block, 0, unroll=4)


def _latent_attn_call(p, pc, kc_block, sink, cos_t, sin_t, layer):
    b, l, _ = p.shape
    lc = pc.shape[1]
    qw = N_HEADS * HEAD_DIM
    kvw = N_KV_HEADS * HEAD_DIM
    q_block = 2048 // qw
    k_block = 2560 // kvw
    kern = functools.partial(_latent_attn_kernel, layer=layer)
    return pl.pallas_call(
        kern,
        grid=(b,),
        in_specs=[
            pl.BlockSpec(memory_space=pltpu.SMEM),
            pl.BlockSpec((1, l, qw), lambda bi: (bi, 0, q_block)),
            pl.BlockSpec((1, l, kvw), lambda bi: (bi, 0, k_block)),
            pl.BlockSpec((1, l, kvw), lambda bi: (bi, 0, k_block + 1)),
            pl.BlockSpec((1, lc, kvw), lambda bi: (bi, 0, kc_block)),
            pl.BlockSpec((1, lc, kvw), lambda bi: (bi, 0, kc_block + 1)),
            _resident((l, LANES), lambda bi: (0, 0)),
            _resident((l, LANES), lambda bi: (0, 0)),
        ],
        out_specs=pl.BlockSpec((1, l, qw), lambda bi: (bi, 0, 0)),
        out_shape=jax.ShapeDtypeStruct((b, l, qw), BF16),
        scratch_shapes=[pltpu.VMEM((l, qw), BF16),
                        pltpu.VMEM((l, qw), BF16),
                        pltpu.VMEM((l + 2 * BLOCK, kvw), BF16),
                        pltpu.VMEM((l + 2 * BLOCK, kvw), BF16),
                        pltpu.VMEM((l // BLOCK + 2, kvw, BLOCK), BF16),
                        pltpu.VMEM((lc, kvw), BF16),
                        pltpu.VMEM((kvw, lc), BF16)],
        compiler_params=_cparams(("parallel",)),
        name="latent_attention",
    )(sink, p, p, p, pc, pc, cos_t, sin_t)


def _ctx_attn_kernel(sink_ref, q_ref, kc_ref, vc_ref, o_ref, *, layer):
    outs = []
    for h in range(N_HEADS):
        kv = h // HEADS_PER_KV
        hc = slice(kv * HEAD_DIM, (kv + 1) * HEAD_DIM)
        qh = q_ref[0, :, h * HEAD_DIM:(h + 1) * HEAD_DIM] * SCALE
        s = _nt_dot(qh, kc_ref[0, :, hc])
        sk = sink_ref[layer, h]
        m = jnp.maximum(jnp.max(s, axis=-1, keepdims=True), sk)
        e = jnp.exp(s - m)
        den = jnp.sum(e, axis=-1, keepdims=True) + jnp.exp(sk - m)
        o = jnp.dot(e.astype(BF16), vc_ref[0, :, hc], preferred_element_type=F32)
        outs.append(o / den)
    o_ref[0] = jnp.concatenate(outs, axis=-1).astype(o_ref.dtype)


def _ctx_attn_call(pc, sink, layer):
    b, lc, _ = pc.shape
    qw = N_HEADS * HEAD_DIM
    kvw = N_KV_HEADS * HEAD_DIM
    kern = functools.partial(_ctx_attn_kernel, layer=layer)
    return pl.pallas_call(
        kern,
        grid=(b,),
        in_specs=[
            pl.BlockSpec(memory_space=pltpu.SMEM),
            pl.BlockSpec((1, lc, qw), lambda bi: (bi, 0, 2048 // qw)),
            pl.BlockSpec((1, lc, kvw), lambda bi: (bi, 0, 2560 // kvw)),
            pl.BlockSpec((1, lc, kvw), lambda bi: (bi, 0, 2560 // kvw + 1)),
        ],
        out_specs=pl.BlockSpec((1, lc, qw), lambda bi: (bi, 0, 0)),
        out_shape=jax.ShapeDtypeStruct((b, lc, qw), BF16),
        compiler_params=_cparams(("parallel",)),
        name="context_attention",
    )(sink, pc, pc, pc)


def _rope_tables(n_pos):
    quarter = HEAD_DIM // 4
    inv = ROPE_THETA ** (-jnp.arange(quarter, dtype=F32) / quarter)
    pos = jnp.arange(n_pos, dtype=jnp.int32)
    row = (pos // GRID_W).astype(F32)
    col = (pos % GRID_W).astype(F32)
    a_row = row[:, None] * inv[None, :]
    a_col = col[:, None] * inv[None, :]
    ang = jnp.concatenate([a_row, a_row, a_col, a_col], axis=-1)
    sign = jnp.tile(jnp.concatenate([-jnp.ones((quarter,), F32), jnp.ones((quarter,), F32)]), 2)
    reps = LANES // HEAD_DIM
    return jnp.tile(jnp.cos(ang), (1, reps)), jnp.tile(jnp.sin(ang) * sign, (1, reps))


def _gelu_tanh(x):
    k1 = -2.0 * math.sqrt(2.0 / math.pi) * LOG2E
    t = x * (k1 + (k1 * 0.044715) * (x * x))
    return x * (1.0 / (1.0 + jnp.exp2(t)))


def _chunk_mlp_kernel(u_ref, v_ref, w_ref, bias_ref, o_ref):
    n_rows = u_ref.shape[1]
    averager = jnp.full((LANES, LANES), 1.0 / LANES, BF16)
    for gi in range(u_ref.shape[2] // LANES):
        cols = slice(gi * LANES, (gi + 1) * LANES)
        u = _gelu_tanh(u_ref[0, :, cols].astype(F32))
        v = _gelu_tanh(v_ref[0, :, cols].astype(F32))
        mu = jnp.dot(v.astype(BF16), averager, preferred_element_type=F32)
        dlt = v - mu
        var = jnp.dot((dlt * dlt).astype(BF16), averager, preferred_element_type=F32)
        vn = (dlt * lax.rsqrt(var + EPS)).astype(BF16)
        w = w_ref[gi].astype(BF16)
        for n in range(n_rows // CHUNK):
            rows = slice(n * CHUNK, (n + 1) * CHUNK)
            mixed = jnp.dot(w, vn[rows], preferred_element_type=F32) + bias_ref[gi]
            o_ref[0, rows, cols] = (u[rows] * mixed).astype(o_ref.dtype)


def _chunk_mlp_call(p, w_s, bias, layer):
    b, l, _ = p.shape
    gpb = 2
    bw = gpb * LANES
    u_block = 2816 // bw
    v_block = 3328 // bw
    return pl.pallas_call(
        _chunk_mlp_kernel,
        grid=(b, N_GROUPS // gpb),
        in_specs=[
            pl.BlockSpec((1, l, bw), lambda bi, j: (bi, 0, u_block + j)),
            pl.BlockSpec((1, l, bw), lambda bi, j: (bi, 0, v_block + j)),
            pl.BlockSpec((None, gpb, CHUNK, CHUNK), lambda bi, j: (layer, j, 0, 0)),
            pl.BlockSpec((None, gpb, CHUNK, LANES), lambda bi, j: (layer, j, 0, 0)),
        ],
        out_specs=pl.BlockSpec((1, l, bw), lambda bi, j: (bi, 0, j)),
        out_shape=jax.ShapeDtypeStruct((b, l, N_GROUPS * LANES), BF16),
        compiler_params=_cparams(("parallel", "parallel")),
        name="chunk_mlp",
    )(p, p, w_s, bias)


def _out_proj_kernel(ya_ref, yb_ref, yc_ref, yd_ref, x_ref, gt_ref, w_ref, o_ref, *, tn):
    d = o_ref.shape[-1]
    kw = ya_ref.shape[-1]
    ys = (ya_ref, yb_ref, yc_ref, yd_ref)
    for j in range(d // tn):
        cols = slice(j * tn, (j + 1) * tn)
        acc = jnp.dot(ys[0][0], w_ref[0:kw, cols], preferred_element_type=F32)
        for k in range(1, 4):
            acc = acc + jnp.dot(ys[k][0], w_ref[k * kw:(k + 1) * kw, cols], preferred_element_type=F32)
        o_ref[0, :, cols] = x_ref[0, :, cols] + gt_ref[:, cols] * acc


def _out_proj_call(ys, x, mods, w_out, layer, mod_row):
    b, l, d = x.shape
    kw = ys[0].shape[-1]
    tm = min(1024, l)
    kern = functools.partial(_out_proj_kernel, tn=512)
    yspec = pl.BlockSpec((1, tm, kw), lambda bi, i: (bi, i, 0))
    return pl.pallas_call(
        kern,
        grid=(b, l // tm),
        in_specs=[yspec, yspec, yspec, yspec,
                  pl.BlockSpec((1, tm, d), lambda bi, i: (bi, i, 0)),
                  _mod_spec(layer, mod_row, GT1, d),
                  _resident((None, d, d), lambda bi, i: (layer, 0, 0))],
        out_specs=pl.BlockSpec((1, tm, d), lambda bi, i: (bi, i, 0)),
        out_shape=jax.ShapeDtypeStruct((b, l, d), F32),
        compiler_params=_cparams(("parallel", "parallel")),
        name="out_proj",
    )(*ys, x, mods, w_out)


def _swiglu_kernel(x_ref, g_ref, sc_ref, sh_ref, gt_ref, wg_hbm, wu_hbm, wd_hbm, gf_ref, o_ref,
                   h_ref, wg_buf, wu_buf, wd_buf, sem, *, tm, tf, layer, final_norm):
    n_f = wg_hbm.shape[-1] // tf
    tile = pl.program_id(0) * pl.num_programs(1) + pl.program_id(1)
    n_tiles = pl.num_programs(0) * pl.num_programs(1)
    first_chunk = tile * n_f

    def chunk_copies(f, slot):
        cols = pl.ds(pl.multiple_of(f * tf, tf), tf)
        return (pltpu.make_async_copy(wg_hbm.at[layer, :, cols], wg_buf.at[slot], sem.at[0, slot]),
                pltpu.make_async_copy(wu_hbm.at[layer, :, cols], wu_buf.at[slot], sem.at[1, slot]),
                pltpu.make_async_copy(wd_hbm.at[layer, cols, :], wd_buf.at[slot], sem.at[2, slot]))

    def start_chunk(f, slot):
        for cp in chunk_copies(f, slot):
            cp.start()

    def wait_chunk(f, slot):
        for cp in chunk_copies(f, slot):
            cp.wait()

    def gated_ffn(h, slot):
        gate = jnp.dot(h, wg_buf[slot], preferred_element_type=F32)
        up = jnp.dot(h, wu_buf[slot], preferred_element_type=F32)
        a = (gate * jax.nn.sigmoid(gate) * up).astype(BF16)
        return gt_ref[...] * jnp.dot(a, wd_buf[slot], preferred_element_type=F32)

    @pl.when(tile == 0)
    def _():
        start_chunk(0, 0)

    slot0 = lax.rem(first_chunk, 2)
    wait_chunk(0, slot0)
    start_chunk(1, 1 - slot0)
    rb = min(tm, 256)
    for r in range(tm // rb):
        rows = slice(r * rb, (r + 1) * rb)
        xr = x_ref[0, rows, :]
        ms = jnp.mean(xr * xr, axis=-1, keepdims=True)
        h = (((xr * lax.rsqrt(ms + EPS)) * g_ref[...]) * (1.0 + sc_ref[...]) + sh_ref[...]).astype(BF16)
        h_ref[rows, :] = h
        o_ref[0, rows, :] = xr + gated_ffn(h, slot0)

    def chunk_step(f, carry):
        slot = lax.rem(first_chunk + f, 2)
        wait_chunk(f, slot)

        @pl.when(jnp.logical_or(f + 1 < n_f, tile + 1 < n_tiles))
        def _():
            start_chunk(lax.rem(f + 1, n_f), 1 - slot)

        o_ref[0] += gated_ffn(h_ref[...], slot)
        return carry

    lax.fori_loop(1, n_f, chunk_step, 0)

    if final_norm:
        def body(r, carry):
            rows = pl.ds(pl.multiple_of(r * 64, 64), 64)
            xr = o_ref[0, rows, :]
            ms = jnp.mean(xr * xr, axis=-1, keepdims=True)
            o_ref[0, rows, :] = (xr * lax.rsqrt(ms + EPS)) * gf_ref[...]
            return carry

        lax.fori_loop(0, tm // 64, body, 0)


def _swiglu_call(x, mods, g, w_gate, w_up, w_down, g_final, layer, mod_row, final_norm):
    b, l, d = x.shape
    ff = w_gate.shape[-1]
    tm = min(1024, l)
    tf = 512
    assert ff % tf == 0 and ff // tf >= 2
    kern = functools.partial(_swiglu_kernel, tm=tm, tf=tf, layer=layer, final_norm=final_norm)
    hbm = pl.BlockSpec(memory_space=pl.ANY)
    return pl.pallas_call(
        kern,
        grid=(b, l // tm),
        in_specs=[
            pl.BlockSpec((1, tm, d), lambda bi, i: (bi, i, 0)),
            pl.BlockSpec((None, 1, d), lambda bi, i: (layer, 0, 0)),
            _mod_spec(layer, mod_row, SC2, d),
            _mod_spec(layer, mod_row, SH2, d),
            _mod_spec(layer, mod_row, GT2, d),
            hbm, hbm, hbm,
            pl.BlockSpec((1, d), lambda bi, i: (0, 0)),
        ],
        out_specs=pl.BlockSpec((1, tm, d), lambda bi, i: (bi, i, 0)),
        out_shape=jax.ShapeDtypeStruct((b, l, d), F32),
        scratch_shapes=[pltpu.VMEM((tm, d), BF16),
                        pltpu.VMEM((2, d, tf), BF16),
                        pltpu.VMEM((2, d, tf), BF16),
                        pltpu.VMEM((2, tf, d), BF16),
                        pltpu.SemaphoreType.DMA((3, 2))],
        compiler_params=_cparams(("arbitrary", "arbitrary")),
        name="swiglu",
    )(x, g, mods, mods, mods, w_gate, w_up, w_down, g_final)


def kernel(x, c, ctx, c_ctx, w_ada, b_ada, g_norm1, w_in, w_conv, sink, w_s, b_s, w_out, g_norm2,
           w_gate, w_up, w_down, g_final):
    batch, n_pos, d = x.shape
    n_ctx = ctx.shape[1]
    depth = w_ada.shape[0]
    cw = (d // 4) // N_GROUPS
    ctx_row = batch

    w_in_b = w_in.astype(BF16)
    w_out_b = w_out.astype(BF16)
    w_gate_b = w_gate.astype(BF16)
    w_up_b = w_up.astype(BF16)
    w_down_b = w_down.astype(BF16)
    g1 = g_norm1.reshape(depth, 1, d)
    g2 = g_norm2.reshape(depth, 1, d)
    gf = g_final.reshape(1, d)

    cs = jnp.concatenate([c, c_ctx[None, :], jnp.zeros((N_MOD_ROWS - batch - 1, d), F32)], axis=0)
    mods = _ada_call(cs, w_ada, b_ada).reshape(depth, N_MOD_ROWS, 1, 6 * d)

    cos_c, sin_c = _dft_tables(cw)
    ccsc = jnp.concatenate([cos_c, sin_c], axis=1).astype(BF16)
    use_fft = n_pos % (FFT_RADIX * 64) == 0 and n_pos // FFT_RADIX >= LANES
    if use_fft:
        ccsc_neg = jnp.concatenate([cos_c, -sin_c], axis=1).astype(BF16)
        fsub, twr, twi = _fft_tables(n_pos)
        latent_fourier = lambda pp: _fourier_fft_call(pp, ccsc_neg, fsub, twr, twi, f_block)
    else:
        cos_p, sin_p = _dft_tables(n_pos)
        cpsp = jnp.concatenate([cos_p, -sin_p], axis=1).astype(BF16)
        latent_fourier = lambda pp: _fourier_call(pp, ccsc, cpsp, f_block)
    cos_x, sin_x = _dft_tables(n_ctx)
    cpsp_ctx = jnp.concatenate([cos_x, -sin_x], axis=1).astype(BF16)
    rope_cos, rope_sin = _rope_tables(n_pos)
    mlp_bias = jnp.broadcast_to(b_s[:, :, :, None], b_s.shape + (cw,))
    f_block = 1536 // (N_GROUPS * cw)

    def flat(a):
        return a.reshape(1, -1, a.shape[-1])

    def unflat(a):
        return a.reshape(batch, n_ctx, a.shape[-1])

    xc = flat(ctx)
    for layer in range(depth):
        last = layer == depth - 1
        if last:
            w_kv = w_in_b[layer:layer + 1, :, 2560:2816]
            pc = unflat(_norm_in_call(xc, mods, g1, w_kv, layer, 0, ctx_row, tn=256))
            kc_block = 0
        else:
            pc_flat = _norm_in_call(xc, mods, g1, w_in_b, layer, layer, ctx_row, tn=768)
            pc = unflat(pc_flat)
            kc_block = 2560 // (N_KV_HEADS * HEAD_DIM)
            ys_c = (_conv_call(pc_flat, w_conv, layer, n_ctx),
                    flat(_fourier_call(pc, ccsc, cpsp_ctx, f_block)),
                    flat(_ctx_attn_call(pc, sink, layer)),
                    _chunk_mlp_call(pc_flat, w_s, mlp_bias, layer))
            xc_new = _out_proj_call(ys_c, xc, mods, w_out_b, layer, ctx_row)
            xc_next = _swiglu_call(xc_new, mods, g2, w_gate_b, w_up_b, w_down_b, gf, layer, ctx_row, False)
        p = _norm_in_call(x, mods, g1, w_in_b, layer, layer, None, tn=768)
        ys = (_conv_call(p, w_conv, layer, n_pos),
              latent_fourier(p),
              _latent_attn_call(p, pc, kc_block, sink, rope_cos, rope_sin, layer),
              _chunk_mlp_call(p, w_s, mlp_bias, layer))
        x = _out_proj_call(ys, x, mods, w_out_b, layer, None)
        x = _swiglu_call(x, mods, g2, w_gate_b, w_up_b, w_down_b, gf, layer, None, last)
        if not last:
            xc = xc_next
    return x
```

```python
import functools
import math

import jax
import jax.numpy as jnp
from jax import lax
from jax.experimental import pallas as pl
from jax.experimental.pallas import tpu as pltpu

F32 = jnp.float32
BF16 = jnp.bfloat16

EPS = 1e-6
NEG = -1e30
GRID_W = 64
HEAD_DIM = 64
N_HEADS = 8
N_KV_HEADS = 2
HEADS_PER_KV = N_HEADS // N_KV_HEADS
BLOCK = 128
CHUNK = 128
N_GROUPS = 4
ROPE_THETA = 10000.0
SCALE = HEAD_DIM ** -0.5
LOG2E = math.log2(math.e)

LANES = 128
VMEM_LIMIT_BYTES = 60 * 1024 * 1024
N_MOD_ROWS = 24

SH1, SC1, GT1, SH2, SC2, GT2 = range(6)


def _cparams(sem):
    return pltpu.CompilerParams(dimension_semantics=sem, vmem_limit_bytes=VMEM_LIMIT_BYTES)


def _resident(block_shape, index_map):
    return pl.BlockSpec(block_shape, index_map, pipeline_mode=pl.Buffered(1))


def _ada_kernel(c_ref, w_ref, b_ref, o_ref):
    c = c_ref[...]
    s = (c * jax.nn.sigmoid(c)).astype(BF16)
    o_ref[...] = jnp.dot(s, w_ref[...].astype(BF16), preferred_element_type=F32) + b_ref[...]


def _ada_call(cs, w_ada, b_ada):
    depth, d, n = w_ada.shape
    tn = 1024
    return pl.pallas_call(
        _ada_kernel,
        grid=(depth, n // tn),
        in_specs=[
            pl.BlockSpec((N_MOD_ROWS, d), lambda l, j: (0, 0)),
            pl.BlockSpec((None, d, tn), lambda l, j: (l, 0, j)),
            pl.BlockSpec((None, 1, tn), lambda l, j: (l, 0, j)),
        ],
        out_specs=pl.BlockSpec((None, N_MOD_ROWS, tn), lambda l, j: (l, 0, j)),
        out_shape=jax.ShapeDtypeStruct((depth, N_MOD_ROWS, n), F32),
        compiler_params=_cparams(("parallel", "parallel")),
        name="ada_mod",
    )(cs, w_ada, b_ada.reshape(depth, 1, n))


def _mod_spec(layer, row, chunk, d):
    if row is None:
        return pl.BlockSpec((None, None, 1, d), lambda b, *_: (layer, b, 0, chunk))
    return pl.BlockSpec((None, None, 1, d), lambda b, *_: (layer, row, 0, chunk))


def _norm_modulate_rows(x_ref, g_ref, sc_ref, sh_ref, h_ref, tm, rb):
    def body(r, carry):
        rows = pl.ds(pl.multiple_of(r * rb, rb), rb)
        xr = x_ref[0, rows, :]
        ms = jnp.mean(xr * xr, axis=-1, keepdims=True)
        y = xr * lax.rsqrt(ms + EPS)
        h = (y * g_ref[0:1, :]) * (1.0 + sc_ref[...]) + sh_ref[...]
        h_ref[rows, :] = h.astype(h_ref.dtype)
        return carry

    lax.fori_loop(0, tm // rb, body, 0)


def _norm_in_kernel(x_ref, xn_ref, g_ref, sc_ref, sh_ref, scn_ref, shn_ref, w_ref, o_ref, ha_ref, hb_ref,
                    *, tm, tn):
    t = pl.program_id(0) * pl.num_programs(1) + pl.program_id(1)
    n = o_ref.shape[-1]

    @pl.when(t == 0)
    def _():
        _norm_modulate_rows(x_ref, g_ref, sc_ref, sh_ref, ha_ref, tm, min(tm, 64))

    def step(h_cur, h_next):
        rb = min(tm, 32)
        n_rb = tm // rb
        ct = 256 if tn % 256 == 0 else tn
        n_ct = n // ct
        anchors = [None] * n_ct
        for r in range(n_rb):
            rows = slice(r * rb, (r + 1) * rb)
            xr = xn_ref[0, rows, :]
            ms = jnp.mean(xr * xr, axis=-1, keepdims=True)
            h = ((xr * lax.rsqrt(ms + EPS)) * g_ref[0:1, :]) * (1.0 + scn_ref[...]) + shn_ref[...]
            hb = h.astype(BF16)
            h_next[rows, :] = hb
            folded = hb[:, 0:LANES]
            for k in range(1, hb.shape[1] // LANES):
                folded = folded + hb[:, k * LANES:(k + 1) * LANES]
            while folded.shape[0] > 16:
                half_rows = folded.shape[0] // 2
                folded = folded[:half_rows] + folded[half_rows:]
            word = pltpu.bitcast(folded, jnp.uint32)
            zero = lax.shift_right_logical(lax.shift_right_logical(word, jnp.uint32(16)), jnp.uint32(16))
            j = r * n_ct // n_rb
            anchors[j] = zero if anchors[j] is None else anchors[j] | zero
        for j in range(n // tn):
            cols = slice(j * tn, (j + 1) * tn)
            out = jnp.dot(h_cur[...], w_ref[:, cols], preferred_element_type=F32)
            pieces = []
            for s in range(tn // ct):
                piece = out[:, s * ct:(s + 1) * ct]
                anchor = anchors[j * (tn // ct) + s]
                if anchor is not None and ct % LANES == 0:
                    zrow = anchor[0:1, :].astype(F32)
                    piece = piece + jnp.concatenate([zrow] * (ct // LANES), axis=1)
                pieces.append(piece)
            o_ref[0, :, cols] = jnp.concatenate(pieces, axis=1).astype(o_ref.dtype)

    parity = lax.rem(t, 2)

    @pl.when(parity == 0)
    def _():
        step(ha_ref, hb_ref)

    @pl.when(parity == 1)
    def _():
        step(hb_ref, ha_ref)


def _norm_in_call(x, mods, g, w, layer, w_layer, mod_row, tn):
    b, l, d = x.shape
    n = w.shape[-1]
    tm = min(512, l)
    nt = l // tm
    kern = functools.partial(_norm_in_kernel, tm=tm, tn=tn)

    def nxt(bi, i):
        t = jnp.minimum(bi * nt + i + 1, b * nt - 1)
        return t // nt, t % nt

    def mod_next(chunk):
        if mod_row is None:
            return pl.BlockSpec((None, None, 1, d), lambda bi, i: (layer, nxt(bi, i)[0], 0, chunk))
        return _mod_spec(layer, mod_row, chunk, d)

    return pl.pallas_call(
        kern,
        grid=(b, nt),
        in_specs=[
            pl.BlockSpec((1, tm, d), lambda bi, i: (0, 0, 0)),
            pl.BlockSpec((1, tm, d), lambda bi, i: nxt(bi, i) + (0,)),
            pl.BlockSpec((None, g.shape[1], d), lambda bi, i: (layer, 0, 0)),
            _mod_spec(layer, mod_row, SC1, d),
            _mod_spec(layer, mod_row, SH1, d),
            mod_next(SC1),
            mod_next(SH1),
            _resident((None, d, n), lambda bi, i: (w_layer, 0, 0)),
        ],
        out_specs=pl.BlockSpec((1, tm, n), lambda bi, i: (bi, i, 0)),
        out_shape=jax.ShapeDtypeStruct((b, l, n), BF16),
        scratch_shapes=[pltpu.VMEM((tm, d), BF16), pltpu.VMEM((tm, d), BF16)],
        compiler_params=_cparams(("arbitrary", "arbitrary")),
        name="norm_in",
    )(x, x, g, mods, mods, mods, mods, w)


def _conv_kernel(x_ref, gb_ref, gc_ref, w_ref, o_ref, *, seq_len):
    n_rows = x_ref.shape[1]
    pos = lax.broadcasted_iota(jnp.int32, (n_rows, LANES), 0) % seq_len
    first = pos == 0
    last = pos == seq_len - 1
    for s in range(x_ref.shape[2] // LANES):
        cols = slice(s * LANES, (s + 1) * LANES)
        z = gc_ref[0, :, cols].astype(F32) * x_ref[0, :, cols].astype(F32)
        z_prev = jnp.where(first, 0.0, pltpu.roll(z, 1, 0))
        z_next = jnp.where(last, 0.0, pltpu.roll(z, n_rows - 1, 0))
        y = z_prev * w_ref[0:1, cols] + z * w_ref[1:2, cols] + z_next * w_ref[2:3, cols]
        o_ref[0, :, cols] = (gb_ref[0, :, cols].astype(F32) * y).astype(o_ref.dtype)


def _conv_call(p, w_conv, layer, seq_len):
    b, l, _ = p.shape
    width = w_conv.shape[-1]
    blk = lambda j: pl.BlockSpec((1, l, width), lambda bi: (bi, 0, j))
    return pl.pallas_call(
        functools.partial(_conv_kernel, seq_len=seq_len),
        grid=(b,),
        in_specs=[blk(0), blk(1), blk(2),
                  pl.BlockSpec((None, 3, width), lambda bi: (layer, 0, 0))],
        out_specs=pl.BlockSpec((1, l, width), lambda bi: (bi, 0, 0)),
        out_shape=jax.ShapeDtypeStruct((b, l, width), BF16),
        compiler_params=_cparams(("parallel",)),
        name="short_conv",
    )(p, p, p, w_conv)


def _fourier_kernel(z_ref, ccsc_ref, cpsp_ref, o_ref, rhs_ref, *, out_scale):
    n_pos = z_ref.shape[1]
    cw = ccsc_ref.shape[0]
    for g in range(N_GROUPS):
        cols = slice(g * cw, (g + 1) * cw)
        ab = jnp.dot(z_ref[0, :, cols], ccsc_ref[...], preferred_element_type=F32)
        rhs_ref[0:n_pos, cols] = ab[:, :cw].astype(BF16)
        rhs_ref[n_pos:2 * n_pos, cols] = ab[:, cw:].astype(BF16)
    out = jnp.dot(cpsp_ref[...], rhs_ref[...], preferred_element_type=F32)
    o_ref[0] = (out * out_scale).astype(o_ref.dtype)


def _fourier_call(p, ccsc, cpsp, col_block):
    b, l, _ = p.shape
    cw = ccsc.shape[0]
    width = N_GROUPS * cw
    kern = functools.partial(_fourier_kernel, out_scale=1.0 / math.sqrt(l * cw))
    return pl.pallas_call(
        kern,
        grid=(b,),
        in_specs=[
            pl.BlockSpec((1, l, width), lambda bi: (bi, 0, col_block)),
            _resident((cw, 2 * cw), lambda bi: (0, 0)),
            _resident((l, 2 * l), lambda bi: (0, 0)),
        ],
        out_specs=pl.BlockSpec((1, l, width), lambda bi: (bi, 0, 0)),
        out_shape=jax.ShapeDtypeStruct((b, l, width), BF16),
        scratch_shapes=[pltpu.VMEM((2 * l, width), BF16)],
        compiler_params=_cparams(("parallel",)),
        name="fourier_mix",
    )(p, ccsc, cpsp)


def _dft_tables(n):
    k = jnp.arange(n, dtype=jnp.int32)
    ang = ((k[:, None] * k[None, :]) % n).astype(F32) * (2.0 * math.pi / n)
    return jnp.cos(ang), jnp.sin(ang)


FFT_RADIX = 8


def _fourier_fft_kernel(z_ref, ccsc_ref, fsub_ref, twr_ref, twi_ref, o_ref, w_ref, g_ref, *, out_scale):
    n_pos = z_ref.shape[1]
    m = n_pos // FFT_RADIX
    cw = ccsc_ref.shape[0]
    c = math.sqrt(0.5)
    rb = 64
    gpp = 2
    for gp in range(N_GROUPS // gpp):
        groups = range(gp * gpp, (gp + 1) * gpp)
        pcols = slice(gp * gpp * cw, (gp + 1) * gpp * cw)
        for g in groups:
            ab = jnp.dot(z_ref[0, :, g * cw:(g + 1) * cw], ccsc_ref[...], preferred_element_type=F32)
            w_ref[g] = ab[:, :cw]
            w_ref[N_GROUPS + g] = ab[:, cw:]
        for n2 in range(FFT_RADIX):
            sub = pl.ds(n2, m, stride=FFT_RADIX)
            rhs = jnp.concatenate(
                [jnp.concatenate([w_ref[part * N_GROUPS + g, sub, :] for g in groups], axis=1)
                 for part in range(2)], axis=0).astype(BF16)
            g_ref[n2, :, pcols] = jnp.dot(fsub_ref[...], rhs, preferred_element_type=F32)
        for r in range(m // rb):
            rows = slice(r * rb, (r + 1) * rb)
            irows = slice(m + r * rb, m + (r + 1) * rb)
            for g in groups:
                cols = slice(g * cw, (g + 1) * cw)
                hr, hi = [g_ref[0, rows, cols]], [g_ref[0, irows, cols]]
                for n2 in range(1, FFT_RADIX):
                    gr, gi = g_ref[n2, rows, cols], g_ref[n2, irows, cols]
                    tr, ti = twr_ref[n2, rows, :], twi_ref[n2, rows, :]
                    hr.append(tr * gr - ti * gi)
                    hi.append(tr * gi + ti * gr)
                a0, a1 = hr[0] + hr[4], hr[0] - hr[4]
                a2 = hr[2] + hr[6]
                b0, b1 = hr[1] + hr[5], hr[1] - hr[5]
                b2, b3 = hr[3] + hr[7], hr[3] - hr[7]
                d1, d2, d3 = hi[1] - hi[5], hi[2] - hi[6], hi[3] - hi[7]
                e0, e1 = a0 + a2, b0 + b2
                odd_r = c * (b1 - b3)
                odd_i = c * (d1 + d3)
                p1, p3 = a1 + odd_r, a1 - odd_r
                p2 = a0 - a2
                q1, q3 = odd_i + d2, odd_i - d2
                q2 = (hi[1] + hi[5]) - (hi[3] + hi[7])
                ys = (e0 + e1, p1 + q1, p2 + q2, p3 + q3, e0 - e1, p3 - q3, p2 - q2, p1 - q1)
                for k2, y in enumerate(ys):
                    orow = slice(k2 * m + r * rb, k2 * m + (r + 1) * rb)
                    o_ref[0, orow, cols] = (y * out_scale).astype(o_ref.dtype)


def _fourier_fft_call(p, ccsc_neg, fsub, twr, twi, col_block):
    b, l, _ = p.shape
    cw = ccsc_neg.shape[0]
    width = N_GROUPS * cw
    m = l // FFT_RADIX
    kern = functools.partial(_fourier_fft_kernel, out_scale=1.0 / math.sqrt(l * cw))
    return pl.pallas_call(
        kern,
        grid=(b,),
        in_specs=[
            pl.BlockSpec((1, l, width), lambda bi: (bi, 0, col_block)),
            _resident((cw, 2 * cw), lambda bi: (0, 0)),
            _resident((2 * m, 2 * m), lambda bi: (0, 0)),
            _resident((FFT_RADIX, m, LANES), lambda bi: (0, 0, 0)),
            _resident((FFT_RADIX, m, LANES), lambda bi: (0, 0, 0)),
        ],
        out_specs=pl.BlockSpec((1, l, width), lambda bi: (bi, 0, 0)),
        out_shape=jax.ShapeDtypeStruct((b, l, width), BF16),
        scratch_shapes=[pltpu.VMEM((2 * N_GROUPS, l, cw), F32),
                        pltpu.VMEM((FFT_RADIX, 2 * m, width), F32)],
        compiler_params=_cparams(("parallel",)),
        name="fourier_fft",
    )(p, ccsc_neg, fsub, twr, twi)


def _fft_tables(n_pos):
    m = n_pos // FFT_RADIX
    cos_m, sin_m = _dft_tables(m)
    fsub = jnp.concatenate([jnp.concatenate([cos_m, sin_m], axis=1),
                            jnp.concatenate([-sin_m, cos_m], axis=1)], axis=0).astype(BF16)
    n2 = jnp.arange(FFT_RADIX, dtype=jnp.int32)[:, None]
    k1 = jnp.arange(m, dtype=jnp.int32)[None, :]
    ang = ((n2 * k1) % n_pos).astype(F32) * (2.0 * math.pi / n_pos)
    bcast = lambda t: jnp.broadcast_to(t[:, :, None], (FFT_RADIX, m, LANES))
    return fsub, bcast(jnp.cos(ang)), bcast(-jnp.sin(ang))


def _nt_dot(a, b):
    return lax.dot_general(a, b, (((1,), (1,)), ((), ())), preferred_element_type=F32)


def _latent_attn_kernel(sink_ref, q_ref, k_ref, v_ref, kc_ref, vc_ref, cos_ref, sin_ref, o_ref,
                        qlo_ref, qhi_ref, kpad_ref, kswp_ref, vt_ref, kcs_ref, vct_ref, *, layer):
    n_pos = q_ref.shape[1]
    n_blk = n_pos // BLOCK
    n_ctx = kc_ref.shape[1]
    half = LANES // 2

    lane = lax.broadcasted_iota(jnp.int32, (BLOCK, LANES), 1)
    low16 = (lane % 32) < 16
    lo_half = lane < half

    def prep_block(r, carry):
        rows = pl.ds(pl.multiple_of(r * BLOCK, BLOCK), BLOCK)
        cos = cos_ref[rows, :]
        sin = sin_ref[rows, :]

        def rope(t):
            swapped = jnp.where(low16, pltpu.roll(t, LANES - 16, 1), pltpu.roll(t, 16, 1))
            return t * cos + swapped * sin

        for s in range(q_ref.shape[-1] // LANES):
            cols = slice(s * LANES, (s + 1) * LANES)
            t = rope(q_ref[0, rows, cols].astype(F32)) * (SCALE * LOG2E)
            qlo_ref[rows, cols] = jnp.where(lo_half, t, 0.0).astype(BF16)
            qhi_ref[rows, cols] = jnp.where(lo_half, 0.0, t).astype(BF16)
        prow = pl.ds(pl.multiple_of(r * BLOCK + BLOCK, BLOCK), BLOCK)
        tk = rope(k_ref[0, rows, :].astype(F32))
        kpad_ref[prow, :] = tk.astype(BF16)
        kswp_ref[prow, :] = pltpu.roll(tk, half, 1).astype(BF16)
        vt_ref[r + 1] = v_ref[0, rows, :].astype(F32).T.astype(BF16)
        return carry

    zeros = jnp.zeros((BLOCK, LANES), BF16)
    for ref in (kpad_ref, kswp_ref):
        ref[0:BLOCK, :] = zeros
        ref[n_pos + BLOCK:n_pos + 2 * BLOCK, :] = zeros
    vt_ref[0] = zeros
    vt_ref[n_blk + 1] = zeros
    kcs_ref[...] = pltpu.roll(kc_ref[0].astype(F32), half, 1).astype(BF16)
    for cb in range(n_ctx // BLOCK):
        crow = slice(cb * BLOCK, (cb + 1) * BLOCK)
        vct_ref[:, crow] = vc_ref[0, crow, :].astype(F32).T.astype(BF16)
    lax.fori_loop(0, n_blk, prep_block, 0, unroll=4)

    head_order = (0, 2, 5, 7, 1, 3, 4, 6)
    col_of = {h: i for i, h in enumerate(head_order)}
    n_col = N_HEADS * BLOCK
    kj = lax.broadcasted_iota(jnp.int32, (3 * BLOCK, BLOCK), 0)
    qi = lax.broadcasted_iota(jnp.int32, (3 * BLOCK, BLOCK), 1)
    in_window = (kj >= qi) & (kj <= qi + 2 * BLOCK)
    sk = jnp.concatenate([jnp.full((1, BLOCK), sink_ref[layer, h] * LOG2E, F32) for h in head_order], axis=1)
    neg = NEG * LOG2E

    def slab(ref, rows, s):
        return ref[rows, s * LANES:(s + 1) * LANES]

    def attn_block(n, carry):
        rows = pl.ds(pl.multiple_of(n * BLOCK, BLOCK), BLOCK)
        band = pl.ds(pl.multiple_of(n * BLOCK, BLOCK), 3 * BLOCK)
        valid = in_window & (kj >= (1 - n) * BLOCK) & (kj < n_pos - (n - 1) * BLOCK)

        def masked(s):
            return jnp.concatenate([jnp.where(valid, s[:, i * BLOCK:(i + 1) * BLOCK], neg)
                                    for i in range(s.shape[1] // BLOCK)], axis=1)

        rhs_nat = jnp.concatenate([slab(qlo_ref, rows, 0), slab(qlo_ref, rows, 1),
                                   slab(qhi_ref, rows, 2), slab(qhi_ref, rows, 3)], axis=0)
        rhs_swp = jnp.concatenate([slab(qhi_ref, rows, 0), slab(qhi_ref, rows, 1),
                                   slab(qlo_ref, rows, 2), slab(qlo_ref, rows, 3)], axis=0)
        s_loc = jnp.concatenate([masked(_nt_dot(kpad_ref[band, :], rhs_nat)),
                                 masked(_nt_dot(kswp_ref[band, :], rhs_swp))], axis=1)
        s_ctx = jnp.concatenate([_nt_dot(kc_ref[0], rhs_nat),
                                 _nt_dot(kcs_ref[...], rhs_swp)], axis=1)
        m = jnp.maximum(jnp.maximum(jnp.max(s_loc, axis=0, keepdims=True),
                                    jnp.max(s_ctx, axis=0, keepdims=True)), sk)
        e_loc = jnp.exp2(s_loc - m)
        e_ctx = jnp.exp2(s_ctx - m)
        inv_den = 1.0 / (jnp.sum(e_loc, axis=0, keepdims=True) + jnp.sum(e_ctx, axis=0, keepdims=True)
                         + jnp.exp2(sk - m))
        e_loc = e_loc.astype(BF16)
        e_ctx = e_ctx.astype(BF16)
        vt_band = jnp.concatenate([vt_ref[n], vt_ref[n + 1], vt_ref[n + 2]], axis=1)
        o_t = {}
        for g in range(N_KV_HEADS):
            heads = [4 * g + j for j in range(HEADS_PER_KV)]
            pick = lambda arr: jnp.concatenate(
                [arr[:, col_of[h] * BLOCK:(col_of[h] + 1) * BLOCK] for h in heads], axis=1)
            vrows = slice(g * HEAD_DIM, (g + 1) * HEAD_DIM)
            o = (jnp.dot(vt_band[vrows], pick(e_loc), preferred_element_type=F32)
                 + jnp.dot(vct_ref[vrows, :], pick(e_ctx), preferred_element_type=F32)) * pick(inv_den)
            for j, h in enumerate(heads):
                o_t[h] = o[:, j * BLOCK:(j + 1) * BLOCK]
        slabs = [jnp.concatenate([o_t[2 * s], o_t[2 * s + 1]], axis=0).T for s in range(N_HEADS // 2)]
        o_ref[0, rows, :] = jnp.concatenate(slabs, axis=1).astype(o_ref.dtype)
        return carry

    lax.fori_loop(0, n_blk, attn_block, 0, unroll=4)


def _latent_attn_call(p, pc, kc_block, sink, cos_t, sin_t, layer):
    b, l, _ = p.shape
    lc = pc.shape[1]
    qw = N_HEADS * HEAD_DIM
    kvw = N_KV_HEADS * HEAD_DIM
    q_block = 2048 // qw
    k_block = 2560 // kvw
    kern = functools.partial(_latent_attn_kernel, layer=layer)
    return pl.pallas_call(
        kern,
        grid=(b,),
        in_specs=[
            pl.BlockSpec(memory_space=pltpu.SMEM),
            pl.BlockSpec((1, l, qw), lambda bi: (bi, 0, q_block)),
            pl.BlockSpec((1, l, kvw), lambda bi: (bi, 0, k_block)),
            pl.BlockSpec((1, l, kvw), lambda bi: (bi, 0, k_block + 1)),
            pl.BlockSpec((1, lc, kvw), lambda bi: (bi, 0, kc_block)),
            pl.BlockSpec((1, lc, kvw), lambda bi: (bi, 0, kc_block + 1)),
            _resident((l, LANES), lambda bi: (0, 0)),
            _resident((l, LANES), lambda bi: (0, 0)),
        ],
        out_specs=pl.BlockSpec((1, l, qw), lambda bi: (bi, 0, 0)),
        out_shape=jax.ShapeDtypeStruct((b, l, qw), BF16),
        scratch_shapes=[pltpu.VMEM((l, qw), BF16),
                        pltpu.VMEM((l, qw), BF16),
                        pltpu.VMEM((l + 2 * BLOCK, kvw), BF16),
                        pltpu.VMEM((l + 2 * BLOCK, kvw), BF16),
                        pltpu.VMEM((l // BLOCK + 2, kvw, BLOCK), BF16),
                        pltpu.VMEM((lc, kvw), BF16),
                        pltpu.VMEM((kvw, lc), BF16)],
        compiler_params=_cparams(("parallel",)),
        name="latent_attention",
    )(sink, p, p, p, pc, pc, cos_t, sin_t)


def _ctx_attn_kernel(sink_ref, q_ref, kc_ref, vc_ref, o_ref, *, layer):
    outs = []
    for h in range(N_HEADS):
        kv = h // HEADS_PER_KV
        hc = slice(kv * HEAD_DIM, (kv + 1) * HEAD_DIM)
        qh = q_ref[0, :, h * HEAD_DIM:(h + 1) * HEAD_DIM] * SCALE
        s = _nt_dot(qh, kc_ref[0, :, hc])
        sk = sink_ref[layer, h]
        m = jnp.maximum(jnp.max(s, axis=-1, keepdims=True), sk)
        e = jnp.exp(s - m)
        den = jnp.sum(e, axis=-1, keepdims=True) + jnp.exp(sk - m)
        o = jnp.dot(e.astype(BF16), vc_ref[0, :, hc], preferred_element_type=F32)
        outs.append(o / den)
    o_ref[0] = jnp.concatenate(outs, axis=-1).astype(o_ref.dtype)


def _ctx_attn_call(pc, sink, layer):
    b, lc, _ = pc.shape
    qw = N_HEADS * HEAD_DIM
    kvw = N_KV_HEADS * HEAD_DIM
    kern = functools.partial(_ctx_attn_kernel, layer=layer)
    return pl.pallas_call(
        kern,
        grid=(b,),
        in_specs=[
            pl.BlockSpec(memory_space=pltpu.SMEM),
            pl.BlockSpec((1, lc, qw), lambda bi: (bi, 0, 2048 // qw)),
            pl.BlockSpec((1, lc, kvw), lambda bi: (bi, 0, 2560 // kvw)),
            pl.BlockSpec((1, lc, kvw), lambda bi: (bi, 0, 2560 // kvw + 1)),
        ],
        out_specs=pl.BlockSpec((1, lc, qw), lambda bi: (bi, 0, 0)),
        out_shape=jax.ShapeDtypeStruct((b, lc, qw), BF16),
        compiler_params=_cparams(("parallel",)),
        name="context_attention",
    )(sink, pc, pc, pc)


def _rope_tables(n_pos):
    quarter = HEAD_DIM // 4
    inv = ROPE_THETA ** (-jnp.arange(quarter, dtype=F32) / quarter)
    pos = jnp.arange(n_pos, dtype=jnp.int32)
    row = (pos // GRID_W).astype(F32)
    col = (pos % GRID_W).astype(F32)
    a_row = row[:, None] * inv[None, :]
    a_col = col[:, None] * inv[None, :]
    ang = jnp.concatenate([a_row, a_row, a_col, a_col], axis=-1)
    sign = jnp.tile(jnp.concatenate([-jnp.ones((quarter,), F32), jnp.ones((quarter,), F32)]), 2)
    reps = LANES // HEAD_DIM
    return jnp.tile(jnp.cos(ang), (1, reps)), jnp.tile(jnp.sin(ang) * sign, (1, reps))


def _gelu_tanh(x):
    k1 = -2.0 * math.sqrt(2.0 / math.pi) * LOG2E
    t = x * (k1 + (k1 * 0.044715) * (x * x))
    return x * (1.0 / (1.0 + jnp.exp2(t)))


def _chunk_mlp_kernel(u_ref, v_ref, w_ref, bias_ref, o_ref):
    n_rows = u_ref.shape[1]
    averager = jnp.full((LANES, LANES), 1.0 / LANES, BF16)
    for gi in range(u_ref.shape[2] // LANES):
        cols = slice(gi * LANES, (gi + 1) * LANES)
        u = _gelu_tanh(u_ref[0, :, cols].astype(F32))
        v = _gelu_tanh(v_ref[0, :, cols].astype(F32))
        mu = jnp.dot(v.astype(BF16), averager, preferred_element_type=F32)
        dlt = v - mu
        var = jnp.dot((dlt * dlt).astype(BF16), averager, preferred_element_type=F32)
        vn = (dlt * lax.rsqrt(var + EPS)).astype(BF16)
        w = w_ref[gi].astype(BF16)
        for n in range(n_rows // CHUNK):
            rows = slice(n * CHUNK, (n + 1) * CHUNK)
            mixed = jnp.dot(w, vn[rows], preferred_element_type=F32) + bias_ref[gi]
            o_ref[0, rows, cols] = (u[rows] * mixed).astype(o_ref.dtype)


def _chunk_mlp_call(p, w_s, bias, layer):
    b, l, _ = p.shape
    gpb = 2
    bw = gpb * LANES
    u_block = 2816 // bw
    v_block = 3328 // bw
    return pl.pallas_call(
        _chunk_mlp_kernel,
        grid=(b, N_GROUPS // gpb),
        in_specs=[
            pl.BlockSpec((1, l, bw), lambda bi, j: (bi, 0, u_block + j)),
            pl.BlockSpec((1, l, bw), lambda bi, j: (bi, 0, v_block + j)),
            pl.BlockSpec((None, gpb, CHUNK, CHUNK), lambda bi, j: (layer, j, 0, 0)),
            pl.BlockSpec((None, gpb, CHUNK, LANES), lambda bi, j: (layer, j, 0, 0)),
        ],
        out_specs=pl.BlockSpec((1, l, bw), lambda bi, j: (bi, 0, j)),
        out_shape=jax.ShapeDtypeStruct((b, l, N_GROUPS * LANES), BF16),
        compiler_params=_cparams(("parallel", "parallel")),
        name="chunk_mlp",
    )(p, p, w_s, bias)


def _out_proj_kernel(ya_ref, yb_ref, yc_ref, yd_ref, x_ref, gt_ref, w_ref, o_ref, *, tn):
    d = o_ref.shape[-1]
    kw = ya_ref.shape[-1]
    ys = (ya_ref, yb_ref, yc_ref, yd_ref)
    for j in range(d // tn):
        cols = slice(j * tn, (j + 1) * tn)
        acc = jnp.dot(ys[0][0], w_ref[0:kw, cols], preferred_element_type=F32)
        for k in range(1, 4):
            acc = acc + jnp.dot(ys[k][0], w_ref[k * kw:(k + 1) * kw, cols], preferred_element_type=F32)
        o_ref[0, :, cols] = x_ref[0, :, cols] + gt_ref[:, cols] * acc


def _out_proj_call(ys, x, mods, w_out, layer, mod_row):
    b, l, d = x.shape
    kw = ys[0].shape[-1]
    tm = min(1024, l)
    kern = functools.partial(_out_proj_kernel, tn=512)
    yspec = pl.BlockSpec((1, tm, kw), lambda bi, i: (bi, i, 0))
    return pl.pallas_call(
        kern,
        grid=(b, l // tm),
        in_specs=[yspec, yspec, yspec, yspec,
                  pl.BlockSpec((1, tm, d), lambda bi, i: (bi, i, 0)),
                  _mod_spec(layer, mod_row, GT1, d),
                  _resident((None, d, d), lambda bi, i: (layer, 0, 0))],
        out_specs=pl.BlockSpec((1, tm, d), lambda bi, i: (bi, i, 0)),
        out_shape=jax.ShapeDtypeStruct((b, l, d), F32),
        compiler_params=_cparams(("parallel", "parallel")),
        name="out_proj",
    )(*ys, x, mods, w_out)


def _swiglu_kernel(x_ref, g_ref, sc_ref, sh_ref, gt_ref, wg_hbm, wu_hbm, wd_hbm, gf_ref, o_ref,
                   h_ref, wg_buf, wu_buf, wd_buf, sem, *, tm, tf, layer, final_norm):
    n_f = wg_hbm.shape[-1] // tf
    tile = pl.program_id(0) * pl.num_programs(1) + pl.program_id(1)
    n_tiles = pl.num_programs(0) * pl.num_programs(1)
    first_chunk = tile * n_f

    def chunk_copies(f, slot):
        cols = pl.ds(pl.multiple_of(f * tf, tf), tf)
        return (pltpu.make_async_copy(wg_hbm.at[layer, :, cols], wg_buf.at[slot], sem.at[0, slot]),
                pltpu.make_async_copy(wu_hbm.at[layer, :, cols], wu_buf.at[slot], sem.at[1, slot]),
                pltpu.make_async_copy(wd_hbm.at[layer, cols, :], wd_buf.at[slot], sem.at[2, slot]))

    def start_chunk(f, slot):
        for cp in chunk_copies(f, slot):
            cp.start()

    def wait_chunk(f, slot):
        for cp in chunk_copies(f, slot):
            cp.wait()

    def gated_ffn(h, slot):
        gate = jnp.dot(h, wg_buf[slot], preferred_element_type=F32)
        up = jnp.dot(h, wu_buf[slot], preferred_element_type=F32)
        a = (gate * jax.nn.sigmoid(gate) * up).astype(BF16)
        return gt_ref[...] * jnp.dot(a, wd_buf[slot], preferred_element_type=F32)

    @pl.when(tile == 0)
    def _():
        start_chunk(0, 0)

    slot0 = lax.rem(first_chunk, 2)
    wait_chunk(0, slot0)
    start_chunk(1, 1 - slot0)
    rb = min(tm, 256)
    for r in range(tm // rb):
        rows = slice(r * rb, (r + 1) * rb)
        xr = x_ref[0, rows, :]
        ms = jnp.mean(xr * xr, axis=-1, keepdims=True)
        h = (((xr * lax.rsqrt(ms + EPS)) * g_ref[0:1, :]) * (1.0 + sc_ref[...]) + sh_ref[...]).astype(BF16)
        h_ref[rows, :] = h
        o_ref[0, rows, :] = xr + gated_ffn(h, slot0)

    def chunk_step(f, carry):
        slot = lax.rem(first_chunk + f, 2)
        wait_chunk(f, slot)

        @pl.when(jnp.logical_or(f + 1 < n_f, tile + 1 < n_tiles))
        def _():
            start_chunk(lax.rem(f + 1, n_f), 1 - slot)

        o_ref[0] += gated_ffn(h_ref[...], slot)
        return carry

    lax.fori_loop(1, n_f, chunk_step, 0)

    if final_norm:
        def body(r, carry):
            rows = pl.ds(pl.multiple_of(r * 64, 64), 64)
            xr = o_ref[0, rows, :]
            ms = jnp.mean(xr * xr, axis=-1, keepdims=True)
            o_ref[0, rows, :] = (xr * lax.rsqrt(ms + EPS)) * gf_ref[0:1, :]
            return carry

        lax.fori_loop(0, tm // 64, body, 0)


def _swiglu_call(x, mods, g, w_gate, w_up, w_down, g_final, layer, mod_row, final_norm):
    b, l, d = x.shape
    ff = w_gate.shape[-1]
    tm = min(1024, l)
    tf = 512
    assert ff % tf == 0 and ff // tf >= 2
    kern = functools.partial(_swiglu_kernel, tm=tm, tf=tf, layer=layer, final_norm=final_norm)
    hbm = pl.BlockSpec(memory_space=pl.ANY)
    return pl.pallas_call(
        kern,
        grid=(b, l // tm),
        in_specs=[
            pl.BlockSpec((1, tm, d), lambda bi, i: (bi, i, 0)),
            pl.BlockSpec((None, g.shape[1], d), lambda bi, i: (layer, 0, 0)),
            _mod_spec(layer, mod_row, SC2, d),
            _mod_spec(layer, mod_row, SH2, d),
            _mod_spec(layer, mod_row, GT2, d),
            hbm, hbm, hbm,
            pl.BlockSpec(g_final.shape, lambda bi, i: (0, 0)),
        ],
        out_specs=pl.BlockSpec((1, tm, d), lambda bi, i: (bi, i, 0)),
        out_shape=jax.ShapeDtypeStruct((b, l, d), F32),
        scratch_shapes=[pltpu.VMEM((tm, d), BF16),
                        pltpu.VMEM((2, d, tf), BF16),
                        pltpu.VMEM((2, d, tf), BF16),
                        pltpu.VMEM((2, tf, d), BF16),
                        pltpu.SemaphoreType.DMA((3, 2))],
        compiler_params=_cparams(("arbitrary", "arbitrary")),
        name="swiglu",
    )(x, g, mods, mods, mods, w_gate, w_up, w_down, g_final)


def kernel(x, c, ctx, c_ctx, w_ada, b_ada, g_norm1, w_in, w_conv, sink, w_s, b_s, w_out, g_norm2,
           w_gate, w_up, w_down, g_final):
    batch, n_pos, d = x.shape
    n_ctx = ctx.shape[1]
    depth = w_ada.shape[0]
    cw = (d // 4) // N_GROUPS
    ctx_row = batch

    w_in_b = w_in.astype(BF16)
    w_out_b = w_out.astype(BF16)
    w_gate_b = w_gate.astype(BF16)
    w_up_b = w_up.astype(BF16)
    w_down_b = w_down.astype(BF16)
    g1 = jnp.broadcast_to(g_norm1[:, None, :], (depth, 2, d))
    g2 = jnp.broadcast_to(g_norm2[:, None, :], (depth, 2, d))
    gf = jnp.broadcast_to(g_final[None, :], (2, d))

    cs = jnp.concatenate([c, c_ctx[None, :], jnp.zeros((N_MOD_ROWS - batch - 1, d), F32)], axis=0)
    mods = _ada_call(cs, w_ada, b_ada).reshape(depth, N_MOD_ROWS, 1, 6 * d)

    cos_c, sin_c = _dft_tables(cw)
    ccsc = jnp.concatenate([cos_c, sin_c], axis=1).astype(BF16)
    use_fft = n_pos % (FFT_RADIX * 64) == 0 and n_pos // FFT_RADIX >= LANES
    if use_fft:
        ccsc_neg = jnp.concatenate([cos_c, -sin_c], axis=1).astype(BF16)
        fsub, twr, twi = _fft_tables(n_pos)
        latent_fourier = lambda pp: _fourier_fft_call(pp, ccsc_neg, fsub, twr, twi, f_block)
    else:
        cos_p, sin_p = _dft_tables(n_pos)
        cpsp = jnp.concatenate([cos_p, -sin_p], axis=1).astype(BF16)
        latent_fourier = lambda pp: _fourier_call(pp, ccsc, cpsp, f_block)
    cos_x, sin_x = _dft_tables(n_ctx)
    cpsp_ctx = jnp.concatenate([cos_x, -sin_x], axis=1).astype(BF16)
    rope_cos, rope_sin = _rope_tables(n_pos)
    mlp_bias = jnp.broadcast_to(b_s[:, :, :, None], b_s.shape + (cw,))
    f_block = 1536 // (N_GROUPS * cw)

    def flat(a):
        return a.reshape(1, -1, a.shape[-1])

    def unflat(a):
        return a.reshape(batch, n_ctx, a.shape[-1])

    xc = flat(ctx)
    for layer in range(depth):
        last = layer == depth - 1
        if last:
            w_kv = w_in_b[layer:layer + 1, :, 2560:2816]
            pc = unflat(_norm_in_call(xc, mods, g1, w_kv, layer, 0, ctx_row, tn=256))
            kc_block = 0
        else:
            pc_flat = _norm_in_call(xc, mods, g1, w_in_b, layer, layer, ctx_row, tn=768)
            pc = unflat(pc_flat)
            kc_block = 2560 // (N_KV_HEADS * HEAD_DIM)
            ys_c = (_conv_call(pc_flat, w_conv, layer, n_ctx),
                    flat(_fourier_call(pc, ccsc, cpsp_ctx, f_block)),
                    flat(_ctx_attn_call(pc, sink, layer)),
                    _chunk_mlp_call(pc_flat, w_s, mlp_bias, layer))
            xc_new = _out_proj_call(ys_c, xc, mods, w_out_b, layer, ctx_row)
            xc_next = _swiglu_call(xc_new, mods, g2, w_gate_b, w_up_b, w_down_b, gf, layer, ctx_row, False)
        p = _norm_in_call(x, mods, g1, w_in_b, layer, layer, None, tn=768)
        ys = (_conv_call(p, w_conv, layer, n_pos),
              latent_fourier(p),
              _latent_attn_call(p, pc, kc_block, sink, rope_cos, rope_sin, layer),
              _chunk_mlp_call(p, w_s, mlp_bias, layer))
        x = _out_proj_call(ys, x, mods, w_out_b, layer, None)
        x = _swiglu_call(x, mods, g2, w_gate_b, w_up_b, w_down_b, gf, layer, None, last)
        if not last:
            xc = xc_next
    return x
```

```python
import functools
import math

import jax
import jax.numpy as jnp
from jax import lax
from jax.experimental import pallas as pl
from jax.experimental.pallas import tpu as pltpu

F32 = jnp.float32
BF16 = jnp.bfloat16

EPS = 1e-6
NEG = -1e30
GRID_W = 64
HEAD_DIM = 64
N_HEADS = 8
N_KV_HEADS = 2
HEADS_PER_KV = N_HEADS // N_KV_HEADS
BLOCK = 128
CHUNK = 128
N_GROUPS = 4
ROPE_THETA = 10000.0
SCALE = HEAD_DIM ** -0.5
LOG2E = math.log2(math.e)
KV_WIDTH = N_KV_HEADS * HEAD_DIM


def _proj_offsets(width):
    off = {"conv": 0, "fourier": 3 * width, "q": 4 * width, "k": 5 * width}
    off["v"] = off["k"] + KV_WIDTH
    off["u"] = off["v"] + KV_WIDTH
    off["mv"] = off["u"] + width
    return off

LANES = 128
VMEM_LIMIT_BYTES = 60 * 1024 * 1024
N_MOD_ROWS = 24

SH1, SC1, GT1, SH2, SC2, GT2 = range(6)


def _cparams(sem):
    return pltpu.CompilerParams(dimension_semantics=sem, vmem_limit_bytes=VMEM_LIMIT_BYTES)


def _resident(block_shape, index_map):
    return pl.BlockSpec(block_shape, index_map, pipeline_mode=pl.Buffered(1))


def _ada_kernel(c_ref, w_ref, b_ref, o_ref):
    c = c_ref[...]
    s = (c * jax.nn.sigmoid(c)).astype(BF16)
    o_ref[...] = jnp.dot(s, w_ref[...].astype(BF16), preferred_element_type=F32) + b_ref[...]


def _ada_call(cs, w_ada, b_ada):
    depth, d, n = w_ada.shape
    tn = 2048
    return pl.pallas_call(
        _ada_kernel,
        grid=(depth, n // tn),
        in_specs=[
            pl.BlockSpec((N_MOD_ROWS, d), lambda l, j: (0, 0)),
            pl.BlockSpec((None, d, tn), lambda l, j: (l, 0, j)),
            pl.BlockSpec((None, 1, tn), lambda l, j: (l, 0, j)),
        ],
        out_specs=pl.BlockSpec((None, N_MOD_ROWS, tn), lambda l, j: (l, 0, j)),
        out_shape=jax.ShapeDtypeStruct((depth, N_MOD_ROWS, n), F32),
        compiler_params=_cparams(("parallel", "parallel")),
        name="ada_mod",
    )(cs, w_ada, b_ada.reshape(depth, 1, n))


def _mod_spec(layer, row, chunk, d):
    if row is None:
        return pl.BlockSpec((None, None, 1, d), lambda b, *_: (layer, b, 0, chunk))
    return pl.BlockSpec((None, None, 1, d), lambda b, *_: (layer, row, 0, chunk))


def _norm_modulate_rows(x_ref, g_ref, sc_ref, sh_ref, h_ref, tm, rb):
    def body(r, carry):
        rows = pl.ds(pl.multiple_of(r * rb, rb), rb)
        xr = x_ref[0, rows, :]
        ms = jnp.mean(xr * xr, axis=-1, keepdims=True)
        y = xr * lax.rsqrt(ms + EPS)
        h = (y * g_ref[0:1, :]) * (1.0 + sc_ref[...]) + sh_ref[...]
        h_ref[rows, :] = h.astype(h_ref.dtype)
        return carry

    lax.fori_loop(0, tm // rb, body, 0)


def _norm_in_kernel(x_ref, xn_ref, g_ref, sc_ref, sh_ref, scn_ref, shn_ref, w_ref, o_ref, ha_ref, hb_ref,
                    *, tm, tn):
    t = pl.program_id(0) * pl.num_programs(1) + pl.program_id(1)
    n = o_ref.shape[-1]

    @pl.when(t == 0)
    def _():
        _norm_modulate_rows(x_ref, g_ref, sc_ref, sh_ref, ha_ref, tm, min(tm, 64))

    def step(h_cur, h_next):
        rb = min(tm, 32)
        n_rb = tm // rb
        ct = 256 if tn % 256 == 0 else tn
        n_ct = n // ct
        anchors = [None] * n_ct
        for r in range(n_rb):
            rows = slice(r * rb, (r + 1) * rb)
            xr = xn_ref[0, rows, :]
            ms = jnp.mean(xr * xr, axis=-1, keepdims=True)
            h = ((xr * lax.rsqrt(ms + EPS)) * g_ref[0:1, :]) * (1.0 + scn_ref[...]) + shn_ref[...]
            hb = h.astype(BF16)
            h_next[rows, :] = hb
            folded = hb[:, 0:LANES]
            for k in range(1, hb.shape[1] // LANES):
                folded = folded + hb[:, k * LANES:(k + 1) * LANES]
            while folded.shape[0] > 16:
                half_rows = folded.shape[0] // 2
                folded = folded[:half_rows] + folded[half_rows:]
            word = pltpu.bitcast(folded, jnp.uint32)
            zero = lax.shift_right_logical(lax.shift_right_logical(word, jnp.uint32(16)), jnp.uint32(16))
            j = r * n_ct // n_rb
            anchors[j] = zero if anchors[j] is None else anchors[j] | zero
        for j in range(n // tn):
            cols = slice(j * tn, (j + 1) * tn)
            out = jnp.dot(h_cur[...], w_ref[:, cols], preferred_element_type=F32)
            pieces = []
            for s in range(tn // ct):
                piece = out[:, s * ct:(s + 1) * ct]
                anchor = anchors[j * (tn // ct) + s]
                if anchor is not None and ct % LANES == 0:
                    zrow = anchor[0:1, :].astype(F32)
                    piece = piece + jnp.concatenate([zrow] * (ct // LANES), axis=1)
                pieces.append(piece)
            o_ref[0, :, cols] = jnp.concatenate(pieces, axis=1).astype(o_ref.dtype)

    parity = lax.rem(t, 2)

    @pl.when(parity == 0)
    def _():
        step(ha_ref, hb_ref)

    @pl.when(parity == 1)
    def _():
        step(hb_ref, ha_ref)


def _norm_in_call(x, mods, g, w, layer, w_layer, mod_row, tn):
    b, l, d = x.shape
    n = w.shape[-1]
    tm = min(512, l)
    nt = l // tm
    kern = functools.partial(_norm_in_kernel, tm=tm, tn=tn)

    def nxt(bi, i):
        t = jnp.minimum(bi * nt + i + 1, b * nt - 1)
        return t // nt, t % nt

    def mod_next(chunk):
        if mod_row is None:
            return pl.BlockSpec((None, None, 1, d), lambda bi, i: (layer, nxt(bi, i)[0], 0, chunk))
        return _mod_spec(layer, mod_row, chunk, d)

    return pl.pallas_call(
        kern,
        grid=(b, nt),
        in_specs=[
            pl.BlockSpec((1, tm, d), lambda bi, i: (0, 0, 0)),
            pl.BlockSpec((1, tm, d), lambda bi, i: nxt(bi, i) + (0,)),
            pl.BlockSpec((None, g.shape[1], d), lambda bi, i: (layer, 0, 0)),
            _mod_spec(layer, mod_row, SC1, d),
            _mod_spec(layer, mod_row, SH1, d),
            mod_next(SC1),
            mod_next(SH1),
            _resident((None, d, n), lambda bi, i: (w_layer, 0, 0)),
        ],
        out_specs=pl.BlockSpec((1, tm, n), lambda bi, i: (bi, i, 0)),
        out_shape=jax.ShapeDtypeStruct((b, l, n), BF16),
        scratch_shapes=[pltpu.VMEM((tm, d), BF16), pltpu.VMEM((tm, d), BF16)],
        compiler_params=_cparams(("arbitrary", "arbitrary")),
        name="norm_in",
    )(x, x, g, mods, mods, mods, mods, w)


def _conv_kernel(x_ref, gb_ref, gc_ref, w_ref, o_ref, *, seq_len):
    n_rows = x_ref.shape[1]
    pos = lax.broadcasted_iota(jnp.int32, (n_rows, LANES), 0) % seq_len
    first = pos == 0
    last = pos == seq_len - 1
    for s in range(x_ref.shape[2] // LANES):
        cols = slice(s * LANES, (s + 1) * LANES)
        z = gc_ref[0, :, cols].astype(F32) * x_ref[0, :, cols].astype(F32)
        z_prev = jnp.where(first, 0.0, pltpu.roll(z, 1, 0))
        z_next = jnp.where(last, 0.0, pltpu.roll(z, n_rows - 1, 0))
        y = z_prev * w_ref[0:1, cols] + z * w_ref[1:2, cols] + z_next * w_ref[2:3, cols]
        o_ref[0, :, cols] = (gb_ref[0, :, cols].astype(F32) * y).astype(o_ref.dtype)


def _conv_call(p, w_conv, layer, seq_len):
    b, l, _ = p.shape
    width = w_conv.shape[-1]
    blk = lambda j: pl.BlockSpec((1, l, width), lambda bi: (bi, 0, j))
    return pl.pallas_call(
        functools.partial(_conv_kernel, seq_len=seq_len),
        grid=(b,),
        in_specs=[blk(0), blk(1), blk(2),
                  pl.BlockSpec((None, 3, width), lambda bi: (layer, 0, 0))],
        out_specs=pl.BlockSpec((1, l, width), lambda bi: (bi, 0, 0)),
        out_shape=jax.ShapeDtypeStruct((b, l, width), BF16),
        compiler_params=_cparams(("parallel",)),
        name="short_conv",
    )(p, p, p, w_conv)


def _fourier_kernel(z_ref, ccsc_ref, cpsp_ref, o_ref, rhs_ref, *, out_scale):
    n_pos = z_ref.shape[1]
    cw = ccsc_ref.shape[0]
    for g in range(N_GROUPS):
        cols = slice(g * cw, (g + 1) * cw)
        ab = jnp.dot(z_ref[0, :, cols], ccsc_ref[...], preferred_element_type=F32)
        rhs_ref[0:n_pos, cols] = ab[:, :cw].astype(BF16)
        rhs_ref[n_pos:2 * n_pos, cols] = ab[:, cw:].astype(BF16)
    out = jnp.dot(cpsp_ref[...], rhs_ref[...], preferred_element_type=F32)
    o_ref[0] = (out * out_scale).astype(o_ref.dtype)


def _fourier_call(p, ccsc, cpsp, col_block):
    b, l, _ = p.shape
    cw = ccsc.shape[0]
    width = N_GROUPS * cw
    kern = functools.partial(_fourier_kernel, out_scale=1.0 / math.sqrt(l * cw))
    return pl.pallas_call(
        kern,
        grid=(b,),
        in_specs=[
            pl.BlockSpec((1, l, width), lambda bi: (bi, 0, col_block)),
            _resident((cw, 2 * cw), lambda bi: (0, 0)),
            _resident((l, 2 * l), lambda bi: (0, 0)),
        ],
        out_specs=pl.BlockSpec((1, l, width), lambda bi: (bi, 0, 0)),
        out_shape=jax.ShapeDtypeStruct((b, l, width), BF16),
        scratch_shapes=[pltpu.VMEM((2 * l, width), BF16)],
        compiler_params=_cparams(("parallel",)),
        name="fourier_mix",
    )(p, ccsc, cpsp)


def _dft_tables(n):
    k = jnp.arange(n, dtype=jnp.int32)
    ang = ((k[:, None] * k[None, :]) % n).astype(F32) * (2.0 * math.pi / n)
    return jnp.cos(ang), jnp.sin(ang)


FFT_RADIX = 8


def _fourier_fft_kernel(z_ref, ccsc_ref, fsub_ref, twr_ref, twi_ref, o_ref, w_ref, g_ref, *, out_scale):
    n_pos = z_ref.shape[1]
    m = n_pos // FFT_RADIX
    cw = ccsc_ref.shape[0]
    c = math.sqrt(0.5)
    rb = 64
    gpp = 2
    for gp in range(N_GROUPS // gpp):
        groups = range(gp * gpp, (gp + 1) * gpp)
        pcols = slice(gp * gpp * cw, (gp + 1) * gpp * cw)
        for g in groups:
            ab = jnp.dot(z_ref[0, :, g * cw:(g + 1) * cw], ccsc_ref[...], preferred_element_type=F32)
            w_ref[g] = ab[:, :cw]
            w_ref[N_GROUPS + g] = ab[:, cw:]
        for n2 in range(FFT_RADIX):
            sub = pl.ds(n2, m, stride=FFT_RADIX)
            rhs = jnp.concatenate(
                [jnp.concatenate([w_ref[part * N_GROUPS + g, sub, :] for g in groups], axis=1)
                 for part in range(2)], axis=0).astype(BF16)
            g_ref[n2, :, pcols] = jnp.dot(fsub_ref[...], rhs, preferred_element_type=F32)
        for r in range(m // rb):
            rows = slice(r * rb, (r + 1) * rb)
            irows = slice(m + r * rb, m + (r + 1) * rb)
            for g in groups:
                cols = slice(g * cw, (g + 1) * cw)
                hr, hi = [g_ref[0, rows, cols]], [g_ref[0, irows, cols]]
                for n2 in range(1, FFT_RADIX):
                    gr, gi = g_ref[n2, rows, cols], g_ref[n2, irows, cols]
                    tr, ti = twr_ref[n2, rows, :], twi_ref[n2, rows, :]
                    hr.append(tr * gr - ti * gi)
                    hi.append(tr * gi + ti * gr)
                a0, a1 = hr[0] + hr[4], hr[0] - hr[4]
                a2 = hr[2] + hr[6]
                b0, b1 = hr[1] + hr[5], hr[1] - hr[5]
                b2, b3 = hr[3] + hr[7], hr[3] - hr[7]
                d1, d2, d3 = hi[1] - hi[5], hi[2] - hi[6], hi[3] - hi[7]
                e0, e1 = a0 + a2, b0 + b2
                odd_r = c * (b1 - b3)
                odd_i = c * (d1 + d3)
                p1, p3 = a1 + odd_r, a1 - odd_r
                p2 = a0 - a2
                q1, q3 = odd_i + d2, odd_i - d2
                q2 = (hi[1] + hi[5]) - (hi[3] + hi[7])
                ys = (e0 + e1, p1 + q1, p2 + q2, p3 + q3, e0 - e1, p3 - q3, p2 - q2, p1 - q1)
                for k2, y in enumerate(ys):
                    orow = slice(k2 * m + r * rb, k2 * m + (r + 1) * rb)
                    o_ref[0, orow, cols] = (y * out_scale).astype(o_ref.dtype)


def _fourier_fft_call(p, ccsc_neg, fsub, twr, twi, col_block):
    b, l, _ = p.shape
    cw = ccsc_neg.shape[0]
    width = N_GROUPS * cw
    m = l // FFT_RADIX
    kern = functools.partial(_fourier_fft_kernel, out_scale=1.0 / math.sqrt(l * cw))
    return pl.pallas_call(
        kern,
        grid=(b,),
        in_specs=[
            pl.BlockSpec((1, l, width), lambda bi: (bi, 0, col_block)),
            _resident((cw, 2 * cw), lambda bi: (0, 0)),
            _resident((2 * m, 2 * m), lambda bi: (0, 0)),
            _resident((FFT_RADIX, m, LANES), lambda bi: (0, 0, 0)),
            _resident((FFT_RADIX, m, LANES), lambda bi: (0, 0, 0)),
        ],
        out_specs=pl.BlockSpec((1, l, width), lambda bi: (bi, 0, 0)),
        out_shape=jax.ShapeDtypeStruct((b, l, width), BF16),
        scratch_shapes=[pltpu.VMEM((2 * N_GROUPS, l, cw), F32),
                        pltpu.VMEM((FFT_RADIX, 2 * m, width), F32)],
        compiler_params=_cparams(("parallel",)),
        name="fourier_fft",
    )(p, ccsc_neg, fsub, twr, twi)


def _fft_tables(n_pos):
    m = n_pos // FFT_RADIX
    cos_m, sin_m = _dft_tables(m)
    fsub = jnp.concatenate([jnp.concatenate([cos_m, sin_m], axis=1),
                            jnp.concatenate([-sin_m, cos_m], axis=1)], axis=0).astype(BF16)
    n2 = jnp.arange(FFT_RADIX, dtype=jnp.int32)[:, None]
    k1 = jnp.arange(m, dtype=jnp.int32)[None, :]
    ang = ((n2 * k1) % n_pos).astype(F32) * (2.0 * math.pi / n_pos)
    bcast = lambda t: jnp.broadcast_to(t[:, :, None], (FFT_RADIX, m, LANES))
    return fsub, bcast(jnp.cos(ang)), bcast(-jnp.sin(ang))


def _nt_dot(a, b):
    return lax.dot_general(a, b, (((1,), (1,)), ((), ())), preferred_element_type=F32)


def _latent_attn_kernel(sink_ref, q_ref, k_ref, v_ref, kc_ref, vc_ref, cos_ref, sin_ref, o_ref,
                        qlo_ref, qhi_ref, kpad_ref, kswp_ref, vt_ref, kcs_ref, vct_ref, *, layer):
    n_pos = q_ref.shape[1]
    n_blk = n_pos // BLOCK
    n_ctx = kc_ref.shape[1]
    half = LANES // 2

    lane = lax.broadcasted_iota(jnp.int32, (BLOCK, LANES), 1)
    low16 = (lane % 32) < 16
    lo_half = lane < half

    def prep_block(r, carry):
        rows = pl.ds(pl.multiple_of(r * BLOCK, BLOCK), BLOCK)
        cos = cos_ref[rows, :]
        sin = sin_ref[rows, :]

        def rope(t):
            swapped = jnp.where(low16, pltpu.roll(t, LANES - 16, 1), pltpu.roll(t, 16, 1))
            return t * cos + swapped * sin

        for s in range(q_ref.shape[-1] // LANES):
            cols = slice(s * LANES, (s + 1) * LANES)
            t = rope(q_ref[0, rows, cols].astype(F32)) * (SCALE * LOG2E)
            qlo_ref[rows, cols] = jnp.where(lo_half, t, 0.0).astype(BF16)
            qhi_ref[rows, cols] = jnp.where(lo_half, 0.0, t).astype(BF16)
        prow = pl.ds(pl.multiple_of(r * BLOCK + BLOCK, BLOCK), BLOCK)
        tk = rope(k_ref[0, rows, :].astype(F32))
        kpad_ref[prow, :] = tk.astype(BF16)
        kswp_ref[prow, :] = pltpu.roll(tk, half, 1).astype(BF16)
        vt_ref[r + 1] = v_ref[0, rows, :].astype(F32).T.astype(BF16)
        return carry

    zeros = jnp.zeros((BLOCK, LANES), BF16)
    for ref in (kpad_ref, kswp_ref):
        ref[0:BLOCK, :] = zeros
        ref[n_pos + BLOCK:n_pos + 2 * BLOCK, :] = zeros
    vt_ref[0] = zeros
    vt_ref[n_blk + 1] = zeros
    kcs_ref[...] = pltpu.roll(kc_ref[0].astype(F32), half, 1).astype(BF16)
    for cb in range(n_ctx // BLOCK):
        crow = slice(cb * BLOCK, (cb + 1) * BLOCK)
        vct_ref[:, crow] = vc_ref[0, crow, :].astype(F32).T.astype(BF16)
    lax.fori_loop(0, n_blk, prep_block, 0, unroll=4)

    head_order = (0, 2, 5, 7, 1, 3, 4, 6)
    col_of = {h: i for i, h in enumerate(head_order)}
    n_col = N_HEADS * BLOCK
    kj = lax.broadcasted_iota(jnp.int32, (3 * BLOCK, BLOCK), 0)
    qi = lax.broadcasted_iota(jnp.int32, (3 * BLOCK, BLOCK), 1)
    in_window = (kj >= qi) & (kj <= qi + 2 * BLOCK)
    sk = jnp.concatenate([jnp.full((1, BLOCK), sink_ref[layer, h] * LOG2E, F32) for h in head_order], axis=1)
    neg = NEG * LOG2E

    def slab(ref, rows, s):
        return ref[rows, s * LANES:(s + 1) * LANES]

    def attn_block(n, carry):
        rows = pl.ds(pl.multiple_of(n * BLOCK, BLOCK), BLOCK)
        band = pl.ds(pl.multiple_of(n * BLOCK, BLOCK), 3 * BLOCK)
        valid = in_window & (kj >= (1 - n) * BLOCK) & (kj < n_pos - (n - 1) * BLOCK)

        def masked(s):
            return jnp.concatenate([jnp.where(valid, s[:, i * BLOCK:(i + 1) * BLOCK], neg)
                                    for i in range(s.shape[1] // BLOCK)], axis=1)

        rhs_nat = jnp.concatenate([slab(qlo_ref, rows, 0), slab(qlo_ref, rows, 1),
                                   slab(qhi_ref, rows, 2), slab(qhi_ref, rows, 3)], axis=0)
        rhs_swp = jnp.concatenate([slab(qhi_ref, rows, 0), slab(qhi_ref, rows, 1),
                                   slab(qlo_ref, rows, 2), slab(qlo_ref, rows, 3)], axis=0)
        s_loc = jnp.concatenate([masked(_nt_dot(kpad_ref[band, :], rhs_nat)),
                                 masked(_nt_dot(kswp_ref[band, :], rhs_swp))], axis=1)
        s_ctx = jnp.concatenate([_nt_dot(kc_ref[0], rhs_nat),
                                 _nt_dot(kcs_ref[...], rhs_swp)], axis=1)
        m = jnp.maximum(jnp.maximum(jnp.max(s_loc, axis=0, keepdims=True),
                                    jnp.max(s_ctx, axis=0, keepdims=True)), sk)
        e_loc = jnp.exp2(s_loc - m)
        e_ctx = jnp.exp2(s_ctx - m)
        inv_den = 1.0 / (jnp.sum(e_loc, axis=0, keepdims=True) + jnp.sum(e_ctx, axis=0, keepdims=True)
                         + jnp.exp2(sk - m))
        e_loc = e_loc.astype(BF16)
        e_ctx = e_ctx.astype(BF16)
        vt_band = jnp.concatenate([vt_ref[n], vt_ref[n + 1], vt_ref[n + 2]], axis=1)
        o_t = {}
        for g in range(N_KV_HEADS):
            heads = [4 * g + j for j in range(HEADS_PER_KV)]
            pick = lambda arr: jnp.concatenate(
                [arr[:, col_of[h] * BLOCK:(col_of[h] + 1) * BLOCK] for h in heads], axis=1)
            vrows = slice(g * HEAD_DIM, (g + 1) * HEAD_DIM)
            o = (jnp.dot(vt_band[vrows], pick(e_loc), preferred_element_type=F32)
                 + jnp.dot(vct_ref[vrows, :], pick(e_ctx), preferred_element_type=F32)) * pick(inv_den)
            for j, h in enumerate(heads):
                o_t[h] = o[:, j * BLOCK:(j + 1) * BLOCK]
        slabs = [jnp.concatenate([o_t[2 * s], o_t[2 * s + 1]], axis=0).T for s in range(N_HEADS // 2)]
        o_ref[0, rows, :] = jnp.concatenate(slabs, axis=1).astype(o_ref.dtype)
        return carry

    lax.fori_loop(0, n_blk, attn_block, 0, unroll=4)


def _latent_attn_call(p, pc, kc_block, sink, cos_t, sin_t, layer):
    b, l, _ = p.shape
    lc = pc.shape[1]
    qw = N_HEADS * HEAD_DIM
    kvw = N_KV_HEADS * HEAD_DIM
    off = _proj_offsets(qw)
    q_block = off["q"] // qw
    k_block = off["k"] // kvw
    kern = functools.partial(_latent_attn_kernel, layer=layer)
    return pl.pallas_call(
        kern,
        grid=(b,),
        in_specs=[
            pl.BlockSpec(memory_space=pltpu.SMEM),
            pl.BlockSpec((1, l, qw), lambda bi: (bi, 0, q_block)),
            pl.BlockSpec((1, l, kvw), lambda bi: (bi, 0, k_block)),
            pl.BlockSpec((1, l, kvw), lambda bi: (bi, 0, k_block + 1)),
            pl.BlockSpec((1, lc, kvw), lambda bi: (bi, 0, kc_block)),
            pl.BlockSpec((1, lc, kvw), lambda bi: (bi, 0, kc_block + 1)),
            _resident((l, LANES), lambda bi: (0, 0)),
            _resident((l, LANES), lambda bi: (0, 0)),
        ],
        out_specs=pl.BlockSpec((1, l, qw), lambda bi: (bi, 0, 0)),
        out_shape=jax.ShapeDtypeStruct((b, l, qw), BF16),
        scratch_shapes=[pltpu.VMEM((l, qw), BF16),
                        pltpu.VMEM((l, qw), BF16),
                        pltpu.VMEM((l + 2 * BLOCK, kvw), BF16),
                        pltpu.VMEM((l + 2 * BLOCK, kvw), BF16),
                        pltpu.VMEM((l // BLOCK + 2, kvw, BLOCK), BF16),
                        pltpu.VMEM((lc, kvw), BF16),
                        pltpu.VMEM((kvw, lc), BF16)],
        compiler_params=_cparams(("parallel",)),
        name="latent_attention",
    )(sink, p, p, p, pc, pc, cos_t, sin_t)


def _ctx_attn_kernel(sink_ref, q_ref, kc_ref, vc_ref, o_ref, *, layer):
    outs = []
    for h in range(N_HEADS):
        kv = h // HEADS_PER_KV
        hc = slice(kv * HEAD_DIM, (kv + 1) * HEAD_DIM)
        qh = q_ref[0, :, h * HEAD_DIM:(h + 1) * HEAD_DIM] * SCALE
        s = _nt_dot(qh, kc_ref[0, :, hc])
        sk = sink_ref[layer, h]
        m = jnp.maximum(jnp.max(s, axis=-1, keepdims=True), sk)
        e = jnp.exp(s - m)
        den = jnp.sum(e, axis=-1, keepdims=True) + jnp.exp(sk - m)
        o = jnp.dot(e.astype(BF16), vc_ref[0, :, hc], preferred_element_type=F32)
        outs.append(o / den)
    o_ref[0] = jnp.concatenate(outs, axis=-1).astype(o_ref.dtype)


def _ctx_attn_call(pc, sink, layer):
    b, lc, _ = pc.shape
    qw = N_HEADS * HEAD_DIM
    kvw = N_KV_HEADS * HEAD_DIM
    off = _proj_offsets(qw)
    kern = functools.partial(_ctx_attn_kernel, layer=layer)
    return pl.pallas_call(
        kern,
        grid=(b,),
        in_specs=[
            pl.BlockSpec(memory_space=pltpu.SMEM),
            pl.BlockSpec((1, lc, qw), lambda bi: (bi, 0, off["q"] // qw)),
            pl.BlockSpec((1, lc, kvw), lambda bi: (bi, 0, off["k"] // kvw)),
            pl.BlockSpec((1, lc, kvw), lambda bi: (bi, 0, off["v"] // kvw)),
        ],
        out_specs=pl.BlockSpec((1, lc, qw), lambda bi: (bi, 0, 0)),
        out_shape=jax.ShapeDtypeStruct((b, lc, qw), BF16),
        compiler_params=_cparams(("parallel",)),
        name="context_attention",
    )(sink, pc, pc, pc)


def _rope_tables(n_pos):
    quarter = HEAD_DIM // 4
    inv = ROPE_THETA ** (-jnp.arange(quarter, dtype=F32) / quarter)
    pos = jnp.arange(n_pos, dtype=jnp.int32)
    row = (pos // GRID_W).astype(F32)
    col = (pos % GRID_W).astype(F32)
    a_row = row[:, None] * inv[None, :]
    a_col = col[:, None] * inv[None, :]
    ang = jnp.concatenate([a_row, a_row, a_col, a_col], axis=-1)
    sign = jnp.tile(jnp.concatenate([-jnp.ones((quarter,), F32), jnp.ones((quarter,), F32)]), 2)
    reps = LANES // HEAD_DIM
    return jnp.tile(jnp.cos(ang), (1, reps)), jnp.tile(jnp.sin(ang) * sign, (1, reps))


def _gelu_tanh(x):
    k1 = -2.0 * math.sqrt(2.0 / math.pi) * LOG2E
    t = x * (k1 + (k1 * 0.044715) * (x * x))
    return x * (1.0 / (1.0 + jnp.exp2(t)))


def _chunk_mlp_kernel(u_ref, v_ref, w_ref, bias_ref, o_ref):
    n_rows = u_ref.shape[1]
    averager = jnp.full((LANES, LANES), 1.0 / LANES, BF16)
    for gi in range(u_ref.shape[2] // LANES):
        cols = slice(gi * LANES, (gi + 1) * LANES)
        u = _gelu_tanh(u_ref[0, :, cols].astype(F32))
        v = _gelu_tanh(v_ref[0, :, cols].astype(F32))
        mu = jnp.dot(v.astype(BF16), averager, preferred_element_type=F32)
        dlt = v - mu
        var = jnp.dot((dlt * dlt).astype(BF16), averager, preferred_element_type=F32)
        vn = (dlt * lax.rsqrt(var + EPS)).astype(BF16)
        w = w_ref[gi].astype(BF16)
        for n in range(n_rows // CHUNK):
            rows = slice(n * CHUNK, (n + 1) * CHUNK)
            mixed = jnp.dot(w, vn[rows], preferred_element_type=F32) + bias_ref[gi]
            o_ref[0, rows, cols] = (u[rows] * mixed).astype(o_ref.dtype)


def _chunk_mlp_call(p, w_s, bias, layer):
    b, l, _ = p.shape
    gpb = 2
    bw = gpb * LANES
    off = _proj_offsets(N_GROUPS * LANES)
    u_block = off["u"] // bw
    v_block = off["mv"] // bw
    return pl.pallas_call(
        _chunk_mlp_kernel,
        grid=(b, N_GROUPS // gpb),
        in_specs=[
            pl.BlockSpec((1, l, bw), lambda bi, j: (bi, 0, u_block + j)),
            pl.BlockSpec((1, l, bw), lambda bi, j: (bi, 0, v_block + j)),
            pl.BlockSpec((None, gpb, CHUNK, CHUNK), lambda bi, j: (layer, j, 0, 0)),
            pl.BlockSpec((None, gpb, CHUNK, LANES), lambda bi, j: (layer, j, 0, 0)),
        ],
        out_specs=pl.BlockSpec((1, l, bw), lambda bi, j: (bi, 0, j)),
        out_shape=jax.ShapeDtypeStruct((b, l, N_GROUPS * LANES), BF16),
        compiler_params=_cparams(("parallel", "parallel")),
        name="chunk_mlp",
    )(p, p, w_s, bias)


def _out_proj_kernel(ya_ref, yb_ref, yc_ref, yd_ref, x_ref, gt_ref, w_ref, o_ref, *, tn):
    d = o_ref.shape[-1]
    kw = ya_ref.shape[-1]
    ys = (ya_ref, yb_ref, yc_ref, yd_ref)
    for j in range(d // tn):
        cols = slice(j * tn, (j + 1) * tn)
        acc = jnp.dot(ys[0][0], w_ref[0:kw, cols], preferred_element_type=F32)
        for k in range(1, 4):
            acc = acc + jnp.dot(ys[k][0], w_ref[k * kw:(k + 1) * kw, cols], preferred_element_type=F32)
        o_ref[0, :, cols] = x_ref[0, :, cols] + gt_ref[:, cols] * acc


def _out_proj_call(ys, x, mods, w_out, layer, mod_row):
    b, l, d = x.shape
    kw = ys[0].shape[-1]
    tm = min(1024, l)
    kern = functools.partial(_out_proj_kernel, tn=512)
    yspec = pl.BlockSpec((1, tm, kw), lambda bi, i: (bi, i, 0))
    return pl.pallas_call(
        kern,
        grid=(b, l // tm),
        in_specs=[yspec, yspec, yspec, yspec,
                  pl.BlockSpec((1, tm, d), lambda bi, i: (bi, i, 0)),
                  _mod_spec(layer, mod_row, GT1, d),
                  _resident((None, d, d), lambda bi, i: (layer, 0, 0))],
        out_specs=pl.BlockSpec((1, tm, d), lambda bi, i: (bi, i, 0)),
        out_shape=jax.ShapeDtypeStruct((b, l, d), F32),
        compiler_params=_cparams(("parallel", "parallel")),
        name="out_proj",
    )(*ys, x, mods, w_out)


def _swiglu_kernel(x_ref, g_ref, sc_ref, sh_ref, gt_ref, wg_hbm, wu_hbm, wd_hbm, gf_ref, o_ref,
                   h_ref, wg_buf, wu_buf, wd_buf, sem, *, tm, tf, layer, final_norm):
    n_f = wg_hbm.shape[-1] // tf
    tile = pl.program_id(0) * pl.num_programs(1) + pl.program_id(1)
    n_tiles = pl.num_programs(0) * pl.num_programs(1)
    first_chunk = tile * n_f

    def chunk_copies(f, slot):
        cols = pl.ds(pl.multiple_of(f * tf, tf), tf)
        return (pltpu.make_async_copy(wg_hbm.at[layer, :, cols], wg_buf.at[slot], sem.at[0, slot]),
                pltpu.make_async_copy(wu_hbm.at[layer, :, cols], wu_buf.at[slot], sem.at[1, slot]),
                pltpu.make_async_copy(wd_hbm.at[layer, cols, :], wd_buf.at[slot], sem.at[2, slot]))

    def start_chunk(f, slot):
        for cp in chunk_copies(f, slot):
            cp.start()

    def wait_chunk(f, slot):
        for cp in chunk_copies(f, slot):
            cp.wait()

    def gated_ffn(h, slot):
        gate = jnp.dot(h, wg_buf[slot], preferred_element_type=F32)
        up = jnp.dot(h, wu_buf[slot], preferred_element_type=F32)
        a = (gate * jax.nn.sigmoid(gate) * up).astype(BF16)
        return gt_ref[...] * jnp.dot(a, wd_buf[slot], preferred_element_type=F32)

    @pl.when(tile == 0)
    def _():
        start_chunk(0, 0)

    slot0 = lax.rem(first_chunk, 2)
    wait_chunk(0, slot0)
    start_chunk(1, 1 - slot0)
    rb = min(tm, 256)
    for r in range(tm // rb):
        rows = slice(r * rb, (r + 1) * rb)
        xr = x_ref[0, rows, :]
        ms = jnp.mean(xr * xr, axis=-1, keepdims=True)
        h = (((xr * lax.rsqrt(ms + EPS)) * g_ref[0:1, :]) * (1.0 + sc_ref[...]) + sh_ref[...]).astype(BF16)
        h_ref[rows, :] = h
        o_ref[0, rows, :] = xr + gated_ffn(h, slot0)

    def chunk_step(f, carry):
        slot = lax.rem(first_chunk + f, 2)
        wait_chunk(f, slot)

        @pl.when(jnp.logical_or(f + 1 < n_f, tile + 1 < n_tiles))
        def _():
            start_chunk(lax.rem(f + 1, n_f), 1 - slot)

        o_ref[0] += gated_ffn(h_ref[...], slot)
        return carry

    lax.fori_loop(1, n_f, chunk_step, 0)

    if final_norm:
        def body(r, carry):
            rows = pl.ds(pl.multiple_of(r * 64, 64), 64)
            xr = o_ref[0, rows, :]
            ms = jnp.mean(xr * xr, axis=-1, keepdims=True)
            o_ref[0, rows, :] = (xr * lax.rsqrt(ms + EPS)) * gf_ref[0:1, :]
            return carry

        lax.fori_loop(0, tm // 64, body, 0)


def _swiglu_call(x, mods, g, w_gate, w_up, w_down, g_final, layer, mod_row, final_norm):
    b, l, d = x.shape
    ff = w_gate.shape[-1]
    tm = min(1024, l)
    tf = 512
    assert ff % tf == 0 and ff // tf >= 2
    kern = functools.partial(_swiglu_kernel, tm=tm, tf=tf, layer=layer, final_norm=final_norm)
    hbm = pl.BlockSpec(memory_space=pl.ANY)
    return pl.pallas_call(
        kern,
        grid=(b, l // tm),
        in_specs=[
            pl.BlockSpec((1, tm, d), lambda bi, i: (bi, i, 0)),
            pl.BlockSpec((None, g.shape[1], d), lambda bi, i: (layer, 0, 0)),
            _mod_spec(layer, mod_row, SC2, d),
            _mod_spec(layer, mod_row, SH2, d),
            _mod_spec(layer, mod_row, GT2, d),
            hbm, hbm, hbm,
            pl.BlockSpec(g_final.shape, lambda bi, i: (0, 0)),
        ],
        out_specs=pl.BlockSpec((1, tm, d), lambda bi, i: (bi, i, 0)),
        out_shape=jax.ShapeDtypeStruct((b, l, d), F32),
        scratch_shapes=[pltpu.VMEM((tm, d), BF16),
                        pltpu.VMEM((2, d, tf), BF16),
                        pltpu.VMEM((2, d, tf), BF16),
                        pltpu.VMEM((2, tf, d), BF16),
                        pltpu.SemaphoreType.DMA((3, 2))],
        compiler_params=_cparams(("arbitrary", "arbitrary")),
        name="swiglu",
    )(x, g, mods, mods, mods, w_gate, w_up, w_down, g_final)


def kernel(x, c, ctx, c_ctx, w_ada, b_ada, g_norm1, w_in, w_conv, sink, w_s, b_s, w_out, g_norm2,
           w_gate, w_up, w_down, g_final):
    batch, n_pos, d = x.shape
    n_ctx = ctx.shape[1]
    depth = w_ada.shape[0]
    cw = (d // 4) // N_GROUPS
    ctx_row = batch

    w_in_b = w_in.astype(BF16)
    w_out_b = w_out.astype(BF16)
    w_gate_b = w_gate.astype(BF16)
    w_up_b = w_up.astype(BF16)
    w_down_b = w_down.astype(BF16)
    g1 = jnp.broadcast_to(g_norm1[:, None, :], (depth, 2, d))
    g2 = jnp.broadcast_to(g_norm2[:, None, :], (depth, 2, d))
    gf = jnp.broadcast_to(g_final[None, :], (2, d))

    cs = jnp.concatenate([c, c_ctx[None, :], jnp.zeros((N_MOD_ROWS - batch - 1, d), F32)], axis=0)
    mods = _ada_call(cs, w_ada, b_ada).reshape(depth, N_MOD_ROWS, 1, 6 * d)

    cos_c, sin_c = _dft_tables(cw)
    ccsc = jnp.concatenate([cos_c, sin_c], axis=1).astype(BF16)
    use_fft = n_pos % (FFT_RADIX * 64) == 0 and n_pos // FFT_RADIX >= LANES
    if use_fft:
        ccsc_neg = jnp.concatenate([cos_c, -sin_c], axis=1).astype(BF16)
        fsub, twr, twi = _fft_tables(n_pos)
        latent_fourier = lambda pp: _fourier_fft_call(pp, ccsc_neg, fsub, twr, twi, f_block)
    else:
        cos_p, sin_p = _dft_tables(n_pos)
        cpsp = jnp.concatenate([cos_p, -sin_p], axis=1).astype(BF16)
        latent_fourier = lambda pp: _fourier_call(pp, ccsc, cpsp, f_block)
    cos_x, sin_x = _dft_tables(n_ctx)
    cpsp_ctx = jnp.concatenate([cos_x, -sin_x], axis=1).astype(BF16)
    rope_cos, rope_sin = _rope_tables(n_pos)
    mlp_bias = jnp.broadcast_to(b_s[:, :, :, None], b_s.shape + (cw,))
    off = _proj_offsets(d // 4)
    f_block = off["fourier"] // (N_GROUPS * cw)

    def flat(a):
        return a.reshape(1, -1, a.shape[-1])

    def unflat(a):
        return a.reshape(batch, n_ctx, a.shape[-1])

    xc = flat(ctx)
    for layer in range(depth):
        last = layer == depth - 1
        if last:
            w_kv = w_in_b[layer:layer + 1, :, off["k"]:off["u"]]
            pc = unflat(_norm_in_call(xc, mods, g1, w_kv, layer, 0, ctx_row, tn=256))
            kc_block = 0
        else:
            pc_flat = _norm_in_call(xc, mods, g1, w_in_b, layer, layer, ctx_row, tn=768)
            pc = unflat(pc_flat)
            kc_block = off["k"] // KV_WIDTH
            ys_c = (_conv_call(pc_flat, w_conv, layer, n_ctx),
                    flat(_fourier_call(pc, ccsc, cpsp_ctx, f_block)),
                    flat(_ctx_attn_call(pc, sink, layer)),
                    _chunk_mlp_call(pc_flat, w_s, mlp_bias, layer))
            xc_new = _out_proj_call(ys_c, xc, mods, w_out_b, layer, ctx_row)
            xc_next = _swiglu_call(xc_new, mods, g2, w_gate_b, w_up_b, w_down_b, gf, layer, ctx_row, False)
        p = _norm_in_call(x, mods, g1, w_in_b, layer, layer, None, tn=768)
        ys = (_conv_call(p, w_conv, layer, n_pos),
              latent_fourier(p),
              _latent_attn_call(p, pc, kc_block, sink, rope_cos, rope_sin, layer),
              _chunk_mlp_call(p, w_s, mlp_bias, layer))
        x = _out_proj_call(ys, x, mods, w_out_b, layer, None)
        x = _swiglu_call(x, mods, g2, w_gate_b, w_up_b, w_down_b, gf, layer, None, last)
        if not last:
            xc = xc_next
    return x
```

```python
import functools
import math

import jax
import jax.numpy as jnp
from jax import lax
from jax.experimental import pallas as pl
from jax.experimental.pallas import tpu as pltpu

F32 = jnp.float32
BF16 = jnp.bfloat16

EPS = 1e-6
NEG = -1e30
GRID_W = 64
HEAD_DIM = 64
N_HEADS = 8
N_KV_HEADS = 2
HEADS_PER_KV = N_HEADS // N_KV_HEADS
BLOCK = 128
CHUNK = 128
N_GROUPS = 4
ROPE_THETA = 10000.0
SCALE = HEAD_DIM ** -0.5
LOG2E = math.log2(math.e)
KV_WIDTH = N_KV_HEADS * HEAD_DIM


def _proj_offsets(width):
    off = {"conv": 0, "fourier": 3 * width, "q": 4 * width, "k": 5 * width}
    off["v"] = off["k"] + KV_WIDTH
    off["u"] = off["v"] + KV_WIDTH
    off["mv"] = off["u"] + width
    return off

LANES = 128
VMEM_LIMIT_BYTES = 60 * 1024 * 1024
N_MOD_ROWS = 24

SH1, SC1, GT1, SH2, SC2, GT2 = range(6)


def _cparams(sem):
    return pltpu.CompilerParams(dimension_semantics=sem, vmem_limit_bytes=VMEM_LIMIT_BYTES)


def _resident(block_shape, index_map):
    return pl.BlockSpec(block_shape, index_map, pipeline_mode=pl.Buffered(1))


def _ada_kernel(c_ref, w_ref, b_ref, o_ref):
    c = c_ref[...]
    s = (c * jax.nn.sigmoid(c)).astype(BF16)
    o_ref[...] = jnp.dot(s, w_ref[...].astype(BF16), preferred_element_type=F32) + b_ref[...]


def _ada_call(cs, w_ada, b_ada):
    depth, d, n = w_ada.shape
    tn = 2048
    return pl.pallas_call(
        _ada_kernel,
        grid=(depth, n // tn),
        in_specs=[
            pl.BlockSpec((N_MOD_ROWS, d), lambda l, j: (0, 0)),
            pl.BlockSpec((None, d, tn), lambda l, j: (l, 0, j)),
            pl.BlockSpec((None, 1, tn), lambda l, j: (l, 0, j)),
        ],
        out_specs=pl.BlockSpec((None, N_MOD_ROWS, tn), lambda l, j: (l, 0, j)),
        out_shape=jax.ShapeDtypeStruct((depth, N_MOD_ROWS, n), F32),
        compiler_params=_cparams(("parallel", "parallel")),
        name="ada_mod",
    )(cs, w_ada, b_ada.reshape(depth, 1, n))


def _mod_spec(layer, row, chunk, d):
    if row is None:
        return pl.BlockSpec((None, None, 1, d), lambda b, *_: (layer, b, 0, chunk))
    return pl.BlockSpec((None, None, 1, d), lambda b, *_: (layer, row, 0, chunk))


def _norm_modulate_rows(x_ref, g_ref, sc_ref, sh_ref, h_ref, tm, rb):
    def body(r, carry):
        rows = pl.ds(pl.multiple_of(r * rb, rb), rb)
        xr = x_ref[0, rows, :]
        ms = jnp.mean(xr * xr, axis=-1, keepdims=True)
        y = xr * lax.rsqrt(ms + EPS)
        h = (y * g_ref[0:1, :]) * (1.0 + sc_ref[...]) + sh_ref[...]
        h_ref[rows, :] = h.astype(h_ref.dtype)
        return carry

    lax.fori_loop(0, tm // rb, body, 0)


def _norm_in_kernel(x_ref, xn_ref, g_ref, sc_ref, sh_ref, scn_ref, shn_ref, w_ref, o_ref, ha_ref, hb_ref,
                    *, tm, tn):
    t = pl.program_id(0) * pl.num_programs(1) + pl.program_id(1)
    n = o_ref.shape[-1]

    @pl.when(t == 0)
    def _():
        _norm_modulate_rows(x_ref, g_ref, sc_ref, sh_ref, ha_ref, tm, min(tm, 64))

    def step(h_cur, h_next):
        rb = min(tm, 32)
        n_rb = tm // rb
        ct = 256 if tn % 256 == 0 else tn
        n_ct = n // ct
        anchors = [None] * n_ct
        for r in range(n_rb):
            rows = slice(r * rb, (r + 1) * rb)
            xr = xn_ref[0, rows, :]
            ms = jnp.mean(xr * xr, axis=-1, keepdims=True)
            h = ((xr * lax.rsqrt(ms + EPS)) * g_ref[0:1, :]) * (1.0 + scn_ref[...]) + shn_ref[...]
            hb = h.astype(BF16)
            h_next[rows, :] = hb
            folded = hb[:, 0:LANES]
            for k in range(1, hb.shape[1] // LANES):
                folded = folded + hb[:, k * LANES:(k + 1) * LANES]
            while folded.shape[0] > 16:
                half_rows = folded.shape[0] // 2
                folded = folded[:half_rows] + folded[half_rows:]
            word = pltpu.bitcast(folded, jnp.uint32)
            zero = lax.shift_right_logical(lax.shift_right_logical(word, jnp.uint32(16)), jnp.uint32(16))
            j = r * n_ct // n_rb
            anchors[j] = zero if anchors[j] is None else anchors[j] | zero
        for j in range(n // tn):
            cols = slice(j * tn, (j + 1) * tn)
            out = jnp.dot(h_cur[...], w_ref[:, cols], preferred_element_type=F32)
            pieces = []
            for s in range(tn // ct):
                piece = out[:, s * ct:(s + 1) * ct]
                anchor = anchors[j * (tn // ct) + s]
                if anchor is not None and ct % LANES == 0:
                    zrow = anchor[0:1, :].astype(F32)
                    piece = piece + jnp.concatenate([zrow] * (ct // LANES), axis=1)
                pieces.append(piece)
            o_ref[0, :, cols] = jnp.concatenate(pieces, axis=1).astype(o_ref.dtype)

    parity = lax.rem(t, 2)

    @pl.when(parity == 0)
    def _():
        step(ha_ref, hb_ref)

    @pl.when(parity == 1)
    def _():
        step(hb_ref, ha_ref)


def _norm_in_call(x, mods, g, w, layer, w_layer, mod_row, tn):
    b, l, d = x.shape
    n = w.shape[-1]
    tm = min(512, l)
    nt = l // tm
    kern = functools.partial(_norm_in_kernel, tm=tm, tn=tn)

    def nxt(bi, i):
        t = jnp.minimum(bi * nt + i + 1, b * nt - 1)
        return t // nt, t % nt

    def mod_next(chunk):
        if mod_row is None:
            return pl.BlockSpec((None, None, 1, d), lambda bi, i: (layer, nxt(bi, i)[0], 0, chunk))
        return _mod_spec(layer, mod_row, chunk, d)

    return pl.pallas_call(
        kern,
        grid=(b, nt),
        in_specs=[
            pl.BlockSpec((1, tm, d), lambda bi, i: (0, 0, 0)),
            pl.BlockSpec((1, tm, d), lambda bi, i: nxt(bi, i) + (0,)),
            pl.BlockSpec((None, g.shape[1], d), lambda bi, i: (layer, 0, 0)),
            _mod_spec(layer, mod_row, SC1, d),
            _mod_spec(layer, mod_row, SH1, d),
            mod_next(SC1),
            mod_next(SH1),
            _resident((None, d, n), lambda bi, i: (w_layer, 0, 0)),
        ],
        out_specs=pl.BlockSpec((1, tm, n), lambda bi, i: (bi, i, 0)),
        out_shape=jax.ShapeDtypeStruct((b, l, n), BF16),
        scratch_shapes=[pltpu.VMEM((tm, d), BF16), pltpu.VMEM((tm, d), BF16)],
        compiler_params=_cparams(("arbitrary", "arbitrary")),
        name="norm_in",
    )(x, x, g, mods, mods, mods, mods, w)


def _conv_kernel(x_ref, gb_ref, gc_ref, w_ref, o_ref, *, seq_len):
    n_rows = x_ref.shape[1]
    pos = lax.broadcasted_iota(jnp.int32, (n_rows, LANES), 0) % seq_len
    first = pos == 0
    last = pos == seq_len - 1
    for s in range(x_ref.shape[2] // LANES):
        cols = slice(s * LANES, (s + 1) * LANES)
        z = gc_ref[0, :, cols].astype(F32) * x_ref[0, :, cols].astype(F32)
        z_prev = jnp.where(first, 0.0, pltpu.roll(z, 1, 0))
        z_next = jnp.where(last, 0.0, pltpu.roll(z, n_rows - 1, 0))
        y = z_prev * w_ref[0:1, cols] + z * w_ref[1:2, cols] + z_next * w_ref[2:3, cols]
        o_ref[0, :, cols] = (gb_ref[0, :, cols].astype(F32) * y).astype(o_ref.dtype)


def _conv_call(p, w_conv, layer, seq_len):
    b, l, _ = p.shape
    width = w_conv.shape[-1]
    blk = lambda j: pl.BlockSpec((1, l, width), lambda bi: (bi, 0, j))
    return pl.pallas_call(
        functools.partial(_conv_kernel, seq_len=seq_len),
        grid=(b,),
        in_specs=[blk(0), blk(1), blk(2),
                  pl.BlockSpec((None, 3, width), lambda bi: (layer, 0, 0))],
        out_specs=pl.BlockSpec((1, l, width), lambda bi: (bi, 0, 0)),
        out_shape=jax.ShapeDtypeStruct((b, l, width), BF16),
        compiler_params=_cparams(("parallel",)),
        name="short_conv",
    )(p, p, p, w_conv)


def _fourier_kernel(z_ref, ccsc_ref, cpsp_ref, o_ref, rhs_ref, *, out_scale):
    n_pos = z_ref.shape[1]
    cw = ccsc_ref.shape[0]
    for g in range(N_GROUPS):
        cols = slice(g * cw, (g + 1) * cw)
        ab = jnp.dot(z_ref[0, :, cols], ccsc_ref[...], preferred_element_type=F32)
        rhs_ref[0:n_pos, cols] = ab[:, :cw].astype(BF16)
        rhs_ref[n_pos:2 * n_pos, cols] = ab[:, cw:].astype(BF16)
    out = jnp.dot(cpsp_ref[...], rhs_ref[...], preferred_element_type=F32)
    o_ref[0] = (out * out_scale).astype(o_ref.dtype)


def _fourier_call(p, ccsc, cpsp, col_block):
    b, l, _ = p.shape
    cw = ccsc.shape[0]
    width = N_GROUPS * cw
    kern = functools.partial(_fourier_kernel, out_scale=1.0 / math.sqrt(l * cw))
    return pl.pallas_call(
        kern,
        grid=(b,),
        in_specs=[
            pl.BlockSpec((1, l, width), lambda bi: (bi, 0, col_block)),
            _resident((cw, 2 * cw), lambda bi: (0, 0)),
            _resident((l, 2 * l), lambda bi: (0, 0)),
        ],
        out_specs=pl.BlockSpec((1, l, width), lambda bi: (bi, 0, 0)),
        out_shape=jax.ShapeDtypeStruct((b, l, width), BF16),
        scratch_shapes=[pltpu.VMEM((2 * l, width), BF16)],
        compiler_params=_cparams(("parallel",)),
        name="fourier_mix",
    )(p, ccsc, cpsp)


def _dft_tables(n):
    k = jnp.arange(n, dtype=jnp.int32)
    ang = ((k[:, None] * k[None, :]) % n).astype(F32) * (2.0 * math.pi / n)
    return jnp.cos(ang), jnp.sin(ang)


FFT_RADIX = 8


def _fourier_fft_kernel(z_ref, ccsc_ref, fsub_ref, twr_ref, twi_ref, o_ref, w_ref, g_ref, *, out_scale):
    n_pos = z_ref.shape[1]
    m = n_pos // FFT_RADIX
    cw = ccsc_ref.shape[0]
    c = math.sqrt(0.5)
    rb = 64
    gpp = 2
    for gp in range(N_GROUPS // gpp):
        groups = range(gp * gpp, (gp + 1) * gpp)
        pcols = slice(gp * gpp * cw, (gp + 1) * gpp * cw)
        for g in groups:
            ab = jnp.dot(z_ref[0, :, g * cw:(g + 1) * cw], ccsc_ref[...], preferred_element_type=F32)
            w_ref[g] = ab[:, :cw]
            w_ref[N_GROUPS + g] = ab[:, cw:]
        for n2 in range(FFT_RADIX):
            sub = pl.ds(n2, m, stride=FFT_RADIX)
            rhs = jnp.concatenate(
                [jnp.concatenate([w_ref[part * N_GROUPS + g, sub, :] for g in groups], axis=1)
                 for part in range(2)], axis=0).astype(BF16)
            g_ref[n2, :, pcols] = jnp.dot(fsub_ref[...], rhs, preferred_element_type=F32)
        for r in range(m // rb):
            rows = slice(r * rb, (r + 1) * rb)
            irows = slice(m + r * rb, m + (r + 1) * rb)
            for g in groups:
                cols = slice(g * cw, (g + 1) * cw)
                hr, hi = [g_ref[0, rows, cols]], [g_ref[0, irows, cols]]
                for n2 in range(1, FFT_RADIX):
                    gr, gi = g_ref[n2, rows, cols], g_ref[n2, irows, cols]
                    tr, ti = twr_ref[n2, rows, :], twi_ref[n2, rows, :]
                    hr.append(tr * gr - ti * gi)
                    hi.append(tr * gi + ti * gr)
                a0, a1 = hr[0] + hr[4], hr[0] - hr[4]
                a2 = hr[2] + hr[6]
                b0, b1 = hr[1] + hr[5], hr[1] - hr[5]
                b2, b3 = hr[3] + hr[7], hr[3] - hr[7]
                d1, d2, d3 = hi[1] - hi[5], hi[2] - hi[6], hi[3] - hi[7]
                e0, e1 = a0 + a2, b0 + b2
                odd_r = c * (b1 - b3)
                odd_i = c * (d1 + d3)
                p1, p3 = a1 + odd_r, a1 - odd_r
                p2 = a0 - a2
                q1, q3 = odd_i + d2, odd_i - d2
                q2 = (hi[1] + hi[5]) - (hi[3] + hi[7])
                ys = (e0 + e1, p1 + q1, p2 + q2, p3 + q3, e0 - e1, p3 - q3, p2 - q2, p1 - q1)
                for k2, y in enumerate(ys):
                    orow = slice(k2 * m + r * rb, k2 * m + (r + 1) * rb)
                    o_ref[0, orow, cols] = (y * out_scale).astype(o_ref.dtype)


def _fourier_fft_call(p, ccsc_neg, fsub, twr, twi, col_block):
    b, l, _ = p.shape
    cw = ccsc_neg.shape[0]
    width = N_GROUPS * cw
    m = l // FFT_RADIX
    kern = functools.partial(_fourier_fft_kernel, out_scale=1.0 / math.sqrt(l * cw))
    return pl.pallas_call(
        kern,
        grid=(b,),
        in_specs=[
            pl.BlockSpec((1, l, width), lambda bi: (bi, 0, col_block)),
            _resident((cw, 2 * cw), lambda bi: (0, 0)),
            _resident((2 * m, 2 * m), lambda bi: (0, 0)),
            _resident((FFT_RADIX, m, LANES), lambda bi: (0, 0, 0)),
            _resident((FFT_RADIX, m, LANES), lambda bi: (0, 0, 0)),
        ],
        out_specs=pl.BlockSpec((1, l, width), lambda bi: (bi, 0, 0)),
        out_shape=jax.ShapeDtypeStruct((b, l, width), BF16),
        scratch_shapes=[pltpu.VMEM((2 * N_GROUPS, l, cw), F32),
                        pltpu.VMEM((FFT_RADIX, 2 * m, width), F32)],
        compiler_params=_cparams(("parallel",)),
        name="fourier_fft",
    )(p, ccsc_neg, fsub, twr, twi)


def _fft_tables(n_pos):
    m = n_pos // FFT_RADIX
    cos_m, sin_m = _dft_tables(m)
    fsub = jnp.concatenate([jnp.concatenate([cos_m, sin_m], axis=1),
                            jnp.concatenate([-sin_m, cos_m], axis=1)], axis=0).astype(BF16)
    n2 = jnp.arange(FFT_RADIX, dtype=jnp.int32)[:, None]
    k1 = jnp.arange(m, dtype=jnp.int32)[None, :]
    ang = ((n2 * k1) % n_pos).astype(F32) * (2.0 * math.pi / n_pos)
    bcast = lambda t: jnp.broadcast_to(t[:, :, None], (FFT_RADIX, m, LANES))
    return fsub, bcast(jnp.cos(ang)), bcast(-jnp.sin(ang))


def _nt_dot(a, b):
    return lax.dot_general(a, b, (((1,), (1,)), ((), ())), preferred_element_type=F32)


def _latent_attn_kernel(sink_ref, q_ref, k_ref, v_ref, kc_ref, vc_ref, cos_ref, sin_ref, o_ref,
                        qlo_ref, qhi_ref, kpad_ref, kswp_ref, vt_ref, kcs_ref, vct_ref, *, layer):
    n_pos = q_ref.shape[1]
    n_blk = n_pos // BLOCK
    n_ctx = kc_ref.shape[1]
    half = LANES // 2

    lane = lax.broadcasted_iota(jnp.int32, (BLOCK, LANES), 1)
    low16 = (lane % 32) < 16
    lo_half = lane < half

    def prep_block(r, carry):
        rows = pl.ds(pl.multiple_of(r * BLOCK, BLOCK), BLOCK)
        cos = cos_ref[rows, :]
        sin = sin_ref[rows, :]

        def rope(t):
            swapped = jnp.where(low16, pltpu.roll(t, LANES - 16, 1), pltpu.roll(t, 16, 1))
            return t * cos + swapped * sin

        for s in range(q_ref.shape[-1] // LANES):
            cols = slice(s * LANES, (s + 1) * LANES)
            t = rope(q_ref[0, rows, cols].astype(F32)) * (SCALE * LOG2E)
            qlo_ref[rows, cols] = jnp.where(lo_half, t, 0.0).astype(BF16)
            qhi_ref[rows, cols] = jnp.where(lo_half, 0.0, t).astype(BF16)
        prow = pl.ds(pl.multiple_of(r * BLOCK + BLOCK, BLOCK), BLOCK)
        tk = rope(k_ref[0, rows, :].astype(F32))
        kpad_ref[prow, :] = tk.astype(BF16)
        kswp_ref[prow, :] = pltpu.roll(tk, half, 1).astype(BF16)
        vt_ref[r + 1] = v_ref[0, rows, :].astype(F32).T.astype(BF16)
        return carry

    zeros = jnp.zeros((BLOCK, LANES), BF16)
    for ref in (kpad_ref, kswp_ref):
        ref[0:BLOCK, :] = zeros
        ref[n_pos + BLOCK:n_pos + 2 * BLOCK, :] = zeros
    vt_ref[0] = zeros
    vt_ref[n_blk + 1] = zeros
    kcs_ref[...] = pltpu.roll(kc_ref[0].astype(F32), half, 1).astype(BF16)
    for cb in range(n_ctx // BLOCK):
        crow = slice(cb * BLOCK, (cb + 1) * BLOCK)
        vct_ref[:, crow] = vc_ref[0, crow, :].astype(F32).T.astype(BF16)
    lax.fori_loop(0, n_blk, prep_block, 0, unroll=4)

    head_order = (0, 2, 5, 7, 1, 3, 4, 6)
    col_of = {h: i for i, h in enumerate(head_order)}
    n_col = N_HEADS * BLOCK
    kj = lax.broadcasted_iota(jnp.int32, (3 * BLOCK, BLOCK), 0)
    qi = lax.broadcasted_iota(jnp.int32, (3 * BLOCK, BLOCK), 1)
    in_window = (kj >= qi) & (kj <= qi + 2 * BLOCK)
    sk = jnp.concatenate([jnp.full((1, BLOCK), sink_ref[layer, h] * LOG2E, F32) for h in head_order], axis=1)
    neg = NEG * LOG2E

    def slab(ref, rows, s):
        return ref[rows, s * LANES:(s + 1) * LANES]

    def attn_block(n, carry):
        rows = pl.ds(pl.multiple_of(n * BLOCK, BLOCK), BLOCK)
        band = pl.ds(pl.multiple_of(n * BLOCK, BLOCK), 3 * BLOCK)
        valid = in_window & (kj >= (1 - n) * BLOCK) & (kj < n_pos - (n - 1) * BLOCK)

        def masked(s):
            return jnp.concatenate([jnp.where(valid, s[:, i * BLOCK:(i + 1) * BLOCK], neg)
                                    for i in range(s.shape[1] // BLOCK)], axis=1)

        rhs_nat = jnp.concatenate([slab(qlo_ref, rows, 0), slab(qlo_ref, rows, 1),
                                   slab(qhi_ref, rows, 2), slab(qhi_ref, rows, 3)], axis=0)
        rhs_swp = jnp.concatenate([slab(qhi_ref, rows, 0), slab(qhi_ref, rows, 1),
                                   slab(qlo_ref, rows, 2), slab(qlo_ref, rows, 3)], axis=0)
        s_loc = jnp.concatenate([masked(_nt_dot(kpad_ref[band, :], rhs_nat)),
                                 masked(_nt_dot(kswp_ref[band, :], rhs_swp))], axis=1)
        s_ctx = jnp.concatenate([_nt_dot(kc_ref[0], rhs_nat),
                                 _nt_dot(kcs_ref[...], rhs_swp)], axis=1)
        m = jnp.maximum(jnp.maximum(jnp.max(s_loc, axis=0, keepdims=True),
                                    jnp.max(s_ctx, axis=0, keepdims=True)), sk)
        e_loc = jnp.exp2(s_loc - m)
        e_ctx = jnp.exp2(s_ctx - m)
        inv_den = 1.0 / (jnp.sum(e_loc, axis=0, keepdims=True) + jnp.sum(e_ctx, axis=0, keepdims=True)
                         + jnp.exp2(sk - m))
        e_loc = e_loc.astype(BF16)
        e_ctx = e_ctx.astype(BF16)
        vt_band = jnp.concatenate([vt_ref[n], vt_ref[n + 1], vt_ref[n + 2]], axis=1)
        o_t = {}
        for g in range(N_KV_HEADS):
            heads = [4 * g + j for j in range(HEADS_PER_KV)]
            pick = lambda arr: jnp.concatenate(
                [arr[:, col_of[h] * BLOCK:(col_of[h] + 1) * BLOCK] for h in heads], axis=1)
            vrows = slice(g * HEAD_DIM, (g + 1) * HEAD_DIM)
            o = (jnp.dot(vt_band[vrows], pick(e_loc), preferred_element_type=F32)
                 + jnp.dot(vct_ref[vrows, :], pick(e_ctx), preferred_element_type=F32)) * pick(inv_den)
            for j, h in enumerate(heads):
                o_t[h] = o[:, j * BLOCK:(j + 1) * BLOCK]
        slabs = [jnp.concatenate([o_t[2 * s], o_t[2 * s + 1]], axis=0).T for s in range(N_HEADS // 2)]
        o_ref[0, rows, :] = jnp.concatenate(slabs, axis=1).astype(o_ref.dtype)
        return carry

    lax.fori_loop(0, n_blk, attn_block, 0, unroll=4)


def _latent_attn_call(p, pc, kc_block, sink, cos_t, sin_t, layer):
    b, l, _ = p.shape
    lc = pc.shape[1]
    qw = N_HEADS * HEAD_DIM
    kvw = N_KV_HEADS * HEAD_DIM
    off = _proj_offsets(qw)
    q_block = off["q"] // qw
    k_block = off["k"] // kvw
    kern = functools.partial(_latent_attn_kernel, layer=layer)
    return pl.pallas_call(
        kern,
        grid=(b,),
        in_specs=[
            pl.BlockSpec(memory_space=pltpu.SMEM),
            pl.BlockSpec((1, l, qw), lambda bi: (bi, 0, q_block)),
            pl.BlockSpec((1, l, kvw), lambda bi: (bi, 0, k_block)),
            pl.BlockSpec((1, l, kvw), lambda bi: (bi, 0, k_block + 1)),
            pl.BlockSpec((1, lc, kvw), lambda bi: (bi, 0, kc_block)),
            pl.BlockSpec((1, lc, kvw), lambda bi: (bi, 0, kc_block + 1)),
            _resident((l, LANES), lambda bi: (0, 0)),
            _resident((l, LANES), lambda bi: (0, 0)),
        ],
        out_specs=pl.BlockSpec((1, l, qw), lambda bi: (bi, 0, 0)),
        out_shape=jax.ShapeDtypeStruct((b, l, qw), BF16),
        scratch_shapes=[pltpu.VMEM((l, qw), BF16),
                        pltpu.VMEM((l, qw), BF16),
                        pltpu.VMEM((l + 2 * BLOCK, kvw), BF16),
                        pltpu.VMEM((l + 2 * BLOCK, kvw), BF16),
                        pltpu.VMEM((l // BLOCK + 2, kvw, BLOCK), BF16),
                        pltpu.VMEM((lc, kvw), BF16),
                        pltpu.VMEM((kvw, lc), BF16)],
        compiler_params=_cparams(("parallel",)),
        name="latent_attention",
    )(sink, p, p, p, pc, pc, cos_t, sin_t)


def _ctx_attn_kernel(sink_ref, q_ref, kc_ref, vc_ref, o_ref, *, layer):
    outs = []
    for h in range(N_HEADS):
        kv = h // HEADS_PER_KV
        hc = slice(kv * HEAD_DIM, (kv + 1) * HEAD_DIM)
        qh = q_ref[0, :, h * HEAD_DIM:(h + 1) * HEAD_DIM] * SCALE
        s = _nt_dot(qh, kc_ref[0, :, hc])
        sk = sink_ref[layer, h]
        m = jnp.maximum(jnp.max(s, axis=-1, keepdims=True), sk)
        e = jnp.exp(s - m)
        den = jnp.sum(e, axis=-1, keepdims=True) + jnp.exp(sk - m)
        o = jnp.dot(e.astype(BF16), vc_ref[0, :, hc], preferred_element_type=F32)
        outs.append(o / den)
    o_ref[0] = jnp.concatenate(outs, axis=-1).astype(o_ref.dtype)


def _ctx_attn_call(pc, sink, layer):
    b, lc, _ = pc.shape
    qw = N_HEADS * HEAD_DIM
    kvw = N_KV_HEADS * HEAD_DIM
    off = _proj_offsets(qw)
    kern = functools.partial(_ctx_attn_kernel, layer=layer)
    return pl.pallas_call(
        kern,
        grid=(b,),
        in_specs=[
            pl.BlockSpec(memory_space=pltpu.SMEM),
            pl.BlockSpec((1, lc, qw), lambda bi: (bi, 0, off["q"] // qw)),
            pl.BlockSpec((1, lc, kvw), lambda bi: (bi, 0, off["k"] // kvw)),
            pl.BlockSpec((1, lc, kvw), lambda bi: (bi, 0, off["v"] // kvw)),
        ],
        out_specs=pl.BlockSpec((1, lc, qw), lambda bi: (bi, 0, 0)),
        out_shape=jax.ShapeDtypeStruct((b, lc, qw), BF16),
        compiler_params=_cparams(("parallel",)),
        name="context_attention",
    )(sink, pc, pc, pc)


def _rope_tables(n_pos):
    quarter = HEAD_DIM // 4
    inv = ROPE_THETA ** (-jnp.arange(quarter, dtype=F32) / quarter)
    pos = jnp.arange(n_pos, dtype=jnp.int32)
    row = (pos // GRID_W).astype(F32)
    col = (pos % GRID_W).astype(F32)
    a_row = row[:, None] * inv[None, :]
    a_col = col[:, None] * inv[None, :]
    ang = jnp.concatenate([a_row, a_row, a_col, a_col], axis=-1)
    sign = jnp.tile(jnp.concatenate([-jnp.ones((quarter,), F32), jnp.ones((quarter,), F32)]), 2)
    reps = LANES // HEAD_DIM
    return jnp.tile(jnp.cos(ang), (1, reps)), jnp.tile(jnp.sin(ang) * sign, (1, reps))


def _gelu_tanh(x):
    k1 = -2.0 * math.sqrt(2.0 / math.pi) * LOG2E
    t = x * (k1 + (k1 * 0.044715) * (x * x))
    return x * (1.0 / (1.0 + jnp.exp2(t)))


def _chunk_mlp_kernel(u_ref, v_ref, w_ref, bias_ref, o_ref):
    n_rows = u_ref.shape[1]
    averager = jnp.full((LANES, LANES), 1.0 / LANES, BF16)
    for gi in range(u_ref.shape[2] // LANES):
        cols = slice(gi * LANES, (gi + 1) * LANES)
        u = _gelu_tanh(u_ref[0, :, cols].astype(F32))
        v = _gelu_tanh(v_ref[0, :, cols].astype(F32))
        mu = jnp.dot(v.astype(BF16), averager, preferred_element_type=F32)
        dlt = v - mu
        var = jnp.dot((dlt * dlt).astype(BF16), averager, preferred_element_type=F32)
        vn = (dlt * lax.rsqrt(var + EPS)).astype(BF16)
        w = w_ref[gi].astype(BF16)
        for n in range(n_rows // CHUNK):
            rows = slice(n * CHUNK, (n + 1) * CHUNK)
            mixed = jnp.dot(w, vn[rows], preferred_element_type=F32) + bias_ref[gi]
            o_ref[0, rows, cols] = (u[rows] * mixed).astype(o_ref.dtype)


def _chunk_mlp_call(p, w_s, bias, layer):
    b, l, _ = p.shape
    gpb = 2
    bw = gpb * LANES
    off = _proj_offsets(N_GROUPS * LANES)
    u_block = off["u"] // bw
    v_block = off["mv"] // bw
    return pl.pallas_call(
        _chunk_mlp_kernel,
        grid=(b, N_GROUPS // gpb),
        in_specs=[
            pl.BlockSpec((1, l, bw), lambda bi, j: (bi, 0, u_block + j)),
            pl.BlockSpec((1, l, bw), lambda bi, j: (bi, 0, v_block + j)),
            pl.BlockSpec((None, gpb, CHUNK, CHUNK), lambda bi, j: (layer, j, 0, 0)),
            pl.BlockSpec((None, gpb, CHUNK, LANES), lambda bi, j: (layer, j, 0, 0)),
        ],
        out_specs=pl.BlockSpec((1, l, bw), lambda bi, j: (bi, 0, j)),
        out_shape=jax.ShapeDtypeStruct((b, l, N_GROUPS * LANES), BF16),
        compiler_params=_cparams(("parallel", "parallel")),
        name="chunk_mlp",
    )(p, p, w_s, bias)


def _out_proj_kernel(ya_ref, yb_ref, yc_ref, yd_ref, x_ref, gt_ref, w_ref, o_ref, *, tn):
    d = o_ref.shape[-1]
    kw = ya_ref.shape[-1]
    ys = (ya_ref, yb_ref, yc_ref, yd_ref)
    for j in range(d // tn):
        cols = slice(j * tn, (j + 1) * tn)
        acc = jnp.dot(ys[0][0], w_ref[0:kw, cols], preferred_element_type=F32)
        for k in range(1, 4):
            acc = acc + jnp.dot(ys[k][0], w_ref[k * kw:(k + 1) * kw, cols], preferred_element_type=F32)
        o_ref[0, :, cols] = x_ref[0, :, cols] + gt_ref[:, cols] * acc


def _out_proj_call(ys, x, mods, w_out, layer, mod_row):
    b, l, d = x.shape
    kw = ys[0].shape[-1]
    tm = min(1024, l)
    kern = functools.partial(_out_proj_kernel, tn=512)
    yspec = pl.BlockSpec((1, tm, kw), lambda bi, i: (bi, i, 0))
    return pl.pallas_call(
        kern,
        grid=(b, l // tm),
        in_specs=[yspec, yspec, yspec, yspec,
                  pl.BlockSpec((1, tm, d), lambda bi, i: (bi, i, 0)),
                  _mod_spec(layer, mod_row, GT1, d),
                  _resident((None, d, d), lambda bi, i: (layer, 0, 0))],
        out_specs=pl.BlockSpec((1, tm, d), lambda bi, i: (bi, i, 0)),
        out_shape=jax.ShapeDtypeStruct((b, l, d), F32),
        compiler_params=_cparams(("parallel", "parallel")),
        name="out_proj",
    )(*ys, x, mods, w_out)


def _swiglu_kernel(x_ref, g_ref, sc_ref, sh_ref, gt_ref, wg_hbm, wu_hbm, wd_hbm, gf_ref, o_ref,
                   h_ref, wg_buf, wu_buf, wd_buf, sem, *, tm, tf, layer, final_norm):
    n_f = wg_hbm.shape[-1] // tf
    tile = pl.program_id(0) * pl.num_programs(1) + pl.program_id(1)
    n_tiles = pl.num_programs(0) * pl.num_programs(1)
    first_chunk = tile * n_f

    def chunk_copies(f, slot):
        cols = pl.ds(pl.multiple_of(f * tf, tf), tf)
        return (pltpu.make_async_copy(wg_hbm.at[layer, :, cols], wg_buf.at[slot], sem.at[0, slot]),
                pltpu.make_async_copy(wu_hbm.at[layer, :, cols], wu_buf.at[slot], sem.at[1, slot]),
                pltpu.make_async_copy(wd_hbm.at[layer, cols, :], wd_buf.at[slot], sem.at[2, slot]))

    def start_chunk(f, slot):
        for cp in chunk_copies(f, slot):
            cp.start()

    def wait_chunk(f, slot):
        for cp in chunk_copies(f, slot):
            cp.wait()

    def gated_ffn(h, slot):
        gate = jnp.dot(h, wg_buf[slot], preferred_element_type=F32)
        up = jnp.dot(h, wu_buf[slot], preferred_element_type=F32)
        a = (gate * jax.nn.sigmoid(gate) * up).astype(BF16)
        return gt_ref[...] * jnp.dot(a, wd_buf[slot], preferred_element_type=F32)

    @pl.when(tile == 0)
    def _():
        start_chunk(0, 0)

    slot0 = lax.rem(first_chunk, 2)
    wait_chunk(0, slot0)
    start_chunk(1, 1 - slot0)
    rb = min(tm, 512)
    for r in range(tm // rb):
        rows = slice(r * rb, (r + 1) * rb)
        xr = x_ref[0, rows, :]
        ms = jnp.mean(xr * xr, axis=-1, keepdims=True)
        h = (((xr * lax.rsqrt(ms + EPS)) * g_ref[0:1, :]) * (1.0 + sc_ref[...]) + sh_ref[...]).astype(BF16)
        h_ref[rows, :] = h
        o_ref[0, rows, :] = xr + gated_ffn(h, slot0)

    def chunk_step(f, carry):
        slot = lax.rem(first_chunk + f, 2)
        wait_chunk(f, slot)

        @pl.when(jnp.logical_or(f + 1 < n_f, tile + 1 < n_tiles))
        def _():
            start_chunk(lax.rem(f + 1, n_f), 1 - slot)

        o_ref[0] += gated_ffn(h_ref[...], slot)
        return carry

    lax.fori_loop(1, n_f, chunk_step, 0)

    if final_norm:
        def body(r, carry):
            rows = pl.ds(pl.multiple_of(r * 64, 64), 64)
            xr = o_ref[0, rows, :]
            ms = jnp.mean(xr * xr, axis=-1, keepdims=True)
            o_ref[0, rows, :] = (xr * lax.rsqrt(ms + EPS)) * gf_ref[0:1, :]
            return carry

        lax.fori_loop(0, tm // 64, body, 0)


def _swiglu_call(x, mods, g, w_gate, w_up, w_down, g_final, layer, mod_row, final_norm):
    b, l, d = x.shape
    ff = w_gate.shape[-1]
    tm = min(1024, l)
    tf = 512
    assert ff % tf == 0 and ff // tf >= 2
    kern = functools.partial(_swiglu_kernel, tm=tm, tf=tf, layer=layer, final_norm=final_norm)
    hbm = pl.BlockSpec(memory_space=pl.ANY)
    return pl.pallas_call(
        kern,
        grid=(b, l // tm),
        in_specs=[
            pl.BlockSpec((1, tm, d), lambda bi, i: (bi, i, 0)),
            pl.BlockSpec((None, g.shape[1], d), lambda bi, i: (layer, 0, 0)),
            _mod_spec(layer, mod_row, SC2, d),
            _mod_spec(layer, mod_row, SH2, d),
            _mod_spec(layer, mod_row, GT2, d),
            hbm, hbm, hbm,
            pl.BlockSpec(g_final.shape, lambda bi, i: (0, 0)),
        ],
        out_specs=pl.BlockSpec((1, tm, d), lambda bi, i: (bi, i, 0)),
        out_shape=jax.ShapeDtypeStruct((b, l, d), F32),
        scratch_shapes=[pltpu.VMEM((tm, d), BF16),
                        pltpu.VMEM((2, d, tf), BF16),
                        pltpu.VMEM((2, d, tf), BF16),
                        pltpu.VMEM((2, tf, d), BF16),
                        pltpu.SemaphoreType.DMA((3, 2))],
        compiler_params=_cparams(("arbitrary", "arbitrary")),
        name="swiglu",
    )(x, g, mods, mods, mods, w_gate, w_up, w_down, g_final)


def kernel(x, c, ctx, c_ctx, w_ada, b_ada, g_norm1, w_in, w_conv, sink, w_s, b_s, w_out, g_norm2,
           w_gate, w_up, w_down, g_final):
    batch, n_pos, d = x.shape
    n_ctx = ctx.shape[1]
    depth = w_ada.shape[0]
    cw = (d // 4) // N_GROUPS
    ctx_row = batch

    w_in_b = w_in.astype(BF16)
    w_out_b = w_out.astype(BF16)
    w_gate_b = w_gate.astype(BF16)
    w_up_b = w_up.astype(BF16)
    w_down_b = w_down.astype(BF16)
    g1 = jnp.broadcast_to(g_norm1[:, None, :], (depth, 2, d))
    g2 = jnp.broadcast_to(g_norm2[:, None, :], (depth, 2, d))
    gf = jnp.broadcast_to(g_final[None, :], (2, d))

    cs = jnp.concatenate([c, c_ctx[None, :], jnp.zeros((N_MOD_ROWS - batch - 1, d), F32)], axis=0)
    mods = _ada_call(cs, w_ada, b_ada).reshape(depth, N_MOD_ROWS, 1, 6 * d)

    cos_c, sin_c = _dft_tables(cw)
    ccsc = jnp.concatenate([cos_c, sin_c], axis=1).astype(BF16)
    use_fft = n_pos % (FFT_RADIX * 64) == 0 and n_pos // FFT_RADIX >= LANES
    if use_fft:
        ccsc_neg = jnp.concatenate([cos_c, -sin_c], axis=1).astype(BF16)
        fsub, twr, twi = _fft_tables(n_pos)
        latent_fourier = lambda pp: _fourier_fft_call(pp, ccsc_neg, fsub, twr, twi, f_block)
    else:
        cos_p, sin_p = _dft_tables(n_pos)
        cpsp = jnp.concatenate([cos_p, -sin_p], axis=1).astype(BF16)
        latent_fourier = lambda pp: _fourier_call(pp, ccsc, cpsp, f_block)
    cos_x, sin_x = _dft_tables(n_ctx)
    cpsp_ctx = jnp.concatenate([cos_x, -sin_x], axis=1).astype(BF16)
    rope_cos, rope_sin = _rope_tables(n_pos)
    mlp_bias = jnp.broadcast_to(b_s[:, :, :, None], b_s.shape + (cw,))
    off = _proj_offsets(d // 4)
    f_block = off["fourier"] // (N_GROUPS * cw)

    def flat(a):
        return a.reshape(1, -1, a.shape[-1])

    def unflat(a):
        return a.reshape(batch, n_ctx, a.shape[-1])

    xc = flat(ctx)
    for layer in range(depth):
        last = layer == depth - 1
        if last:
            w_kv = w_in_b[layer:layer + 1, :, off["k"]:off["u"]]
            pc = unflat(_norm_in_call(xc, mods, g1, w_kv, layer, 0, ctx_row, tn=256))
            kc_block = 0
        else:
            pc_flat = _norm_in_call(xc, mods, g1, w_in_b, layer, layer, ctx_row, tn=768)
            pc = unflat(pc_flat)
            kc_block = off["k"] // KV_WIDTH
            ys_c = (_conv_call(pc_flat, w_conv, layer, n_ctx),
                    flat(_fourier_call(pc, ccsc, cpsp_ctx, f_block)),
                    flat(_ctx_attn_call(pc, sink, layer)),
                    _chunk_mlp_call(pc_flat, w_s, mlp_bias, layer))
            xc_new = _out_proj_call(ys_c, xc, mods, w_out_b, layer, ctx_row)
            xc_next = _swiglu_call(xc_new, mods, g2, w_gate_b, w_up_b, w_down_b, gf, layer, ctx_row, False)
        p = _norm_in_call(x, mods, g1, w_in_b, layer, layer, None, tn=768)
        ys = (_conv_call(p, w_conv, layer, n_pos),
              latent_fourier(p),
              _latent_attn_call(p, pc, kc_block, sink, rope_cos, rope_sin, layer),
              _chunk_mlp_call(p, w_s, mlp_bias, layer))
        x = _out_proj_call(ys, x, mods, w_out_b, layer, None)
        x = _swiglu_call(x, mods, g2, w_gate_b, w_up_b, w_down_b, gf, layer, None, last)
        if not last:
            xc = xc_next
    return x
```

```python
import functools
import math

import jax
import jax.numpy as jnp
from jax import lax
from jax.experimental import pallas as pl
from jax.experimental.pallas import tpu as pltpu

F32 = jnp.float32
BF16 = jnp.bfloat16

EPS = 1e-6
NEG = -1e30
GRID_W = 64
HEAD_DIM = 64
N_HEADS = 8
N_KV_HEADS = 2
HEADS_PER_KV = N_HEADS // N_KV_HEADS
BLOCK = 128
CHUNK = 128
N_GROUPS = 4
ROPE_THETA = 10000.0
SCALE = HEAD_DIM ** -0.5
LOG2E = math.log2(math.e)
KV_WIDTH = N_KV_HEADS * HEAD_DIM


def _proj_offsets(width):
    off = {"conv": 0, "fourier": 3 * width, "q": 4 * width, "k": 5 * width}
    off["v"] = off["k"] + KV_WIDTH
    off["u"] = off["v"] + KV_WIDTH
    off["mv"] = off["u"] + width
    return off

LANES = 128
VMEM_LIMIT_BYTES = 60 * 1024 * 1024
N_MOD_ROWS = 24

SH1, SC1, GT1, SH2, SC2, GT2 = range(6)


def _cparams(sem):
    return pltpu.CompilerParams(dimension_semantics=sem, vmem_limit_bytes=VMEM_LIMIT_BYTES)


def _resident(block_shape, index_map):
    return pl.BlockSpec(block_shape, index_map, pipeline_mode=pl.Buffered(1))


def _ada_kernel(c_ref, w_ref, b_ref, o_ref):
    c = c_ref[...]
    s = (c * jax.nn.sigmoid(c)).astype(BF16)
    o_ref[...] = jnp.dot(s, w_ref[...].astype(BF16), preferred_element_type=F32) + b_ref[...]


def _ada_call(cs, w_ada, b_ada):
    depth, d, n = w_ada.shape
    tn = 2048
    return pl.pallas_call(
        _ada_kernel,
        grid=(depth, n // tn),
        in_specs=[
            pl.BlockSpec((N_MOD_ROWS, d), lambda l, j: (0, 0)),
            pl.BlockSpec((None, d, tn), lambda l, j: (l, 0, j)),
            pl.BlockSpec((None, 1, tn), lambda l, j: (l, 0, j)),
        ],
        out_specs=pl.BlockSpec((None, N_MOD_ROWS, tn), lambda l, j: (l, 0, j)),
        out_shape=jax.ShapeDtypeStruct((depth, N_MOD_ROWS, n), F32),
        compiler_params=_cparams(("parallel", "parallel")),
        name="ada_mod",
    )(cs, w_ada, b_ada.reshape(depth, 1, n))


def _mod_spec(layer, row, chunk, d):
    if row is None:
        return pl.BlockSpec((None, None, 1, d), lambda b, *_: (layer, b, 0, chunk))
    return pl.BlockSpec((None, None, 1, d), lambda b, *_: (layer, row, 0, chunk))


def _norm_modulate_rows(x_ref, g_ref, sc_ref, sh_ref, h_ref, tm, rb):
    def body(r, carry):
        rows = pl.ds(pl.multiple_of(r * rb, rb), rb)
        xr = x_ref[0, rows, :]
        ms = jnp.mean(xr * xr, axis=-1, keepdims=True)
        y = xr * lax.rsqrt(ms + EPS)
        h = (y * g_ref[0:1, :]) * (1.0 + sc_ref[...]) + sh_ref[...]
        h_ref[rows, :] = h.astype(h_ref.dtype)
        return carry

    lax.fori_loop(0, tm // rb, body, 0)


def _norm_in_kernel(x_ref, xn_ref, g_ref, sc_ref, sh_ref, scn_ref, shn_ref, w_ref, o_ref, ha_ref, hb_ref,
                    *, tm, tn):
    t = pl.program_id(0) * pl.num_programs(1) + pl.program_id(1)
    n = o_ref.shape[-1]

    @pl.when(t == 0)
    def _():
        _norm_modulate_rows(x_ref, g_ref, sc_ref, sh_ref, ha_ref, tm, min(tm, 64))

    def step(h_cur, h_next):
        rb = min(tm, 32)
        n_rb = tm // rb
        ct = 256 if tn % 256 == 0 else tn
        n_ct = n // ct
        anchors = [None] * n_ct
        for r in range(n_rb):
            rows = slice(r * rb, (r + 1) * rb)
            xr = xn_ref[0, rows, :]
            ms = jnp.mean(xr * xr, axis=-1, keepdims=True)
            h = ((xr * lax.rsqrt(ms + EPS)) * g_ref[0:1, :]) * (1.0 + scn_ref[...]) + shn_ref[...]
            hb = h.astype(BF16)
            h_next[rows, :] = hb
            folded = hb[:, 0:LANES]
            for k in range(1, hb.shape[1] // LANES):
                folded = folded + hb[:, k * LANES:(k + 1) * LANES]
            while folded.shape[0] > 16:
                half_rows = folded.shape[0] // 2
                folded = folded[:half_rows] + folded[half_rows:]
            word = pltpu.bitcast(folded, jnp.uint32)
            zero = lax.shift_right_logical(lax.shift_right_logical(word, jnp.uint32(16)), jnp.uint32(16))
            j = r * n_ct // n_rb
            anchors[j] = zero if anchors[j] is None else anchors[j] | zero
        for j in range(n // tn):
            cols = slice(j * tn, (j + 1) * tn)
            out = jnp.dot(h_cur[...], w_ref[:, cols], preferred_element_type=F32)
            pieces = []
            for s in range(tn // ct):
                piece = out[:, s * ct:(s + 1) * ct]
                anchor = anchors[j * (tn // ct) + s]
                if anchor is not None and ct % LANES == 0:
                    zrow = anchor[0:1, :].astype(F32)
                    piece = piece + jnp.concatenate([zrow] * (ct // LANES), axis=1)
                pieces.append(piece)
            o_ref[0, :, cols] = jnp.concatenate(pieces, axis=1).astype(o_ref.dtype)

    parity = lax.rem(t, 2)

    @pl.when(parity == 0)
    def _():
        step(ha_ref, hb_ref)

    @pl.when(parity == 1)
    def _():
        step(hb_ref, ha_ref)


def _norm_in_call(x, mods, g, w, layer, w_layer, mod_row, tn):
    b, l, d = x.shape
    n = w.shape[-1]
    tm = min(512, l)
    nt = l // tm
    kern = functools.partial(_norm_in_kernel, tm=tm, tn=tn)

    def nxt(bi, i):
        t = jnp.minimum(bi * nt + i + 1, b * nt - 1)
        return t // nt, t % nt

    def mod_next(chunk):
        if mod_row is None:
            return pl.BlockSpec((None, None, 1, d), lambda bi, i: (layer, nxt(bi, i)[0], 0, chunk))
        return _mod_spec(layer, mod_row, chunk, d)

    return pl.pallas_call(
        kern,
        grid=(b, nt),
        in_specs=[
            pl.BlockSpec((1, tm, d), lambda bi, i: (0, 0, 0)),
            pl.BlockSpec((1, tm, d), lambda bi, i: nxt(bi, i) + (0,)),
            pl.BlockSpec((None, g.shape[1], d), lambda bi, i: (layer, 0, 0)),
            _mod_spec(layer, mod_row, SC1, d),
            _mod_spec(layer, mod_row, SH1, d),
            mod_next(SC1),
            mod_next(SH1),
            _resident((None, d, n), lambda bi, i: (w_layer, 0, 0)),
        ],
        out_specs=pl.BlockSpec((1, tm, n), lambda bi, i: (bi, i, 0)),
        out_shape=jax.ShapeDtypeStruct((b, l, n), BF16),
        scratch_shapes=[pltpu.VMEM((tm, d), BF16), pltpu.VMEM((tm, d), BF16)],
        compiler_params=_cparams(("arbitrary", "arbitrary")),
        name="norm_in",
    )(x, x, g, mods, mods, mods, mods, w)


def _conv_kernel(x_ref, gb_ref, gc_ref, w_ref, o_ref, *, seq_len):
    n_rows = x_ref.shape[1]
    pos = lax.broadcasted_iota(jnp.int32, (n_rows, LANES), 0) % seq_len
    first = pos == 0
    last = pos == seq_len - 1
    for s in range(x_ref.shape[2] // LANES):
        cols = slice(s * LANES, (s + 1) * LANES)
        z = gc_ref[0, :, cols].astype(F32) * x_ref[0, :, cols].astype(F32)
        z_prev = jnp.where(first, 0.0, pltpu.roll(z, 1, 0))
        z_next = jnp.where(last, 0.0, pltpu.roll(z, n_rows - 1, 0))
        y = z_prev * w_ref[0:1, cols] + z * w_ref[1:2, cols] + z_next * w_ref[2:3, cols]
        o_ref[0, :, cols] = (gb_ref[0, :, cols].astype(F32) * y).astype(o_ref.dtype)


def _conv_call(p, w_conv, layer, seq_len):
    b, l, _ = p.shape
    width = w_conv.shape[-1]
    blk = lambda j: pl.BlockSpec((1, l, width), lambda bi: (bi, 0, j))
    return pl.pallas_call(
        functools.partial(_conv_kernel, seq_len=seq_len),
        grid=(b,),
        in_specs=[blk(0), blk(1), blk(2),
                  pl.BlockSpec((None, 3, width), lambda bi: (layer, 0, 0))],
        out_specs=pl.BlockSpec((1, l, width), lambda bi: (bi, 0, 0)),
        out_shape=jax.ShapeDtypeStruct((b, l, width), BF16),
        compiler_params=_cparams(("parallel",)),
        name="short_conv",
    )(p, p, p, w_conv)


def _fourier_kernel(z_ref, ccsc_ref, cpsp_ref, o_ref, rhs_ref, *, out_scale):
    n_pos = z_ref.shape[1]
    cw = ccsc_ref.shape[0]
    for g in range(N_GROUPS):
        cols = slice(g * cw, (g + 1) * cw)
        ab = jnp.dot(z_ref[0, :, cols], ccsc_ref[...], preferred_element_type=F32)
        rhs_ref[0:n_pos, cols] = ab[:, :cw].astype(BF16)
        rhs_ref[n_pos:2 * n_pos, cols] = ab[:, cw:].astype(BF16)
    out = jnp.dot(cpsp_ref[...], rhs_ref[...], preferred_element_type=F32)
    o_ref[0] = (out * out_scale).astype(o_ref.dtype)


def _fourier_call(p, ccsc, cpsp, col_block):
    b, l, _ = p.shape
    cw = ccsc.shape[0]
    width = N_GROUPS * cw
    kern = functools.partial(_fourier_kernel, out_scale=1.0 / math.sqrt(l * cw))
    return pl.pallas_call(
        kern,
        grid=(b,),
        in_specs=[
            pl.BlockSpec((1, l, width), lambda bi: (bi, 0, col_block)),
            _resident((cw, 2 * cw), lambda bi: (0, 0)),
            _resident((l, 2 * l), lambda bi: (0, 0)),
        ],
        out_specs=pl.BlockSpec((1, l, width), lambda bi: (bi, 0, 0)),
        out_shape=jax.ShapeDtypeStruct((b, l, width), BF16),
        scratch_shapes=[pltpu.VMEM((2 * l, width), BF16)],
        compiler_params=_cparams(("parallel",)),
        name="fourier_mix",
    )(p, ccsc, cpsp)


def _dft_tables(n):
    k = jnp.arange(n, dtype=jnp.int32)
    ang = ((k[:, None] * k[None, :]) % n).astype(F32) * (2.0 * math.pi / n)
    return jnp.cos(ang), jnp.sin(ang)


FFT_RADIX = 8


def _fourier_fft_kernel(z_ref, ccsc_ref, fsub_ref, twr_ref, twi_ref, o_ref, w_ref, g_ref, *, out_scale):
    n_pos = z_ref.shape[1]
    m = n_pos // FFT_RADIX
    cw = ccsc_ref.shape[0]
    c = math.sqrt(0.5)
    rb = 64
    gpp = 2
    for gp in range(N_GROUPS // gpp):
        groups = range(gp * gpp, (gp + 1) * gpp)
        pcols = slice(gp * gpp * cw, (gp + 1) * gpp * cw)
        for g in groups:
            ab = jnp.dot(z_ref[0, :, g * cw:(g + 1) * cw], ccsc_ref[...], preferred_element_type=F32)
            w_ref[g] = ab[:, :cw]
            w_ref[N_GROUPS + g] = ab[:, cw:]
        for n2 in range(FFT_RADIX):
            sub = pl.ds(n2, m, stride=FFT_RADIX)
            rhs = jnp.concatenate(
                [jnp.concatenate([w_ref[part * N_GROUPS + g, sub, :] for g in groups], axis=1)
                 for part in range(2)], axis=0).astype(BF16)
            g_ref[n2, :, pcols] = jnp.dot(fsub_ref[...], rhs, preferred_element_type=F32)
        for r in range(m // rb):
            rows = slice(r * rb, (r + 1) * rb)
            irows = slice(m + r * rb, m + (r + 1) * rb)
            for g in groups:
                cols = slice(g * cw, (g + 1) * cw)
                hr, hi = [g_ref[0, rows, cols]], [g_ref[0, irows, cols]]
                for n2 in range(1, FFT_RADIX):
                    gr, gi = g_ref[n2, rows, cols], g_ref[n2, irows, cols]
                    tr, ti = twr_ref[n2, rows, :], twi_ref[n2, rows, :]
                    hr.append(tr * gr - ti * gi)
                    hi.append(tr * gi + ti * gr)
                a0, a1 = hr[0] + hr[4], hr[0] - hr[4]
                a2 = hr[2] + hr[6]
                b0, b1 = hr[1] + hr[5], hr[1] - hr[5]
                b2, b3 = hr[3] + hr[7], hr[3] - hr[7]
                d1, d2, d3 = hi[1] - hi[5], hi[2] - hi[6], hi[3] - hi[7]
                e0, e1 = a0 + a2, b0 + b2
                odd_r = c * (b1 - b3)
                odd_i = c * (d1 + d3)
                p1, p3 = a1 + odd_r, a1 - odd_r
                p2 = a0 - a2
                q1, q3 = odd_i + d2, odd_i - d2
                q2 = (hi[1] + hi[5]) - (hi[3] + hi[7])
                ys = (e0 + e1, p1 + q1, p2 + q2, p3 + q3, e0 - e1, p3 - q3, p2 - q2, p1 - q1)
                for k2, y in enumerate(ys):
                    orow = slice(k2 * m + r * rb, k2 * m + (r + 1) * rb)
                    o_ref[0, orow, cols] = (y * out_scale).astype(o_ref.dtype)


def _fourier_fft_call(p, ccsc_neg, fsub, twr, twi, col_block):
    b, l, _ = p.shape
    cw = ccsc_neg.shape[0]
    width = N_GROUPS * cw
    m = l // FFT_RADIX
    kern = functools.partial(_fourier_fft_kernel, out_scale=1.0 / math.sqrt(l * cw))
    return pl.pallas_call(
        kern,
        grid=(b,),
        in_specs=[
            pl.BlockSpec((1, l, width), lambda bi: (bi, 0, col_block)),
            _resident((cw, 2 * cw), lambda bi: (0, 0)),
            _resident((2 * m, 2 * m), lambda bi: (0, 0)),
            _resident((FFT_RADIX, m, LANES), lambda bi: (0, 0, 0)),
            _resident((FFT_RADIX, m, LANES), lambda bi: (0, 0, 0)),
        ],
        out_specs=pl.BlockSpec((1, l, width), lambda bi: (bi, 0, 0)),
        out_shape=jax.ShapeDtypeStruct((b, l, width), BF16),
        scratch_shapes=[pltpu.VMEM((2 * N_GROUPS, l, cw), F32),
                        pltpu.VMEM((FFT_RADIX, 2 * m, width), F32)],
        compiler_params=_cparams(("parallel",)),
        name="fourier_fft",
    )(p, ccsc_neg, fsub, twr, twi)


def _fft_tables(n_pos):
    m = n_pos // FFT_RADIX
    cos_m, sin_m = _dft_tables(m)
    fsub = jnp.concatenate([jnp.concatenate([cos_m, sin_m], axis=1),
                            jnp.concatenate([-sin_m, cos_m], axis=1)], axis=0).astype(BF16)
    n2 = jnp.arange(FFT_RADIX, dtype=jnp.int32)[:, None]
    k1 = jnp.arange(m, dtype=jnp.int32)[None, :]
    ang = ((n2 * k1) % n_pos).astype(F32) * (2.0 * math.pi / n_pos)
    bcast = lambda t: jnp.broadcast_to(t[:, :, None], (FFT_RADIX, m, LANES))
    return fsub, bcast(jnp.cos(ang)), bcast(-jnp.sin(ang))


def _nt_dot(a, b):
    return lax.dot_general(a, b, (((1,), (1,)), ((), ())), preferred_element_type=F32)


def _latent_attn_kernel(sink_ref, q_ref, k_ref, v_ref, kc_ref, vc_ref, cos_ref, sin_ref, o_ref,
                        qlo_ref, qhi_ref, kpad_ref, kswp_ref, vt_ref, kcs_ref, vct_ref, *, layer):
    n_pos = q_ref.shape[1]
    n_blk = n_pos // BLOCK
    n_ctx = kc_ref.shape[1]
    half = LANES // 2

    lane = lax.broadcasted_iota(jnp.int32, (BLOCK, LANES), 1)
    low16 = (lane % 32) < 16
    lo_half = lane < half

    def prep_block(r, carry):
        rows = pl.ds(pl.multiple_of(r * BLOCK, BLOCK), BLOCK)
        cos = cos_ref[rows, :]
        sin = sin_ref[rows, :]

        def rope(t):
            swapped = jnp.where(low16, pltpu.roll(t, LANES - 16, 1), pltpu.roll(t, 16, 1))
            return t * cos + swapped * sin

        for s in range(q_ref.shape[-1] // LANES):
            cols = slice(s * LANES, (s + 1) * LANES)
            t = rope(q_ref[0, rows, cols].astype(F32)) * (SCALE * LOG2E)
            qlo_ref[rows, cols] = jnp.where(lo_half, t, 0.0).astype(BF16)
            qhi_ref[rows, cols] = jnp.where(lo_half, 0.0, t).astype(BF16)
        prow = pl.ds(pl.multiple_of(r * BLOCK + BLOCK, BLOCK), BLOCK)
        tk = rope(k_ref[0, rows, :].astype(F32))
        kpad_ref[prow, :] = tk.astype(BF16)
        kswp_ref[prow, :] = pltpu.roll(tk, half, 1).astype(BF16)
        vt_ref[r + 1] = v_ref[0, rows, :].astype(F32).T.astype(BF16)
        return carry

    zeros = jnp.zeros((BLOCK, LANES), BF16)
    for ref in (kpad_ref, kswp_ref):
        ref[0:BLOCK, :] = zeros
        ref[n_pos + BLOCK:n_pos + 2 * BLOCK, :] = zeros
    vt_ref[0] = zeros
    vt_ref[n_blk + 1] = zeros
    kcs_ref[...] = pltpu.roll(kc_ref[0].astype(F32), half, 1).astype(BF16)
    for cb in range(n_ctx // BLOCK):
        crow = slice(cb * BLOCK, (cb + 1) * BLOCK)
        vct_ref[:, crow] = vc_ref[0, crow, :].astype(F32).T.astype(BF16)
    lax.fori_loop(0, n_blk, prep_block, 0, unroll=4)

    head_order = (0, 2, 5, 7, 1, 3, 4, 6)
    col_of = {h: i for i, h in enumerate(head_order)}
    n_col = N_HEADS * BLOCK
    kj = lax.broadcasted_iota(jnp.int32, (3 * BLOCK, BLOCK), 0)
    qi = lax.broadcasted_iota(jnp.int32, (3 * BLOCK, BLOCK), 1)
    in_window = (kj >= qi) & (kj <= qi + 2 * BLOCK)
    sk = jnp.concatenate([jnp.full((1, BLOCK), sink_ref[layer, h] * LOG2E, F32) for h in head_order], axis=1)
    neg = NEG * LOG2E

    def slab(ref, rows, s):
        return ref[rows, s * LANES:(s + 1) * LANES]

    def attn_block(n, carry):
        rows = pl.ds(pl.multiple_of(n * BLOCK, BLOCK), BLOCK)
        band = pl.ds(pl.multiple_of(n * BLOCK, BLOCK), 3 * BLOCK)
        valid = in_window & (kj >= (1 - n) * BLOCK) & (kj < n_pos - (n - 1) * BLOCK)

        def masked(s):
            lo, hi = slice(0, BLOCK), slice(2 * BLOCK, 3 * BLOCK)
            cols = [s[:, i * BLOCK:(i + 1) * BLOCK] for i in range(s.shape[1] // BLOCK)]
            return jnp.concatenate(
                [jnp.concatenate([jnp.where(valid[lo], c[lo], neg), c[BLOCK:2 * BLOCK],
                                  jnp.where(valid[hi], c[hi], neg)], axis=0) for c in cols], axis=1)

        rhs_nat = jnp.concatenate([slab(qlo_ref, rows, 0), slab(qlo_ref, rows, 1),
                                   slab(qhi_ref, rows, 2), slab(qhi_ref, rows, 3)], axis=0)
        rhs_swp = jnp.concatenate([slab(qhi_ref, rows, 0), slab(qhi_ref, rows, 1),
                                   slab(qlo_ref, rows, 2), slab(qlo_ref, rows, 3)], axis=0)
        s_loc = jnp.concatenate([masked(_nt_dot(kpad_ref[band, :], rhs_nat)),
                                 masked(_nt_dot(kswp_ref[band, :], rhs_swp))], axis=1)
        s_ctx = jnp.concatenate([_nt_dot(kc_ref[0], rhs_nat),
                                 _nt_dot(kcs_ref[...], rhs_swp)], axis=1)
        m = jnp.maximum(jnp.maximum(jnp.max(s_loc, axis=0, keepdims=True),
                                    jnp.max(s_ctx, axis=0, keepdims=True)), sk)
        e_loc = jnp.exp2(s_loc - m)
        e_ctx = jnp.exp2(s_ctx - m)
        inv_den = 1.0 / (jnp.sum(e_loc, axis=0, keepdims=True) + jnp.sum(e_ctx, axis=0, keepdims=True)
                         + jnp.exp2(sk - m))
        e_loc = e_loc.astype(BF16)
        e_ctx = e_ctx.astype(BF16)
        vt_band = jnp.concatenate([vt_ref[n], vt_ref[n + 1], vt_ref[n + 2]], axis=1)
        o_t = {}
        for g in range(N_KV_HEADS):
            heads = [4 * g + j for j in range(HEADS_PER_KV)]
            pick = lambda arr: jnp.concatenate(
                [arr[:, col_of[h] * BLOCK:(col_of[h] + 1) * BLOCK] for h in heads], axis=1)
            vrows = slice(g * HEAD_DIM, (g + 1) * HEAD_DIM)
            o = (jnp.dot(vt_band[vrows], pick(e_loc), preferred_element_type=F32)
                 + jnp.dot(vct_ref[vrows, :], pick(e_ctx), preferred_element_type=F32)) * pick(inv_den)
            for j, h in enumerate(heads):
                o_t[h] = o[:, j * BLOCK:(j + 1) * BLOCK]
        slabs = [jnp.concatenate([o_t[2 * s], o_t[2 * s + 1]], axis=0).T for s in range(N_HEADS // 2)]
        o_ref[0, rows, :] = jnp.concatenate(slabs, axis=1).astype(o_ref.dtype)
        return carry

    lax.fori_loop(0, n_blk, attn_block, 0, unroll=8)


def _latent_attn_call(p, pc, kc_block, sink, cos_t, sin_t, layer):
    b, l, _ = p.shape
    lc = pc.shape[1]
    qw = N_HEADS * HEAD_DIM
    kvw = N_KV_HEADS * HEAD_DIM
    off = _proj_offsets(qw)
    q_block = off["q"] // qw
    k_block = off["k"] // kvw
    kern = functools.partial(_latent_attn_kernel, layer=layer)
    return pl.pallas_call(
        kern,
        grid=(b,),
        in_specs=[
            pl.BlockSpec(memory_space=pltpu.SMEM),
            pl.BlockSpec((1, l, qw), lambda bi: (bi, 0, q_block)),
            pl.BlockSpec((1, l, kvw), lambda bi: (bi, 0, k_block)),
            pl.BlockSpec((1, l, kvw), lambda bi: (bi, 0, k_block + 1)),
            pl.BlockSpec((1, lc, kvw), lambda bi: (bi, 0, kc_block)),
            pl.BlockSpec((1, lc, kvw), lambda bi: (bi, 0, kc_block + 1)),
            _resident((l, LANES), lambda bi: (0, 0)),
            _resident((l, LANES), lambda bi: (0, 0)),
        ],
        out_specs=pl.BlockSpec((1, l, qw), lambda bi: (bi, 0, 0)),
        out_shape=jax.ShapeDtypeStruct((b, l, qw), BF16),
        scratch_shapes=[pltpu.VMEM((l, qw), BF16),
                        pltpu.VMEM((l, qw), BF16),
                        pltpu.VMEM((l + 2 * BLOCK, kvw), BF16),
                        pltpu.VMEM((l + 2 * BLOCK, kvw), BF16),
                        pltpu.VMEM((l // BLOCK + 2, kvw, BLOCK), BF16),
                        pltpu.VMEM((lc, kvw), BF16),
                        pltpu.VMEM((kvw, lc), BF16)],
        compiler_params=_cparams(("parallel",)),
        name="latent_attention",
    )(sink, p, p, p, pc, pc, cos_t, sin_t)


def _ctx_attn_kernel(sink_ref, q_ref, kc_ref, vc_ref, o_ref, *, layer):
    outs = []
    for h in range(N_HEADS):
        kv = h // HEADS_PER_KV
        hc = slice(kv * HEAD_DIM, (kv + 1) * HEAD_DIM)
        qh = q_ref[0, :, h * HEAD_DIM:(h + 1) * HEAD_DIM] * SCALE
        s = _nt_dot(qh, kc_ref[0, :, hc])
        sk = sink_ref[layer, h]
        m = jnp.maximum(jnp.max(s, axis=-1, keepdims=True), sk)
        e = jnp.exp(s - m)
        den = jnp.sum(e, axis=-1, keepdims=True) + jnp.exp(sk - m)
        o = jnp.dot(e.astype(BF16), vc_ref[0, :, hc], preferred_element_type=F32)
        outs.append(o / den)
    o_ref[0] = jnp.concatenate(outs, axis=-1).astype(o_ref.dtype)


def _ctx_attn_call(pc, sink, layer):
    b, lc, _ = pc.shape
    qw = N_HEADS * HEAD_DIM
    kvw = N_KV_HEADS * HEAD_DIM
    off = _proj_offsets(qw)
    kern = functools.partial(_ctx_attn_kernel, layer=layer)
    return pl.pallas_call(
        kern,
        grid=(b,),
        in_specs=[
            pl.BlockSpec(memory_space=pltpu.SMEM),
            pl.BlockSpec((1, lc, qw), lambda bi: (bi, 0, off["q"] // qw)),
            pl.BlockSpec((1, lc, kvw), lambda bi: (bi, 0, off["k"] // kvw)),
            pl.BlockSpec((1, lc, kvw), lambda bi: (bi, 0, off["v"] // kvw)),
        ],
        out_specs=pl.BlockSpec((1, lc, qw), lambda bi: (bi, 0, 0)),
        out_shape=jax.ShapeDtypeStruct((b, lc, qw), BF16),
        compiler_params=_cparams(("parallel",)),
        name="context_attention",
    )(sink, pc, pc, pc)


def _rope_tables(n_pos):
    quarter = HEAD_DIM // 4
    inv = ROPE_THETA ** (-jnp.arange(quarter, dtype=F32) / quarter)
    pos = jnp.arange(n_pos, dtype=jnp.int32)
    row = (pos // GRID_W).astype(F32)
    col = (pos % GRID_W).astype(F32)
    a_row = row[:, None] * inv[None, :]
    a_col = col[:, None] * inv[None, :]
    ang = jnp.concatenate([a_row, a_row, a_col, a_col], axis=-1)
    sign = jnp.tile(jnp.concatenate([-jnp.ones((quarter,), F32), jnp.ones((quarter,), F32)]), 2)
    reps = LANES // HEAD_DIM
    return jnp.tile(jnp.cos(ang), (1, reps)), jnp.tile(jnp.sin(ang) * sign, (1, reps))


def _gelu_tanh(x):
    k1 = -2.0 * math.sqrt(2.0 / math.pi) * LOG2E
    t = x * (k1 + (k1 * 0.044715) * (x * x))
    return x * (1.0 / (1.0 + jnp.exp2(t)))


def _chunk_mlp_kernel(u_ref, v_ref, w_ref, bias_ref, o_ref):
    n_rows = u_ref.shape[1]
    averager = jnp.full((LANES, LANES), 1.0 / LANES, BF16)
    for gi in range(u_ref.shape[2] // LANES):
        cols = slice(gi * LANES, (gi + 1) * LANES)
        u = _gelu_tanh(u_ref[0, :, cols].astype(F32))
        v = _gelu_tanh(v_ref[0, :, cols].astype(F32))
        mu = jnp.dot(v.astype(BF16), averager, preferred_element_type=F32)
        dlt = v - mu
        var = jnp.dot((dlt * dlt).astype(BF16), averager, preferred_element_type=F32)
        vn = (dlt * lax.rsqrt(var + EPS)).astype(BF16)
        w = w_ref[gi].astype(BF16)
        for n in range(n_rows // CHUNK):
            rows = slice(n * CHUNK, (n + 1) * CHUNK)
            mixed = jnp.dot(w, vn[rows], preferred_element_type=F32) + bias_ref[gi]
            o_ref[0, rows, cols] = (u[rows] * mixed).astype(o_ref.dtype)


def _chunk_mlp_call(p, w_s, bias, layer):
    b, l, _ = p.shape
    gpb = 2
    bw = gpb * LANES
    off = _proj_offsets(N_GROUPS * LANES)
    u_block = off["u"] // bw
    v_block = off["mv"] // bw
    return pl.pallas_call(
        _chunk_mlp_kernel,
        grid=(b, N_GROUPS // gpb),
        in_specs=[
            pl.BlockSpec((1, l, bw), lambda bi, j: (bi, 0, u_block + j)),
            pl.BlockSpec((1, l, bw), lambda bi, j: (bi, 0, v_block + j)),
            pl.BlockSpec((None, gpb, CHUNK, CHUNK), lambda bi, j: (layer, j, 0, 0)),
            pl.BlockSpec((None, gpb, CHUNK, LANES), lambda bi, j: (layer, j, 0, 0)),
        ],
        out_specs=pl.BlockSpec((1, l, bw), lambda bi, j: (bi, 0, j)),
        out_shape=jax.ShapeDtypeStruct((b, l, N_GROUPS * LANES), BF16),
        compiler_params=_cparams(("parallel", "parallel")),
        name="chunk_mlp",
    )(p, p, w_s, bias)


def _out_proj_kernel(ya_ref, yb_ref, yc_ref, yd_ref, x_ref, gt_ref, w_ref, o_ref, *, tn):
    d = o_ref.shape[-1]
    kw = ya_ref.shape[-1]
    ys = (ya_ref, yb_ref, yc_ref, yd_ref)
    for j in range(d // tn):
        cols = slice(j * tn, (j + 1) * tn)
        acc = jnp.dot(ys[0][0], w_ref[0:kw, cols], preferred_element_type=F32)
        for k in range(1, 4):
            acc = acc + jnp.dot(ys[k][0], w_ref[k * kw:(k + 1) * kw, cols], preferred_element_type=F32)
        o_ref[0, :, cols] = x_ref[0, :, cols] + gt_ref[:, cols] * acc


def _out_proj_call(ys, x, mods, w_out, layer, mod_row):
    b, l, d = x.shape
    kw = ys[0].shape[-1]
    tm = min(1024, l)
    kern = functools.partial(_out_proj_kernel, tn=512)
    yspec = pl.BlockSpec((1, tm, kw), lambda bi, i: (bi, i, 0))
    return pl.pallas_call(
        kern,
        grid=(b, l // tm),
        in_specs=[yspec, yspec, yspec, yspec,
                  pl.BlockSpec((1, tm, d), lambda bi, i: (bi, i, 0)),
                  _mod_spec(layer, mod_row, GT1, d),
                  _resident((None, d, d), lambda bi, i: (layer, 0, 0))],
        out_specs=pl.BlockSpec((1, tm, d), lambda bi, i: (bi, i, 0)),
        out_shape=jax.ShapeDtypeStruct((b, l, d), F32),
        compiler_params=_cparams(("parallel", "parallel")),
        name="out_proj",
    )(*ys, x, mods, w_out)


def _swiglu_kernel(x_ref, g_ref, sc_ref, sh_ref, gt_ref, wg_hbm, wu_hbm, wd_hbm, gf_ref, o_ref,
                   h_ref, wg_buf, wu_buf, wd_buf, sem, *, tm, tf, layer, final_norm):
    n_f = wg_hbm.shape[-1] // tf
    tile = pl.program_id(0) * pl.num_programs(1) + pl.program_id(1)
    n_tiles = pl.num_programs(0) * pl.num_programs(1)
    first_chunk = tile * n_f

    def chunk_copies(f, slot):
        cols = pl.ds(pl.multiple_of(f * tf, tf), tf)
        return (pltpu.make_async_copy(wg_hbm.at[layer, :, cols], wg_buf.at[slot], sem.at[0, slot]),
                pltpu.make_async_copy(wu_hbm.at[layer, :, cols], wu_buf.at[slot], sem.at[1, slot]),
                pltpu.make_async_copy(wd_hbm.at[layer, cols, :], wd_buf.at[slot], sem.at[2, slot]))

    def start_chunk(f, slot):
        for cp in chunk_copies(f, slot):
            cp.start()

    def wait_chunk(f, slot):
        for cp in chunk_copies(f, slot):
            cp.wait()

    def gated_ffn(h, slot):
        gate = jnp.dot(h, wg_buf[slot], preferred_element_type=F32)
        up = jnp.dot(h, wu_buf[slot], preferred_element_type=F32)
        a = (gate * jax.nn.sigmoid(gate) * up).astype(BF16)
        return gt_ref[...] * jnp.dot(a, wd_buf[slot], preferred_element_type=F32)

    @pl.when(tile == 0)
    def _():
        start_chunk(0, 0)

    slot0 = lax.rem(first_chunk, 2)
    wait_chunk(0, slot0)
    start_chunk(1, 1 - slot0)
    rb = min(tm, 512)
    for r in range(tm // rb):
        rows = slice(r * rb, (r + 1) * rb)
        xr = x_ref[0, rows, :]
        ms = jnp.mean(xr * xr, axis=-1, keepdims=True)
        h = (((xr * lax.rsqrt(ms + EPS)) * g_ref[0:1, :]) * (1.0 + sc_ref[...]) + sh_ref[...]).astype(BF16)
        h_ref[rows, :] = h
        o_ref[0, rows, :] = xr + gated_ffn(h, slot0)

    def chunk_step(f, carry):
        slot = lax.rem(first_chunk + f, 2)
        wait_chunk(f, slot)

        @pl.when(jnp.logical_or(f + 1 < n_f, tile + 1 < n_tiles))
        def _():
            start_chunk(lax.rem(f + 1, n_f), 1 - slot)

        o_ref[0] += gated_ffn(h_ref[...], slot)
        return carry

    lax.fori_loop(1, n_f, chunk_step, 0)

    if final_norm:
        def body(r, carry):
            rows = pl.ds(pl.multiple_of(r * 64, 64), 64)
            xr = o_ref[0, rows, :]
            ms = jnp.mean(xr * xr, axis=-1, keepdims=True)
            o_ref[0, rows, :] = (xr * lax.rsqrt(ms + EPS)) * gf_ref[0:1, :]
            return carry

        lax.fori_loop(0, tm // 64, body, 0)


def _swiglu_call(x, mods, g, w_gate, w_up, w_down, g_final, layer, mod_row, final_norm):
    b, l, d = x.shape
    ff = w_gate.shape[-1]
    tm = min(1024, l)
    tf = 512
    assert ff % tf == 0 and ff // tf >= 2
    kern = functools.partial(_swiglu_kernel, tm=tm, tf=tf, layer=layer, final_norm=final_norm)
    hbm = pl.BlockSpec(memory_space=pl.ANY)
    return pl.pallas_call(
        kern,
        grid=(b, l // tm),
        in_specs=[
            pl.BlockSpec((1, tm, d), lambda bi, i: (bi, i, 0)),
            pl.BlockSpec((None, g.shape[1], d), lambda bi, i: (layer, 0, 0)),
            _mod_spec(layer, mod_row, SC2, d),
            _mod_spec(layer, mod_row, SH2, d),
            _mod_spec(layer, mod_row, GT2, d),
            hbm, hbm, hbm,
            pl.BlockSpec(g_final.shape, lambda bi, i: (0, 0)),
        ],
        out_specs=pl.BlockSpec((1, tm, d), lambda bi, i: (bi, i, 0)),
        out_shape=jax.ShapeDtypeStruct((b, l, d), F32),
        scratch_shapes=[pltpu.VMEM((tm, d), BF16),
                        pltpu.VMEM((2, d, tf), BF16),
                        pltpu.VMEM((2, d, tf), BF16),
                        pltpu.VMEM((2, tf, d), BF16),
                        pltpu.SemaphoreType.DMA((3, 2))],
        compiler_params=_cparams(("arbitrary", "arbitrary")),
        name="swiglu",
    )(x, g, mods, mods, mods, w_gate, w_up, w_down, g_final)


def kernel(x, c, ctx, c_ctx, w_ada, b_ada, g_norm1, w_in, w_conv, sink, w_s, b_s, w_out, g_norm2,
           w_gate, w_up, w_down, g_final):
    batch, n_pos, d = x.shape
    n_ctx = ctx.shape[1]
    depth = w_ada.shape[0]
    cw = (d // 4) // N_GROUPS
    ctx_row = batch

    w_in_b = w_in.astype(BF16)
    w_out_b = w_out.astype(BF16)
    w_gate_b = w_gate.astype(BF16)
    w_up_b = w_up.astype(BF16)
    w_down_b = w_down.astype(BF16)
    g1 = jnp.broadcast_to(g_norm1[:, None, :], (depth, 2, d))
    g2 = jnp.broadcast_to(g_norm2[:, None, :], (depth, 2, d))
    gf = jnp.broadcast_to(g_final[None, :], (2, d))

    cs = jnp.concatenate([c, c_ctx[None, :], jnp.zeros((N_MOD_ROWS - batch - 1, d), F32)], axis=0)
    mods = _ada_call(cs, w_ada, b_ada).reshape(depth, N_MOD_ROWS, 1, 6 * d)

    cos_c, sin_c = _dft_tables(cw)
    ccsc = jnp.concatenate([cos_c, sin_c], axis=1).astype(BF16)
    use_fft = n_pos % (FFT_RADIX * 64) == 0 and n_pos // FFT_RADIX >= LANES
    if use_fft:
        ccsc_neg = jnp.concatenate([cos_c, -sin_c], axis=1).astype(BF16)
        fsub, twr, twi = _fft_tables(n_pos)
        latent_fourier = lambda pp: _fourier_fft_call(pp, ccsc_neg, fsub, twr, twi, f_block)
    else:
        cos_p, sin_p = _dft_tables(n_pos)
        cpsp = jnp.concatenate([cos_p, -sin_p], axis=1).astype(BF16)
        latent_fourier = lambda pp: _fourier_call(pp, ccsc, cpsp, f_block)
    cos_x, sin_x = _dft_tables(n_ctx)
    cpsp_ctx = jnp.concatenate([cos_x, -sin_x], axis=1).astype(BF16)
    rope_cos, rope_sin = _rope_tables(n_pos)
    mlp_bias = jnp.broadcast_to(b_s[:, :, :, None], b_s.shape + (cw,))
    off = _proj_offsets(d // 4)
    f_block = off["fourier"] // (N_GROUPS * cw)

    def flat(a):
        return a.reshape(1, -1, a.shape[-1])

    def unflat(a):
        return a.reshape(batch, n_ctx, a.shape[-1])

    xc = flat(ctx)
    for layer in range(depth):
        last = layer == depth - 1
        if last:
            w_kv = w_in_b[layer:layer + 1, :, off["k"]:off["u"]]
            pc = unflat(_norm_in_call(xc, mods, g1, w_kv, layer, 0, ctx_row, tn=256))
            kc_block = 0
        else:
            pc_flat = _norm_in_call(xc, mods, g1, w_in_b, layer, layer, ctx_row, tn=768)
            pc = unflat(pc_flat)
            kc_block = off["k"] // KV_WIDTH
            ys_c = (_conv_call(pc_flat, w_conv, layer, n_ctx),
                    flat(_fourier_call(pc, ccsc, cpsp_ctx, f_block)),
                    flat(_ctx_attn_call(pc, sink, layer)),
                    _chunk_mlp_call(pc_flat, w_s, mlp_bias, layer))
            xc_new = _out_proj_call(ys_c, xc, mods, w_out_b, layer, ctx_row)
            xc_next = _swiglu_call(xc_new, mods, g2, w_gate_b, w_up_b, w_down_b, gf, layer, ctx_row, False)
        p = _norm_in_call(x, mods, g1, w_in_b, layer, layer, None, tn=768)
        ys = (_conv_call(p, w_conv, layer, n_pos),
              latent_fourier(p),
              _latent_attn_call(p, pc, kc_block, sink, rope_cos, rope_sin, layer),
              _chunk_mlp_call(p, w_s, mlp_bias, layer))
        x = _out_proj_call(ys, x, mods, w_out_b, layer, None)
        x = _swiglu_call(x, mods, g2, w_gate_b, w_up_b, w_down_b, gf, layer, None, last)
        if not last:
            xc = xc_next
    return x
```

```python
import functools
import math

import jax
import jax.numpy as jnp
from jax import lax
from jax.experimental import pallas as pl
from jax.experimental.pallas import tpu as pltpu

F32 = jnp.float32
BF16 = jnp.bfloat16

EPS = 1e-6
NEG = -1e30
GRID_W = 64
HEAD_DIM = 64
N_HEADS = 8
N_KV_HEADS = 2
HEADS_PER_KV = N_HEADS // N_KV_HEADS
BLOCK = 128
CHUNK = 128
N_GROUPS = 4
ROPE_THETA = 10000.0
SCALE = HEAD_DIM ** -0.5
LOG2E = math.log2(math.e)
KV_WIDTH = N_KV_HEADS * HEAD_DIM


def _proj_offsets(width):
    off = {"conv": 0, "fourier": 3 * width, "q": 4 * width, "k": 5 * width}
    off["v"] = off["k"] + KV_WIDTH
    off["u"] = off["v"] + KV_WIDTH
    off["mv"] = off["u"] + width
    return off

LANES = 128
VMEM_LIMIT_BYTES = 60 * 1024 * 1024
N_MOD_ROWS = 24

SH1, SC1, GT1, SH2, SC2, GT2 = range(6)


def _cparams(sem):
    return pltpu.CompilerParams(dimension_semantics=sem, vmem_limit_bytes=VMEM_LIMIT_BYTES)


def _resident(block_shape, index_map):
    return pl.BlockSpec(block_shape, index_map, pipeline_mode=pl.Buffered(1))


def _ada_kernel(c_ref, w_ref, b_ref, o_ref):
    c = c_ref[...]
    s = (c * jax.nn.sigmoid(c)).astype(BF16)
    o_ref[...] = jnp.dot(s, w_ref[...].astype(BF16), preferred_element_type=F32) + b_ref[...]


def _ada_call(cs, w_ada, b_ada):
    depth, d, n = w_ada.shape
    tn = 2048
    return pl.pallas_call(
        _ada_kernel,
        grid=(depth, n // tn),
        in_specs=[
            pl.BlockSpec((N_MOD_ROWS, d), lambda l, j: (0, 0)),
            pl.BlockSpec((None, d, tn), lambda l, j: (l, 0, j)),
            pl.BlockSpec((None, 1, tn), lambda l, j: (l, 0, j)),
        ],
        out_specs=pl.BlockSpec((None, N_MOD_ROWS, tn), lambda l, j: (l, 0, j)),
        out_shape=jax.ShapeDtypeStruct((depth, N_MOD_ROWS, n), F32),
        compiler_params=_cparams(("parallel", "parallel")),
        name="ada_mod",
    )(cs, w_ada, b_ada.reshape(depth, 1, n))


def _mod_spec(layer, row, chunk, d):
    if row is None:
        return pl.BlockSpec((None, None, 1, d), lambda b, *_: (layer, b, 0, chunk))
    return pl.BlockSpec((None, None, 1, d), lambda b, *_: (layer, row, 0, chunk))


def _norm_modulate_rows(x_ref, g_ref, sc_ref, sh_ref, h_ref, tm, rb):
    def body(r, carry):
        rows = pl.ds(pl.multiple_of(r * rb, rb), rb)
        xr = x_ref[0, rows, :]
        ms = jnp.mean(xr * xr, axis=-1, keepdims=True)
        y = xr * lax.rsqrt(ms + EPS)
        h = (y * g_ref[0:1, :]) * (1.0 + sc_ref[...]) + sh_ref[...]
        h_ref[rows, :] = h.astype(h_ref.dtype)
        return carry

    lax.fori_loop(0, tm // rb, body, 0)


def _norm_in_kernel(x_ref, xn_ref, g_ref, sc_ref, sh_ref, scn_ref, shn_ref, w_ref, o_ref, ha_ref, hb_ref,
                    *, tm, tn):
    t = pl.program_id(0) * pl.num_programs(1) + pl.program_id(1)
    n = o_ref.shape[-1]

    @pl.when(t == 0)
    def _():
        _norm_modulate_rows(x_ref, g_ref, sc_ref, sh_ref, ha_ref, tm, min(tm, 64))

    def step(h_cur, h_next):
        rb = min(tm, 32)
        n_rb = tm // rb
        ct = 256 if tn % 256 == 0 else tn
        n_ct = n // ct
        anchors = [None] * n_ct
        for r in range(n_rb):
            rows = slice(r * rb, (r + 1) * rb)
            xr = xn_ref[0, rows, :]
            ms = jnp.mean(xr * xr, axis=-1, keepdims=True)
            h = ((xr * lax.rsqrt(ms + EPS)) * g_ref[0:1, :]) * (1.0 + scn_ref[...]) + shn_ref[...]
            hb = h.astype(BF16)
            h_next[rows, :] = hb
            folded = hb[:, 0:LANES]
            for k in range(1, hb.shape[1] // LANES):
                folded = folded + hb[:, k * LANES:(k + 1) * LANES]
            while folded.shape[0] > 16:
                half_rows = folded.shape[0] // 2
                folded = folded[:half_rows] + folded[half_rows:]
            word = pltpu.bitcast(folded, jnp.uint32)
            zero = lax.shift_right_logical(lax.shift_right_logical(word, jnp.uint32(16)), jnp.uint32(16))
            j = r * n_ct // n_rb
            anchors[j] = zero if anchors[j] is None else anchors[j] | zero
        for j in range(n // tn):
            cols = slice(j * tn, (j + 1) * tn)
            out = jnp.dot(h_cur[...], w_ref[:, cols], preferred_element_type=F32)
            pieces = []
            for s in range(tn // ct):
                piece = out[:, s * ct:(s + 1) * ct]
                anchor = anchors[j * (tn // ct) + s]
                if anchor is not None and ct % LANES == 0:
                    zrow = anchor[0:1, :].astype(F32)
                    piece = piece + jnp.concatenate([zrow] * (ct // LANES), axis=1)
                pieces.append(piece)
            o_ref[0, :, cols] = jnp.concatenate(pieces, axis=1).astype(o_ref.dtype)

    parity = lax.rem(t, 2)

    @pl.when(parity == 0)
    def _():
        step(ha_ref, hb_ref)

    @pl.when(parity == 1)
    def _():
        step(hb_ref, ha_ref)


def _norm_in_call(x, mods, g, w, layer, w_layer, mod_row, tn):
    b, l, d = x.shape
    n = w.shape[-1]
    tm = min(512, l)
    nt = l // tm
    kern = functools.partial(_norm_in_kernel, tm=tm, tn=tn)

    def nxt(bi, i):
        t = jnp.minimum(bi * nt + i + 1, b * nt - 1)
        return t // nt, t % nt

    def mod_next(chunk):
        if mod_row is None:
            return pl.BlockSpec((None, None, 1, d), lambda bi, i: (layer, nxt(bi, i)[0], 0, chunk))
        return _mod_spec(layer, mod_row, chunk, d)

    return pl.pallas_call(
        kern,
        grid=(b, nt),
        in_specs=[
            pl.BlockSpec((1, tm, d), lambda bi, i: (0, 0, 0)),
            pl.BlockSpec((1, tm, d), lambda bi, i: nxt(bi, i) + (0,)),
            pl.BlockSpec((None, g.shape[1], d), lambda bi, i: (layer, 0, 0)),
            _mod_spec(layer, mod_row, SC1, d),
            _mod_spec(layer, mod_row, SH1, d),
            mod_next(SC1),
            mod_next(SH1),
            _resident((None, d, n), lambda bi, i: (w_layer, 0, 0)),
        ],
        out_specs=pl.BlockSpec((1, tm, n), lambda bi, i: (bi, i, 0)),
        out_shape=jax.ShapeDtypeStruct((b, l, n), BF16),
        scratch_shapes=[pltpu.VMEM((tm, d), BF16), pltpu.VMEM((tm, d), BF16)],
        compiler_params=_cparams(("arbitrary", "arbitrary")),
        name="norm_in",
    )(x, x, g, mods, mods, mods, mods, w)


def _conv_kernel(x_ref, gb_ref, gc_ref, w_ref, o_ref, *, seq_len):
    n_rows = x_ref.shape[1]
    pos = lax.broadcasted_iota(jnp.int32, (n_rows, LANES), 0) % seq_len
    first = pos == 0
    last = pos == seq_len - 1
    for s in range(x_ref.shape[2] // LANES):
        cols = slice(s * LANES, (s + 1) * LANES)
        z = gc_ref[0, :, cols].astype(F32) * x_ref[0, :, cols].astype(F32)
        z_prev = jnp.where(first, 0.0, pltpu.roll(z, 1, 0))
        z_next = jnp.where(last, 0.0, pltpu.roll(z, n_rows - 1, 0))
        y = z_prev * w_ref[0:1, cols] + z * w_ref[1:2, cols] + z_next * w_ref[2:3, cols]
        o_ref[0, :, cols] = (gb_ref[0, :, cols].astype(F32) * y).astype(o_ref.dtype)


def _conv_call(p, w_conv, layer, seq_len):
    b, l, _ = p.shape
    width = w_conv.shape[-1]
    blk = lambda j: pl.BlockSpec((1, l, width), lambda bi: (bi, 0, j))
    return pl.pallas_call(
        functools.partial(_conv_kernel, seq_len=seq_len),
        grid=(b,),
        in_specs=[blk(0), blk(1), blk(2),
                  pl.BlockSpec((None, 3, width), lambda bi: (layer, 0, 0))],
        out_specs=pl.BlockSpec((1, l, width), lambda bi: (bi, 0, 0)),
        out_shape=jax.ShapeDtypeStruct((b, l, width), BF16),
        compiler_params=_cparams(("parallel",)),
        name="short_conv",
    )(p, p, p, w_conv)


def _fourier_kernel(z_ref, ccsc_ref, cpsp_ref, o_ref, rhs_ref, *, out_scale):
    n_pos = z_ref.shape[1]
    cw = ccsc_ref.shape[0]
    for g in range(N_GROUPS):
        cols = slice(g * cw, (g + 1) * cw)
        ab = jnp.dot(z_ref[0, :, cols], ccsc_ref[...], preferred_element_type=F32)
        rhs_ref[0:n_pos, cols] = ab[:, :cw].astype(BF16)
        rhs_ref[n_pos:2 * n_pos, cols] = ab[:, cw:].astype(BF16)
    out = jnp.dot(cpsp_ref[...], rhs_ref[...], preferred_element_type=F32)
    o_ref[0] = (out * out_scale).astype(o_ref.dtype)


def _fourier_call(p, ccsc, cpsp, col_block):
    b, l, _ = p.shape
    cw = ccsc.shape[0]
    width = N_GROUPS * cw
    kern = functools.partial(_fourier_kernel, out_scale=1.0 / math.sqrt(l * cw))
    return pl.pallas_call(
        kern,
        grid=(b,),
        in_specs=[
            pl.BlockSpec((1, l, width), lambda bi: (bi, 0, col_block)),
            _resident((cw, 2 * cw), lambda bi: (0, 0)),
            _resident((l, 2 * l), lambda bi: (0, 0)),
        ],
        out_specs=pl.BlockSpec((1, l, width), lambda bi: (bi, 0, 0)),
        out_shape=jax.ShapeDtypeStruct((b, l, width), BF16),
        scratch_shapes=[pltpu.VMEM((2 * l, width), BF16)],
        compiler_params=_cparams(("parallel",)),
        name="fourier_mix",
    )(p, ccsc, cpsp)


def _dft_tables(n):
    k = jnp.arange(n, dtype=jnp.int32)
    ang = ((k[:, None] * k[None, :]) % n).astype(F32) * (2.0 * math.pi / n)
    return jnp.cos(ang), jnp.sin(ang)


FFT_RADIX = 8


def _fourier_fft_kernel(z_ref, ccsc_ref, fsub_ref, twr_ref, twi_ref, o_ref, w_ref, g_ref, *, out_scale):
    n_pos = z_ref.shape[1]
    m = n_pos // FFT_RADIX
    cw = ccsc_ref.shape[0]
    c = math.sqrt(0.5)
    rb = 64
    gpp = 2
    for gp in range(N_GROUPS // gpp):
        groups = range(gp * gpp, (gp + 1) * gpp)
        pcols = slice(gp * gpp * cw, (gp + 1) * gpp * cw)
        for g in groups:
            ab = jnp.dot(z_ref[0, :, g * cw:(g + 1) * cw], ccsc_ref[...], preferred_element_type=F32)
            w_ref[g] = ab[:, :cw]
            w_ref[N_GROUPS + g] = ab[:, cw:]
        for n2 in range(FFT_RADIX):
            sub = pl.ds(n2, m, stride=FFT_RADIX)
            rhs = jnp.concatenate(
                [jnp.concatenate([w_ref[part * N_GROUPS + g, sub, :] for g in groups], axis=1)
                 for part in range(2)], axis=0).astype(BF16)
            g_ref[n2, :, pcols] = jnp.dot(fsub_ref[...], rhs, preferred_element_type=F32)
        for r in range(m // rb):
            rows = slice(r * rb, (r + 1) * rb)
            irows = slice(m + r * rb, m + (r + 1) * rb)
            for g in groups:
                cols = slice(g * cw, (g + 1) * cw)
                hr, hi = [g_ref[0, rows, cols]], [g_ref[0, irows, cols]]
                for n2 in range(1, FFT_RADIX):
                    gr, gi = g_ref[n2, rows, cols], g_ref[n2, irows, cols]
                    tr, ti = twr_ref[n2, rows, :], twi_ref[n2, rows, :]
                    hr.append(tr * gr - ti * gi)
                    hi.append(tr * gi + ti * gr)
                a0, a1 = hr[0] + hr[4], hr[0] - hr[4]
                a2 = hr[2] + hr[6]
                b0, b1 = hr[1] + hr[5], hr[1] - hr[5]
                b2, b3 = hr[3] + hr[7], hr[3] - hr[7]
                d1, d2, d3 = hi[1] - hi[5], hi[2] - hi[6], hi[3] - hi[7]
                e0, e1 = a0 + a2, b0 + b2
                odd_r = c * (b1 - b3)
                odd_i = c * (d1 + d3)
                p1, p3 = a1 + odd_r, a1 - odd_r
                p2 = a0 - a2
                q1, q3 = odd_i + d2, odd_i - d2
                q2 = (hi[1] + hi[5]) - (hi[3] + hi[7])
                ys = (e0 + e1, p1 + q1, p2 + q2, p3 + q3, e0 - e1, p3 - q3, p2 - q2, p1 - q1)
                for k2, y in enumerate(ys):
                    orow = slice(k2 * m + r * rb, k2 * m + (r + 1) * rb)
                    o_ref[0, orow, cols] = (y * out_scale).astype(o_ref.dtype)


def _fourier_fft_call(p, ccsc_neg, fsub, twr, twi, col_block):
    b, l, _ = p.shape
    cw = ccsc_neg.shape[0]
    width = N_GROUPS * cw
    m = l // FFT_RADIX
    kern = functools.partial(_fourier_fft_kernel, out_scale=1.0 / math.sqrt(l * cw))
    return pl.pallas_call(
        kern,
        grid=(b,),
        in_specs=[
            pl.BlockSpec((1, l, width), lambda bi: (bi, 0, col_block)),
            _resident((cw, 2 * cw), lambda bi: (0, 0)),
            _resident((2 * m, 2 * m), lambda bi: (0, 0)),
            _resident((FFT_RADIX, m, LANES), lambda bi: (0, 0, 0)),
            _resident((FFT_RADIX, m, LANES), lambda bi: (0, 0, 0)),
        ],
        out_specs=pl.BlockSpec((1, l, width), lambda bi: (bi, 0, 0)),
        out_shape=jax.ShapeDtypeStruct((b, l, width), BF16),
        scratch_shapes=[pltpu.VMEM((2 * N_GROUPS, l, cw), F32),
                        pltpu.VMEM((FFT_RADIX, 2 * m, width), F32)],
        compiler_params=_cparams(("parallel",)),
        name="fourier_fft",
    )(p, ccsc_neg, fsub, twr, twi)


def _fft_tables(n_pos):
    m = n_pos // FFT_RADIX
    cos_m, sin_m = _dft_tables(m)
    fsub = jnp.concatenate([jnp.concatenate([cos_m, sin_m], axis=1),
                            jnp.concatenate([-sin_m, cos_m], axis=1)], axis=0).astype(BF16)
    n2 = jnp.arange(FFT_RADIX, dtype=jnp.int32)[:, None]
    k1 = jnp.arange(m, dtype=jnp.int32)[None, :]
    ang = ((n2 * k1) % n_pos).astype(F32) * (2.0 * math.pi / n_pos)
    bcast = lambda t: jnp.broadcast_to(t[:, :, None], (FFT_RADIX, m, LANES))
    return fsub, bcast(jnp.cos(ang)), bcast(-jnp.sin(ang))


def _nt_dot(a, b):
    return lax.dot_general(a, b, (((1,), (1,)), ((), ())), preferred_element_type=F32)


def _latent_attn_kernel(sink_ref, q_ref, k_ref, v_ref, kc_ref, vc_ref, cos_ref, sin_ref, o_ref,
                        qlo_ref, qhi_ref, kpad_ref, kswp_ref, vt_ref, kcs_ref, vct_ref, *, layer):
    n_pos = q_ref.shape[1]
    n_blk = n_pos // BLOCK
    n_ctx = kc_ref.shape[1]
    half = LANES // 2

    lane = lax.broadcasted_iota(jnp.int32, (BLOCK, LANES), 1)
    low16 = (lane % 32) < 16
    lo_half = lane < half

    def prep_block(r, carry):
        rows = pl.ds(pl.multiple_of(r * BLOCK, BLOCK), BLOCK)
        cos = cos_ref[rows, :]
        sin = sin_ref[rows, :]

        def rope(t):
            swapped = jnp.where(low16, pltpu.roll(t, LANES - 16, 1), pltpu.roll(t, 16, 1))
            return t * cos + swapped * sin

        for s in range(q_ref.shape[-1] // LANES):
            cols = slice(s * LANES, (s + 1) * LANES)
            t = rope(q_ref[0, rows, cols].astype(F32)) * (SCALE * LOG2E)
            qlo_ref[rows, cols] = jnp.where(lo_half, t, 0.0).astype(BF16)
            qhi_ref[rows, cols] = jnp.where(lo_half, 0.0, t).astype(BF16)
        prow = pl.ds(pl.multiple_of(r * BLOCK + BLOCK, BLOCK), BLOCK)
        tk = rope(k_ref[0, rows, :].astype(F32))
        kpad_ref[prow, :] = tk.astype(BF16)
        kswp_ref[prow, :] = pltpu.roll(tk, half, 1).astype(BF16)
        vt_ref[r + 1] = v_ref[0, rows, :].astype(F32).T.astype(BF16)
        return carry

    zeros = jnp.zeros((BLOCK, LANES), BF16)
    for ref in (kpad_ref, kswp_ref):
        ref[0:BLOCK, :] = zeros
        ref[n_pos + BLOCK:n_pos + 2 * BLOCK, :] = zeros
    vt_ref[0] = zeros
    vt_ref[n_blk + 1] = zeros
    kcs_ref[...] = pltpu.roll(kc_ref[0].astype(F32), half, 1).astype(BF16)
    for cb in range(n_ctx // BLOCK):
        crow = slice(cb * BLOCK, (cb + 1) * BLOCK)
        vct_ref[:, crow] = vc_ref[0, crow, :].astype(F32).T.astype(BF16)
    lax.fori_loop(0, n_blk, prep_block, 0, unroll=4)

    head_order = (0, 2, 5, 7, 1, 3, 4, 6)
    col_of = {h: i for i, h in enumerate(head_order)}
    n_col = N_HEADS * BLOCK
    kj = lax.broadcasted_iota(jnp.int32, (3 * BLOCK, BLOCK), 0)
    qi = lax.broadcasted_iota(jnp.int32, (3 * BLOCK, BLOCK), 1)
    in_window = (kj >= qi) & (kj <= qi + 2 * BLOCK)
    sk = jnp.concatenate([jnp.full((1, BLOCK), sink_ref[layer, h] * LOG2E, F32) for h in head_order], axis=1)
    neg = NEG * LOG2E

    def slab(ref, rows, s):
        return ref[rows, s * LANES:(s + 1) * LANES]

    def attn_block(n, carry):
        rows = pl.ds(pl.multiple_of(n * BLOCK, BLOCK), BLOCK)
        band = pl.ds(pl.multiple_of(n * BLOCK, BLOCK), 3 * BLOCK)
        valid = in_window & (kj >= (1 - n) * BLOCK) & (kj < n_pos - (n - 1) * BLOCK)

        def masked(s):
            lo, hi = slice(0, BLOCK), slice(2 * BLOCK, 3 * BLOCK)
            cols = [s[:, i * BLOCK:(i + 1) * BLOCK] for i in range(s.shape[1] // BLOCK)]
            return jnp.concatenate(
                [jnp.concatenate([jnp.where(valid[lo], c[lo], neg), c[BLOCK:2 * BLOCK],
                                  jnp.where(valid[hi], c[hi], neg)], axis=0) for c in cols], axis=1)

        rhs_nat = jnp.concatenate([slab(qlo_ref, rows, 0), slab(qlo_ref, rows, 1),
                                   slab(qhi_ref, rows, 2), slab(qhi_ref, rows, 3)], axis=0)
        rhs_swp = jnp.concatenate([slab(qhi_ref, rows, 0), slab(qhi_ref, rows, 1),
                                   slab(qlo_ref, rows, 2), slab(qlo_ref, rows, 3)], axis=0)
        s_loc = jnp.concatenate([masked(_nt_dot(kpad_ref[band, :], rhs_nat)),
                                 masked(_nt_dot(kswp_ref[band, :], rhs_swp))], axis=1)
        s_ctx = jnp.concatenate([_nt_dot(kc_ref[0], rhs_nat),
                                 _nt_dot(kcs_ref[...], rhs_swp)], axis=1)
        m = jnp.maximum(jnp.maximum(jnp.max(s_loc, axis=0, keepdims=True),
                                    jnp.max(s_ctx, axis=0, keepdims=True)), sk)
        e_loc = jnp.exp2(s_loc - m)
        e_ctx = jnp.exp2(s_ctx - m)
        inv_den = 1.0 / (jnp.sum(e_loc, axis=0, keepdims=True) + jnp.sum(e_ctx, axis=0, keepdims=True)
                         + jnp.exp2(sk - m))
        e_loc = e_loc.astype(BF16)
        e_ctx = e_ctx.astype(BF16)
        vt_band = jnp.concatenate([vt_ref[n], vt_ref[n + 1], vt_ref[n + 2]], axis=1)
        o_t = {}
        for g in range(N_KV_HEADS):
            heads = [4 * g + j for j in range(HEADS_PER_KV)]
            pick = lambda arr: jnp.concatenate(
                [arr[:, col_of[h] * BLOCK:(col_of[h] + 1) * BLOCK] for h in heads], axis=1)
            vrows = slice(g * HEAD_DIM, (g + 1) * HEAD_DIM)
            o = (jnp.dot(vt_band[vrows], pick(e_loc), preferred_element_type=F32)
                 + jnp.dot(vct_ref[vrows, :], pick(e_ctx), preferred_element_type=F32)) * pick(inv_den)
            for j, h in enumerate(heads):
                o_t[h] = o[:, j * BLOCK:(j + 1) * BLOCK]
        slabs = [jnp.concatenate([o_t[2 * s], o_t[2 * s + 1]], axis=0).T for s in range(N_HEADS // 2)]
        o_ref[0, rows, :] = jnp.concatenate(slabs, axis=1).astype(o_ref.dtype)
        return carry

    lax.fori_loop(0, n_blk, attn_block, 0, unroll=True)


def _latent_attn_call(p, pc, kc_block, sink, cos_t, sin_t, layer):
    b, l, _ = p.shape
    lc = pc.shape[1]
    qw = N_HEADS * HEAD_DIM
    kvw = N_KV_HEADS * HEAD_DIM
    off = _proj_offsets(qw)
    q_block = off["q"] // qw
    k_block = off["k"] // kvw
    kern = functools.partial(_latent_attn_kernel, layer=layer)
    return pl.pallas_call(
        kern,
        grid=(b,),
        in_specs=[
            pl.BlockSpec(memory_space=pltpu.SMEM),
            pl.BlockSpec((1, l, qw), lambda bi: (bi, 0, q_block)),
            pl.BlockSpec((1, l, kvw), lambda bi: (bi, 0, k_block)),
            pl.BlockSpec((1, l, kvw), lambda bi: (bi, 0, k_block + 1)),
            pl.BlockSpec((1, lc, kvw), lambda bi: (bi, 0, kc_block)),
            pl.BlockSpec((1, lc, kvw), lambda bi: (bi, 0, kc_block + 1)),
            _resident((l, LANES), lambda bi: (0, 0)),
            _resident((l, LANES), lambda bi: (0, 0)),
        ],
        out_specs=pl.BlockSpec((1, l, qw), lambda bi: (bi, 0, 0)),
        out_shape=jax.ShapeDtypeStruct((b, l, qw), BF16),
        scratch_shapes=[pltpu.VMEM((l, qw), BF16),
                        pltpu.VMEM((l, qw), BF16),
                        pltpu.VMEM((l + 2 * BLOCK, kvw), BF16),
                        pltpu.VMEM((l + 2 * BLOCK, kvw), BF16),
                        pltpu.VMEM((l // BLOCK + 2, kvw, BLOCK), BF16),
                        pltpu.VMEM((lc, kvw), BF16),
                        pltpu.VMEM((kvw, lc), BF16)],
        compiler_params=_cparams(("parallel",)),
        name="latent_attention",
    )(sink, p, p, p, pc, pc, cos_t, sin_t)


def _ctx_attn_kernel(sink_ref, q_ref, kc_ref, vc_ref, o_ref, *, layer):
    outs = []
    for h in range(N_HEADS):
        kv = h // HEADS_PER_KV
        hc = slice(kv * HEAD_DIM, (kv + 1) * HEAD_DIM)
        qh = q_ref[0, :, h * HEAD_DIM:(h + 1) * HEAD_DIM] * SCALE
        s = _nt_dot(qh, kc_ref[0, :, hc])
        sk = sink_ref[layer, h]
        m = jnp.maximum(jnp.max(s, axis=-1, keepdims=True), sk)
        e = jnp.exp(s - m)
        den = jnp.sum(e, axis=-1, keepdims=True) + jnp.exp(sk - m)
        o = jnp.dot(e.astype(BF16), vc_ref[0, :, hc], preferred_element_type=F32)
        outs.append(o / den)
    o_ref[0] = jnp.concatenate(outs, axis=-1).astype(o_ref.dtype)


def _ctx_attn_call(pc, sink, layer):
    b, lc, _ = pc.shape
    qw = N_HEADS * HEAD_DIM
    kvw = N_KV_HEADS * HEAD_DIM
    off = _proj_offsets(qw)
    kern = functools.partial(_ctx_attn_kernel, layer=layer)
    return pl.pallas_call(
        kern,
        grid=(b,),
        in_specs=[
            pl.BlockSpec(memory_space=pltpu.SMEM),
            pl.BlockSpec((1, lc, qw), lambda bi: (bi, 0, off["q"] // qw)),
            pl.BlockSpec((1, lc, kvw), lambda bi: (bi, 0, off["k"] // kvw)),
            pl.BlockSpec((1, lc, kvw), lambda bi: (bi, 0, off["v"] // kvw)),
        ],
        out_specs=pl.BlockSpec((1, lc, qw), lambda bi: (bi, 0, 0)),
        out_shape=jax.ShapeDtypeStruct((b, lc, qw), BF16),
        compiler_params=_cparams(("parallel",)),
        name="context_attention",
    )(sink, pc, pc, pc)


def _rope_tables(n_pos):
    quarter = HEAD_DIM // 4
    inv = ROPE_THETA ** (-jnp.arange(quarter, dtype=F32) / quarter)
    pos = jnp.arange(n_pos, dtype=jnp.int32)
    row = (pos // GRID_W).astype(F32)
    col = (pos % GRID_W).astype(F32)
    a_row = row[:, None] * inv[None, :]
    a_col = col[:, None] * inv[None, :]
    ang = jnp.concatenate([a_row, a_row, a_col, a_col], axis=-1)
    sign = jnp.tile(jnp.concatenate([-jnp.ones((quarter,), F32), jnp.ones((quarter,), F32)]), 2)
    reps = LANES // HEAD_DIM
    return jnp.tile(jnp.cos(ang), (1, reps)), jnp.tile(jnp.sin(ang) * sign, (1, reps))


def _gelu_tanh(x):
    k1 = -2.0 * math.sqrt(2.0 / math.pi) * LOG2E
    t = x * (k1 + (k1 * 0.044715) * (x * x))
    return x * (1.0 / (1.0 + jnp.exp2(t)))


def _chunk_mlp_kernel(u_ref, v_ref, w_ref, bias_ref, o_ref):
    n_rows = u_ref.shape[1]
    averager = jnp.full((LANES, LANES), 1.0 / LANES, BF16)
    for gi in range(u_ref.shape[2] // LANES):
        cols = slice(gi * LANES, (gi + 1) * LANES)
        u = _gelu_tanh(u_ref[0, :, cols].astype(F32))
        v = _gelu_tanh(v_ref[0, :, cols].astype(F32))
        mu = jnp.dot(v.astype(BF16), averager, preferred_element_type=F32)
        dlt = v - mu
        var = jnp.dot((dlt * dlt).astype(BF16), averager, preferred_element_type=F32)
        vn = (dlt * lax.rsqrt(var + EPS)).astype(BF16)
        w = w_ref[gi].astype(BF16)
        for n in range(n_rows // CHUNK):
            rows = slice(n * CHUNK, (n + 1) * CHUNK)
            mixed = jnp.dot(w, vn[rows], preferred_element_type=F32) + bias_ref[gi]
            o_ref[0, rows, cols] = (u[rows] * mixed).astype(o_ref.dtype)


def _chunk_mlp_call(p, w_s, bias, layer):
    b, l, _ = p.shape
    gpb = 2
    bw = gpb * LANES
    off = _proj_offsets(N_GROUPS * LANES)
    u_block = off["u"] // bw
    v_block = off["mv"] // bw
    return pl.pallas_call(
        _chunk_mlp_kernel,
        grid=(b, N_GROUPS // gpb),
        in_specs=[
            pl.BlockSpec((1, l, bw), lambda bi, j: (bi, 0, u_block + j)),
            pl.BlockSpec((1, l, bw), lambda bi, j: (bi, 0, v_block + j)),
            pl.BlockSpec((None, gpb, CHUNK, CHUNK), lambda bi, j: (layer, j, 0, 0)),
            pl.BlockSpec((None, gpb, CHUNK, LANES), lambda bi, j: (layer, j, 0, 0)),
        ],
        out_specs=pl.BlockSpec((1, l, bw), lambda bi, j: (bi, 0, j)),
        out_shape=jax.ShapeDtypeStruct((b, l, N_GROUPS * LANES), BF16),
        compiler_params=_cparams(("parallel", "parallel")),
        name="chunk_mlp",
    )(p, p, w_s, bias)


def _out_proj_kernel(ya_ref, yb_ref, yc_ref, yd_ref, x_ref, gt_ref, w_ref, o_ref, *, tn):
    d = o_ref.shape[-1]
    kw = ya_ref.shape[-1]
    ys = (ya_ref, yb_ref, yc_ref, yd_ref)
    for j in range(d // tn):
        cols = slice(j * tn, (j + 1) * tn)
        acc = jnp.dot(ys[0][0], w_ref[0:kw, cols], preferred_element_type=F32)
        for k in range(1, 4):
            acc = acc + jnp.dot(ys[k][0], w_ref[k * kw:(k + 1) * kw, cols], preferred_element_type=F32)
        o_ref[0, :, cols] = x_ref[0, :, cols] + gt_ref[:, cols] * acc


def _out_proj_call(ys, x, mods, w_out, layer, mod_row):
    b, l, d = x.shape
    kw = ys[0].shape[-1]
    tm = min(1024, l)
    kern = functools.partial(_out_proj_kernel, tn=512)
    yspec = pl.BlockSpec((1, tm, kw), lambda bi, i: (bi, i, 0))
    return pl.pallas_call(
        kern,
        grid=(b, l // tm),
        in_specs=[yspec, yspec, yspec, yspec,
                  pl.BlockSpec((1, tm, d), lambda bi, i: (bi, i, 0)),
                  _mod_spec(layer, mod_row, GT1, d),
                  _resident((None, d, d), lambda bi, i: (layer, 0, 0))],
        out_specs=pl.BlockSpec((1, tm, d), lambda bi, i: (bi, i, 0)),
        out_shape=jax.ShapeDtypeStruct((b, l, d), F32),
        compiler_params=_cparams(("parallel", "parallel")),
        name="out_proj",
    )(*ys, x, mods, w_out)


def _swiglu_kernel(x_ref, g_ref, sc_ref, sh_ref, gt_ref, wg_hbm, wu_hbm, wd_hbm, gf_ref, o_ref,
                   h_ref, wg_buf, wu_buf, wd_buf, sem, *, tm, tf, layer, final_norm):
    n_f = wg_hbm.shape[-1] // tf
    tile = pl.program_id(0) * pl.num_programs(1) + pl.program_id(1)
    n_tiles = pl.num_programs(0) * pl.num_programs(1)
    first_chunk = tile * n_f

    def chunk_copies(f, slot):
        cols = pl.ds(pl.multiple_of(f * tf, tf), tf)
        return (pltpu.make_async_copy(wg_hbm.at[layer, :, cols], wg_buf.at[slot], sem.at[0, slot]),
                pltpu.make_async_copy(wu_hbm.at[layer, :, cols], wu_buf.at[slot], sem.at[1, slot]),
                pltpu.make_async_copy(wd_hbm.at[layer, cols, :], wd_buf.at[slot], sem.at[2, slot]))

    def start_chunk(f, slot):
        for cp in chunk_copies(f, slot):
            cp.start()

    def wait_chunk(f, slot):
        for cp in chunk_copies(f, slot):
            cp.wait()

    def gated_ffn(h, slot):
        gate = jnp.dot(h, wg_buf[slot], preferred_element_type=F32)
        up = jnp.dot(h, wu_buf[slot], preferred_element_type=F32)
        a = (gate * jax.nn.sigmoid(gate) * up).astype(BF16)
        return gt_ref[...] * jnp.dot(a, wd_buf[slot], preferred_element_type=F32)

    @pl.when(tile == 0)
    def _():
        start_chunk(0, 0)

    slot0 = lax.rem(first_chunk, 2)
    wait_chunk(0, slot0)
    start_chunk(1, 1 - slot0)
    rb = min(tm, 512)
    for r in range(tm // rb):
        rows = slice(r * rb, (r + 1) * rb)
        xr = x_ref[0, rows, :]
        ms = jnp.mean(xr * xr, axis=-1, keepdims=True)
        h = (((xr * lax.rsqrt(ms + EPS)) * g_ref[0:1, :]) * (1.0 + sc_ref[...]) + sh_ref[...]).astype(BF16)
        h_ref[rows, :] = h
        o_ref[0, rows, :] = xr + gated_ffn(h, slot0)

    def chunk_step(f, carry):
        slot = lax.rem(first_chunk + f, 2)
        wait_chunk(f, slot)

        @pl.when(jnp.logical_or(f + 1 < n_f, tile + 1 < n_tiles))
        def _():
            start_chunk(lax.rem(f + 1, n_f), 1 - slot)

        o_ref[0] += gated_ffn(h_ref[...], slot)
        return carry

    lax.fori_loop(1, n_f, chunk_step, 0)

    if final_norm:
        def body(r, carry):
            rows = pl.ds(pl.multiple_of(r * 64, 64), 64)
            xr = o_ref[0, rows, :]
            ms = jnp.mean(xr * xr, axis=-1, keepdims=True)
            o_ref[0, rows, :] = (xr * lax.rsqrt(ms + EPS)) * gf_ref[0:1, :]
            return carry

        lax.fori_loop(0, tm // 64, body, 0)


def _swiglu_call(x, mods, g, w_gate, w_up, w_down, g_final, layer, mod_row, final_norm):
    b, l, d = x.shape
    ff = w_gate.shape[-1]
    tm = min(1024, l)
    tf = 512
    assert ff % tf == 0 and ff // tf >= 2
    kern = functools.partial(_swiglu_kernel, tm=tm, tf=tf, layer=layer, final_norm=final_norm)
    hbm = pl.BlockSpec(memory_space=pl.ANY)
    return pl.pallas_call(
        kern,
        grid=(b, l // tm),
        in_specs=[
            pl.BlockSpec((1, tm, d), lambda bi, i: (bi, i, 0)),
            pl.BlockSpec((None, g.shape[1], d), lambda bi, i: (layer, 0, 0)),
            _mod_spec(layer, mod_row, SC2, d),
            _mod_spec(layer, mod_row, SH2, d),
            _mod_spec(layer, mod_row, GT2, d),
            hbm, hbm, hbm,
            pl.BlockSpec(g_final.shape, lambda bi, i: (0, 0)),
        ],
        out_specs=pl.BlockSpec((1, tm, d), lambda bi, i: (bi, i, 0)),
        out_shape=jax.ShapeDtypeStruct((b, l, d), F32),
        scratch_shapes=[pltpu.VMEM((tm, d), BF16),
                        pltpu.VMEM((2, d, tf), BF16),
                        pltpu.VMEM((2, d, tf), BF16),
                        pltpu.VMEM((2, tf, d), BF16),
                        pltpu.SemaphoreType.DMA((3, 2))],
        compiler_params=_cparams(("arbitrary", "arbitrary")),
        name="swiglu",
    )(x, g, mods, mods, mods, w_gate, w_up, w_down, g_final)


def kernel(x, c, ctx, c_ctx, w_ada, b_ada, g_norm1, w_in, w_conv, sink, w_s, b_s, w_out, g_norm2,
           w_gate, w_up, w_down, g_final):
    batch, n_pos, d = x.shape
    n_ctx = ctx.shape[1]
    depth = w_ada.shape[0]
    cw = (d // 4) // N_GROUPS
    ctx_row = batch

    w_in_b = w_in.astype(BF16)
    w_out_b = w_out.astype(BF16)
    w_gate_b = w_gate.astype(BF16)
    w_up_b = w_up.astype(BF16)
    w_down_b = w_down.astype(BF16)
    g1 = jnp.broadcast_to(g_norm1[:, None, :], (depth, 2, d))
    g2 = jnp.broadcast_to(g_norm2[:, None, :], (depth, 2, d))
    gf = jnp.broadcast_to(g_final[None, :], (2, d))

    cs = jnp.concatenate([c, c_ctx[None, :], jnp.zeros((N_MOD_ROWS - batch - 1, d), F32)], axis=0)
    mods = _ada_call(cs, w_ada, b_ada).reshape(depth, N_MOD_ROWS, 1, 6 * d)

    cos_c, sin_c = _dft_tables(cw)
    ccsc = jnp.concatenate([cos_c, sin_c], axis=1).astype(BF16)
    use_fft = n_pos % (FFT_RADIX * 64) == 0 and n_pos // FFT_RADIX >= LANES
    if use_fft:
        ccsc_neg = jnp.concatenate([cos_c, -sin_c], axis=1).astype(BF16)
        fsub, twr, twi = _fft_tables(n_pos)
        latent_fourier = lambda pp: _fourier_fft_call(pp, ccsc_neg, fsub, twr, twi, f_block)
    else:
        cos_p, sin_p = _dft_tables(n_pos)
        cpsp = jnp.concatenate([cos_p, -sin_p], axis=1).astype(BF16)
        latent_fourier = lambda pp: _fourier_call(pp, ccsc, cpsp, f_block)
    cos_x, sin_x = _dft_tables(n_ctx)
    cpsp_ctx = jnp.concatenate([cos_x, -sin_x], axis=1).astype(BF16)
    rope_cos, rope_sin = _rope_tables(n_pos)
    mlp_bias = jnp.broadcast_to(b_s[:, :, :, None], b_s.shape + (cw,))
    off = _proj_offsets(d // 4)
    f_block = off["fourier"] // (N_GROUPS * cw)

    def flat(a):
        return a.reshape(1, -1, a.shape[-1])

    def unflat(a):
        return a.reshape(batch, n_ctx, a.shape[-1])

    xc = flat(ctx)
    for layer in range(depth):
        last = layer == depth - 1
        if last:
            w_kv = w_in_b[layer:layer + 1, :, off["k"]:off["u"]]
            pc = unflat(_norm_in_call(xc, mods, g1, w_kv, layer, 0, ctx_row, tn=256))
            kc_block = 0
        else:
            pc_flat = _norm_in_call(xc, mods, g1, w_in_b, layer, layer, ctx_row, tn=768)
            pc = unflat(pc_flat)
            kc_block = off["k"] // KV_WIDTH
            ys_c = (_conv_call(pc_flat, w_conv, layer, n_ctx),
                    flat(_fourier_call(pc, ccsc, cpsp_ctx, f_block)),
                    flat(_ctx_attn_call(pc, sink, layer)),
                    _chunk_mlp_call(pc_flat, w_s, mlp_bias, layer))
            xc_new = _out_proj_call(ys_c, xc, mods, w_out_b, layer, ctx_row)
            xc_next = _swiglu_call(xc_new, mods, g2, w_gate_b, w_up_b, w_down_b, gf, layer, ctx_row, False)
        p = _norm_in_call(x, mods, g1, w_in_b, layer, layer, None, tn=768)
        ys = (_conv_call(p, w_conv, layer, n_pos),
              latent_fourier(p),
              _latent_attn_call(p, pc, kc_block, sink, rope_cos, rope_sin, layer),
              _chunk_mlp_call(p, w_s, mlp_bias, layer))
        x = _out_proj_call(ys, x, mods, w_out_b, layer, None)
        x = _swiglu_call(x, mods, g2, w_gate_b, w_up_b, w_down_b, gf, layer, None, last)
        if not last:
            xc = xc_next
    return x
```
